```python
import math
import jax, jax.numpy as jnp
from jax import lax
import numpy as np

D_MODEL = 2048
BATCH = 2
SEQ = 8192
DEPTH = 4

N_EVEN = (DEPTH + 1) // 2
N_ODD = DEPTH // 2
ROPE_THETA = 500000.0
Q_BLOCK = 128
LN_EPS = 1e-5
RMS_EPS = 1e-6

MLA_HEADS = 8
MLA_NOPE = 128
MLA_ROPE = 64
MLA_V = 128
MLA_Q_RANK = 512
MLA_KV_RANK = 256
DIFF_HEADS = 8
DIFF_QK = 64
DIFF_V = 128
DIFF_ROT = DIFF_QK // 4
AB_IN = MLA_Q_RANK + MLA_KV_RANK + MLA_ROPE + 4 * DIFF_HEADS * DIFF_QK + DIFF_HEADS * DIFF_V
AB_OUT = MLA_HEADS * MLA_V + DIFF_HEADS * DIFF_V
SWA_Q_HEADS = 32
SWA_KV_HEADS = 4
SWA_GROUP = SWA_Q_HEADS // SWA_KV_HEADS
SWA_HD = 64
SWA_ROT = SWA_HD // 4
WINDOW = 128
C_IN = (SWA_Q_HEADS + 2 * SWA_KV_HEADS) * SWA_HD
C_OUT = SWA_Q_HEADS * SWA_HD
N_EXPERTS = 64
TOP_K = 8
D_EXPERT = 256
ROUTED_SCALE = 2.5
EXPERT_BLOCK = 128
DEEPNORM_ALPHA = (2 * DEPTH) ** 0.25
DEEPNORM_BETA = (8 * DEPTH) ** -0.25

kernel_name = "hybrid_mla_diff_swa_moe_encoder"


def layer_norm(x, g, b):
    xf = x.astype(jnp.float32)
    mu = xf.mean(-1, keepdims=True)
    var = jnp.square(xf - mu).mean(-1, keepdims=True)
    return ((xf - mu) * lax.rsqrt(var + LN_EPS) * g.astype(jnp.float32) + b.astype(jnp.float32)).astype(x.dtype)


def rms_norm(x, g):
    xf = x.astype(jnp.float32)
    return (xf * lax.rsqrt(jnp.mean(xf * xf, -1, keepdims=True) + RMS_EPS) * g.astype(jnp.float32)).astype(x.dtype)


def rope_tables(positions, rot):
    inv = ROPE_THETA ** (-jnp.arange(0, rot, 2, dtype=jnp.float32) / rot)
    ang = positions.astype(jnp.float32)[..., None] * inv
    return jnp.cos(ang)[:, :, None, :], jnp.sin(ang)[:, :, None, :]


def apply_rope(x, cos, sin):
    h = x.shape[-1] // 2
    x1 = x[..., :h].astype(jnp.float32)
    x2 = x[..., h:].astype(jnp.float32)
    return jnp.concatenate([x1 * cos - x2 * sin, x2 * cos + x1 * sin], -1).astype(x.dtype)


def partial_rope(x, cos, sin, rot):
    return jnp.concatenate([apply_rope(x[..., :rot], cos, sin), x[..., rot:]], -1)


def to_blocks(t):
    b, s = t.shape[:2]
    return jnp.moveaxis(t.reshape(b, s // Q_BLOCK, Q_BLOCK, *t.shape[2:]), 1, 0)


def from_blocks(t):
    n, b, qb = t.shape[:3]
    return jnp.moveaxis(t, 0, 1).reshape(b, n * qb, *t.shape[3:])


def softmax_f32(s):
    return jax.nn.softmax(s.astype(jnp.float32), axis=-1)


def dense_attention(q, k, v, scale):
    def block(qb):
        s = jnp.einsum('bqhd,bkhd->bhqk', qb, k) * scale
        p = softmax_f32(s).astype(v.dtype)
        return jnp.einsum('bhqk,bkhd->bqhd', p, v)
    return from_blocks(lax.map(block, to_blocks(q)))


def differential_attention(q1, q2, k1, k2, v, lam):
    scale = DIFF_QK ** -0.5
    def block(qs):
        qb1, qb2 = qs
        a1 = softmax_f32(jnp.einsum('bqhd,bkhd->bhqk', qb1, k1) * scale)
        a2 = softmax_f32(jnp.einsum('bqhd,bkhd->bhqk', qb2, k2) * scale)
        p = (a1 - lam * a2).astype(v.dtype)
        return jnp.einsum('bhqk,bkhd->bqhd', p, v)
    return from_blocks(lax.map(block, (to_blocks(q1), to_blocks(q2))))


def window_sink_attention(q, k, v, sink):
    b, s = q.shape[:2]
    nblk = s // Q_BLOCK
    span = Q_BLOCK + 2 * WINDOW
    pad = ((0, 0), (WINDOW, WINDOW), (0, 0), (0, 0))
    kp = jnp.pad(k, pad)
    vp = jnp.pad(v, pad)
    qi = jnp.arange(Q_BLOCK)[:, None]
    kj = jnp.arange(span)[None, :]
    in_window = jnp.abs(kj - WINDOW - qi) <= WINDOW
    sink_f = sink.astype(jnp.float32).reshape(1, SWA_KV_HEADS, SWA_GROUP, 1, 1)
    scale = SWA_HD ** -0.5

    def block(args):
        n, qb = args
        start = n * Q_BLOCK
        kb = lax.dynamic_slice_in_dim(kp, start, span, axis=1)
        vb = lax.dynamic_slice_in_dim(vp, start, span, axis=1)
        kpos = start - WINDOW + kj
        valid = in_window & (kpos >= 0) & (kpos < s)
        sc = jnp.einsum('bqhgd,bkhd->bhgqk', qb, kb).astype(jnp.float32) * scale
        sc = jnp.where(valid, sc, -jnp.inf)
        m = jnp.maximum(sc.max(-1, keepdims=True), sink_f)
        p = jnp.exp(sc - m)
        p = p / (p.sum(-1, keepdims=True) + jnp.exp(sink_f - m))
        return jnp.einsum('bhgqk,bkhd->bqhgd', p.astype(vb.dtype), vb)

    return from_blocks(lax.map(block, (jnp.arange(nblk, dtype=jnp.int32), to_blocks(q))))


def mla_diff_mixer(x, layer, cos64, sin64, cos16, sin16, w_in, q_norm, w_uq, kv_norm, w_ukv,
                   lq1, lk1, lq2, lk2, sub_norm, w_out):
    b, s, _ = x.shape
    proj = x @ w_in
    o1 = MLA_Q_RANK
    o2 = o1 + MLA_KV_RANK
    o3 = o2 + MLA_ROPE
    o4 = o3 + 4 * DIFF_HEADS * DIFF_QK
    c_q, c_kv, k_r, dq, dv = jnp.split(proj, [o1, o2, o3, o4], axis=-1)

    q = (rms_norm(c_q, q_norm) @ w_uq).reshape(b, s, MLA_HEADS, MLA_NOPE + MLA_ROPE)
    q = jnp.concatenate([q[..., :MLA_NOPE], apply_rope(q[..., MLA_NOPE:], cos64, sin64)], -1)
    kv = (rms_norm(c_kv, kv_norm) @ w_ukv).reshape(b, s, MLA_HEADS, MLA_NOPE + MLA_V)
    k_nope, v_mla = kv[..., :MLA_NOPE], kv[..., MLA_NOPE:]
    k_rope = apply_rope(k_r[:, :, None, :], cos64, sin64)
    k = jnp.concatenate([k_nope, jnp.broadcast_to(k_rope, (b, s, MLA_HEADS, MLA_ROPE))], -1)
    o_mla = dense_attention(q, k, v_mla, (MLA_NOPE + MLA_ROPE) ** -0.5)

    dq = dq.reshape(b, s, DIFF_HEADS, 4, DIFF_QK)
    q1, q2, k1, k2 = [partial_rope(dq[:, :, :, i], cos16, sin16, DIFF_ROT) for i in range(4)]
    v_d = dv.reshape(b, s, DIFF_HEADS, DIFF_V)
    lam_init = 0.8 - 0.6 * math.exp(-0.3 * layer)
    lam = (jnp.exp(jnp.sum(lq1.astype(jnp.float32) * lk1.astype(jnp.float32)))
           - jnp.exp(jnp.sum(lq2.astype(jnp.float32) * lk2.astype(jnp.float32))) + lam_init)
    o_diff = differential_attention(q1, q2, k1, k2, v_d, lam)
    o_diff = rms_norm(o_diff, sub_norm) * (1.0 - lam_init)

    o = jnp.concatenate([o_mla.reshape(b, s, -1), o_diff.reshape(b, s, -1)], -1)
    return o @ w_out


def swa_mixer(x, cos16, sin16, w_in, sink, w_out):
    b, s, _ = x.shape
    proj = x @ w_in
    q, k, v = jnp.split(proj, [C_OUT, C_OUT + SWA_KV_HEADS * SWA_HD], axis=-1)
    q = partial_rope(q.reshape(b, s, SWA_Q_HEADS, SWA_HD), cos16, sin16, SWA_ROT)
    k = partial_rope(k.reshape(b, s, SWA_KV_HEADS, SWA_HD), cos16, sin16, SWA_ROT)
    v = v.reshape(b, s, SWA_KV_HEADS, SWA_HD)
    q = q.reshape(b, s, SWA_KV_HEADS, SWA_GROUP, SWA_HD)
    o = window_sink_attention(q, k, v, sink)
    return o.reshape(b, s, C_OUT) @ w_out


def moe_ffn(x, router_w, router_bias, w_gate, w_up, w_down, s_gate, s_up, s_down):
    b, s, d = x.shape
    t = b * s
    xf = x.reshape(t, d)
    scores = jax.nn.sigmoid((xf @ router_w).astype(jnp.float32))
    _, idx = lax.top_k(scores + router_bias.astype(jnp.float32), TOP_K)
    gate = jnp.take_along_axis(scores, idx, axis=-1)
    gate = gate / gate.sum(-1, keepdims=True) * ROUTED_SCALE

    a = t * TOP_K
    flat_e = idx.reshape(-1)
    order = jnp.argsort(flat_e)
    sorted_e = flat_e[order]
    sizes = jnp.bincount(flat_e, length=N_EXPERTS)
    starts = jnp.cumsum(sizes) - sizes
    psizes = (sizes + EXPERT_BLOCK - 1) // EXPERT_BLOCK * EXPERT_BLOCK
    pends = jnp.cumsum(psizes)
    pstarts = pends - psizes
    dest = pstarts[sorted_e] + jnp.arange(a) - starts[sorted_e]
    n_blocks = -(-a // EXPERT_BLOCK) + N_EXPERTS
    cap = n_blocks * EXPERT_BLOCK
    slot_tok = jnp.full((cap,), t, jnp.int32).at[dest].set((order // TOP_K).astype(jnp.int32))
    slot_gate = jnp.zeros((cap,), jnp.float32).at[dest].set(gate.reshape(-1)[order])
    block_e = jnp.minimum(
        jnp.searchsorted(pends, jnp.arange(n_blocks) * EXPERT_BLOCK, side='right'), N_EXPERTS - 1)

    xpad = jnp.concatenate([xf, jnp.zeros((1, d), xf.dtype)], 0)

    def expert_block(acc, blk):
        tok_b, gate_b, e = blk
        xb = xpad[tok_b]
        h = jax.nn.silu(xb @ w_gate[e]) * (xb @ w_up[e])
        yb = (h @ w_down[e]) * gate_b[:, None].astype(xb.dtype)
        return acc.at[tok_b].add(yb), None

    acc, _ = lax.scan(expert_block, jnp.zeros_like(xpad),
                      (slot_tok.reshape(n_blocks, EXPERT_BLOCK),
                       slot_gate.reshape(n_blocks, EXPERT_BLOCK), block_e))
    shared = (jax.nn.silu(xf @ s_gate) * (xf @ s_up)) @ s_down
    return (acc[:t] + shared).reshape(b, s, d)


def setup_inputs(seed: int = 0) -> dict:
    key = jax.random.key(seed)
    ks = jax.random.split(key, 28)
    f32 = jnp.float32

    def nrm(k, shape, scale):
        return jax.random.normal(k, shape, f32) * scale

    def gain(k, shape):
        return 1.0 + 0.02 * jax.random.normal(k, shape, f32)

    beta = DEEPNORM_BETA
    return {
        "x": jax.random.normal(ks[0], (BATCH, SEQ, D_MODEL), f32),
        "positions": jnp.broadcast_to(jnp.arange(SEQ, dtype=jnp.int32), (BATCH, SEQ)),
        "ab_w_in": nrm(ks[1], (N_EVEN, D_MODEL, AB_IN), D_MODEL ** -0.5),
        "mla_q_norm": gain(ks[2], (N_EVEN, MLA_Q_RANK)),
        "mla_w_uq": nrm(ks[3], (N_EVEN, MLA_Q_RANK, MLA_HEADS * (MLA_NOPE + MLA_ROPE)), MLA_Q_RANK ** -0.5),
        "mla_kv_norm": gain(ks[4], (N_EVEN, MLA_KV_RANK)),
        "mla_w_ukv": nrm(ks[5], (N_EVEN, MLA_KV_RANK, MLA_HEADS * (MLA_NOPE + MLA_V)), MLA_KV_RANK ** -0.5),
        "diff_lambda_q1": nrm(ks[6], (N_EVEN, DIFF_QK), 0.1),
        "diff_lambda_k1": nrm(ks[7], (N_EVEN, DIFF_QK), 0.1),
        "diff_lambda_q2": nrm(ks[8], (N_EVEN, DIFF_QK), 0.1),
        "diff_lambda_k2": nrm(ks[9], (N_EVEN, DIFF_QK), 0.1),
        "diff_sub_norm": gain(ks[10], (N_EVEN, DIFF_V)),
        "ab_w_out": nrm(ks[11], (N_EVEN, AB_OUT, D_MODEL), AB_OUT ** -0.5 * beta),
        "swa_w_in": nrm(ks[12], (N_ODD, D_MODEL, C_IN), D_MODEL ** -0.5),
        "swa_sink": nrm(ks[13], (N_ODD, SWA_Q_HEADS), 1.0),
        "swa_w_out": nrm(ks[14], (N_ODD, C_OUT, D_MODEL), C_OUT ** -0.5 * beta),
        "mix_ln_g": gain(ks[15], (DEPTH, D_MODEL)),
        "mix_ln_b": nrm(ks[16], (DEPTH, D_MODEL), 0.02),
        "ffn_ln_g": gain(ks[17], (DEPTH, D_MODEL)),
        "ffn_ln_b": nrm(ks[18], (DEPTH, D_MODEL), 0.02),
        "router_w": nrm(ks[19], (DEPTH, D_MODEL, N_EXPERTS), D_MODEL ** -0.5),
        "router_bias": nrm(ks[20], (DEPTH, N_EXPERTS), 0.01),
        "exp_w_gate": nrm(ks[21], (DEPTH, N_EXPERTS, D_MODEL, D_EXPERT), D_MODEL ** -0.5),
        "exp_w_up": nrm(ks[22], (DEPTH, N_EXPERTS, D_MODEL, D_EXPERT), D_MODEL ** -0.5),
        "exp_w_down": nrm(ks[23], (DEPTH, N_EXPERTS, D_EXPERT, D_MODEL), D_EXPERT ** -0.5 * beta),
        "shared_w_gate": nrm(ks[24], (DEPTH, D_MODEL, D_EXPERT), D_MODEL ** -0.5),
        "shared_w_up": nrm(ks[25], (DEPTH, D_MODEL, D_EXPERT), D_MODEL ** -0.5),
        "shared_w_down": nrm(ks[26], (DEPTH, D_EXPERT, D_MODEL), D_EXPERT ** -0.5 * beta),
    }


def reference(x, positions, ab_w_in, mla_q_norm, mla_w_uq, mla_kv_norm, mla_w_ukv,
              diff_lambda_q1, diff_lambda_k1, diff_lambda_q2, diff_lambda_k2, diff_sub_norm,
              ab_w_out, swa_w_in, swa_sink, swa_w_out, mix_ln_g, mix_ln_b, ffn_ln_g, ffn_ln_b,
              router_w, router_bias, exp_w_gate, exp_w_up, exp_w_down,
              shared_w_gate, shared_w_up, shared_w_down):
    cos64, sin64 = rope_tables(positions, MLA_ROPE)
    cos16, sin16 = rope_tables(positions, DIFF_ROT)
    h = x
    for layer in range(DEPTH):
        i = layer // 2
        if layer % 2 == 0:
            mix = mla_diff_mixer(h, layer, cos64, sin64, cos16, sin16, ab_w_in[i], mla_q_norm[i],
                                 mla_w_uq[i], mla_kv_norm[i], mla_w_ukv[i],
                                 diff_lambda_q1[i], diff_lambda_k1[i], diff_lambda_q2[i],
                                 diff_lambda_k2[i], diff_sub_norm[i], ab_w_out[i])
        else:
            mix = swa_mixer(h, cos16, sin16, swa_w_in[i], swa_sink[i], swa_w_out[i])
        h = layer_norm(DEEPNORM_ALPHA * h + mix, mix_ln_g[layer], mix_ln_b[layer])
        ffn = moe_ffn(h, router_w[layer], router_bias[layer], exp_w_gate[layer], exp_w_up[layer],
                      exp_w_down[layer], shared_w_gate[layer], shared_w_up[layer], shared_w_down[layer])
        h = layer_norm(DEEPNORM_ALPHA * h + ffn, ffn_ln_g[layer], ffn_ln_b[layer])
    return h
```

```python
import functools
import math

import numpy as np
import jax
import jax.numpy as jnp
from jax import lax
from jax.experimental import pallas as pl
from jax.experimental.pallas import tpu as pltpu

F32 = jnp.float32
BF16 = jnp.bfloat16
I32 = jnp.int32

D_MODEL = 2048
DEPTH = 4
ROPE_THETA = 500000.0
LN_EPS = 1e-5
RMS_EPS = 1e-6
MLA_HEADS = 8
MLA_NOPE = 128
MLA_ROPE = 64
MLA_V = 128
MLA_Q_RANK = 512
MLA_KV_RANK = 256
DIFF_HEADS = 8
DIFF_QK = 64
DIFF_V = 128
DIFF_ROT = DIFF_QK // 4
SWA_Q_HEADS = 32
SWA_KV_HEADS = 4
SWA_GROUP = SWA_Q_HEADS // SWA_KV_HEADS
SWA_HD = 64
WINDOW = 128
N_EXPERTS = 64
TOP_K = 8
D_EXPERT = 256
ROUTED_SCALE = 2.5
DEEPNORM_ALPHA = (2 * DEPTH) ** 0.25

LANES = 128
MLA_QK_PAD = 256
VMEM_LIMIT = 48 << 20


def _params(sem):
    return pltpu.CompilerParams(dimension_semantics=sem, vmem_limit_bytes=VMEM_LIMIT)


def _dot(a, b):
    return jnp.dot(a, b, preferred_element_type=F32)


def _dot_nt(a, b):
    return lax.dot_general(a, b, (((1,), (1,)), ((), ())), preferred_element_type=F32)


def _tables_kernel(pos_ref, c_ref, out_ref):
    pos = pos_ref[...]
    a64 = pos * c_ref[0:1, :]
    a16 = pos * c_ref[3:4, :]
    s64 = jnp.sin(a64)
    s16 = jnp.sin(a16)
    out_ref[0] = jnp.cos(a64)
    out_ref[1] = s64 * c_ref[1:2, :]
    out_ref[2] = s64 * c_ref[2:3, :]
    out_ref[3] = jnp.cos(a16)
    out_ref[4] = s16 * c_ref[4:5, :]
    out_ref[5] = s16 * c_ref[5:6, :]


def _rope_consts():
    j = np.arange(LANES) % 64
    inv64 = ROPE_THETA ** (-jnp.arange(0, MLA_ROPE, 2, dtype=F32) / MLA_ROPE)
    inv16 = ROPE_THETA ** (-jnp.arange(0, DIFF_ROT, 2, dtype=F32) / DIFF_ROT)
    f64 = inv64[j % 32]
    f16 = jnp.where(j < 16, inv16[j % 8], 0.0)
    rows = [
        f64,
        jnp.asarray(np.where(j < 32, -1.0, 0.0), F32),
        jnp.asarray(np.where(j >= 32, 1.0, 0.0), F32),
        f16,
        jnp.asarray(np.where(j < 8, -1.0, 0.0), F32),
        jnp.asarray(np.where((j >= 8) & (j < 16), 1.0, 0.0), F32),
        jnp.zeros((LANES,), F32),
        jnp.zeros((LANES,), F32),
    ]
    return jnp.stack(rows).astype(F32)


def _rope_tables(positions):
    t = positions.size
    pos = jnp.broadcast_to(positions.reshape(t, 1).astype(F32), (t, LANES))
    tm = min(512, t)
    return pl.pallas_call(
        _tables_kernel,
        grid=(t // tm,),
        in_specs=[pl.BlockSpec((tm, LANES), lambda i: (i, 0)), pl.BlockSpec((8, LANES), lambda i: (0, 0))],
        out_specs=pl.BlockSpec((6, tm, LANES), lambda i: (0, i, 0)),
        out_shape=jax.ShapeDtypeStruct((6, t, LANES), F32),
        compiler_params=_params(("parallel",)),
        name="rope_tables",
    )(pos, _rope_consts())


def _rope_tile(x, c, sa, sb, half):
    return x * c + pltpu.roll(x, LANES - half, 1) * sa + pltpu.roll(x, half, 1) * sb


def _mm_kernel(x_ref, w_ref, o_ref):
    o_ref[...] = _dot(x_ref[...], w_ref[...]).astype(o_ref.dtype)


def _matmul(x, w, name, tm=512, tn=1024):
    m, k = x.shape
    n = w.shape[1]
    tm = min(tm, m)
    tn = min(tn, n)
    return pl.pallas_call(
        _mm_kernel,
        grid=(n // tn, m // tm),
        in_specs=[pl.BlockSpec((tm, k), lambda j, i: (i, 0)), pl.BlockSpec((k, tn), lambda j, i: (0, j))],
        out_specs=pl.BlockSpec((tm, tn), lambda j, i: (i, j)),
        out_shape=jax.ShapeDtypeStruct((m, n), BF16),
        compiler_params=_params(("parallel", "parallel")),
        name=name,
    )(x, w)


def _rms(x, g):
    return x * lax.rsqrt(jnp.mean(x * x, -1, keepdims=True) + RMS_EPS) * g


def _mla_prep_kernel(cq_ref, ckv_ref, tab_ref, qg_ref, kvg_ref, wqn_ref, wqr_ref, wkn_ref, wv_ref,
                     q_ref, k_ref, v_ref, *, scale):
    tm = cq_ref.shape[0]
    cq = cq_ref[...].astype(F32)
    ckv_all = ckv_ref[...].astype(F32)
    ckv = ckv_all[:, :MLA_KV_RANK]
    kr = ckv_all[:, MLA_KV_RANK:MLA_KV_RANK + LANES]
    cqn = _rms(cq, qg_ref[...]).astype(BF16)
    ckvn = _rms(ckv, kvg_ref[...]).astype(BF16)
    qn = _dot(cqn, wqn_ref[...]) * scale
    qr = _dot(cqn, wqr_ref[...]) * scale
    kn = _dot(ckvn, wkn_ref[...])
    vv = _dot(ckvn, wv_ref[...])
    c, sa, sb = tab_ref[0], tab_ref[1], tab_ref[2]
    lo = lax.broadcasted_iota(I32, (tm, LANES), 1) < MLA_ROPE
    half = MLA_ROPE // 2
    krr = jnp.where(lo, _rope_tile(kr, c, sa, sb, half), 0.0).astype(BF16)
    for j in range(MLA_HEADS // 2):
        r = _rope_tile(qr[:, LANES * j:LANES * (j + 1)], c, sa, sb, half)
        q_ref[0, 2 * j, :, LANES:] = jnp.where(lo, r, 0.0).astype(BF16)
        q_ref[0, 2 * j + 1, :, LANES:] = jnp.where(lo, pltpu.roll(r, MLA_ROPE, 1), 0.0).astype(BF16)
    for h in range(MLA_HEADS):
        sl = slice(LANES * h, LANES * (h + 1))
        q_ref[0, h, :, :LANES] = qn[:, sl].astype(BF16)
        k_ref[0, h, :, :LANES] = kn[:, sl].astype(BF16)
        k_ref[0, h, :, LANES:] = krr
        v_ref[0, h] = vv[:, sl].astype(BF16)


def _mla_prep(proj, tab, q_norm, kv_norm, wqn, wqr, wkn, wv, b, s, col_cq, col_ckv):
    tm = min(256, s)
    nt = s // tm
    h = MLA_HEADS
    row = lambda bi, i: bi * nt + i
    const = lambda bi, i: (0, 0)
    return pl.pallas_call(
        functools.partial(_mla_prep_kernel, scale=(MLA_NOPE + MLA_ROPE) ** -0.5),
        grid=(b, nt),
        in_specs=[
            pl.BlockSpec((tm, MLA_Q_RANK), lambda bi, i: (row(bi, i), col_cq)),
            pl.BlockSpec((tm, 512), lambda bi, i: (row(bi, i), col_ckv)),
            pl.BlockSpec((3, tm, LANES), lambda bi, i: (0, row(bi, i), 0)),
            pl.BlockSpec((1, MLA_Q_RANK), const),
            pl.BlockSpec((1, MLA_KV_RANK), const),
            pl.BlockSpec(wqn.shape, const),
            pl.BlockSpec(wqr.shape, const),
            pl.BlockSpec(wkn.shape, const),
            pl.BlockSpec(wv.shape, const),
        ],
        out_specs=[
            pl.BlockSpec((1, h, tm, MLA_QK_PAD), lambda bi, i: (bi, 0, i, 0)),
            pl.BlockSpec((1, h, tm, MLA_QK_PAD), lambda bi, i: (bi, 0, i, 0)),
            pl.BlockSpec((1, h, tm, MLA_V), lambda bi, i: (bi, 0, i, 0)),
        ],
        out_shape=[
            jax.ShapeDtypeStruct((b, h, s, MLA_QK_PAD), BF16),
            jax.ShapeDtypeStruct((b, h, s, MLA_QK_PAD), BF16),
            jax.ShapeDtypeStruct((b, h, s, MLA_V), BF16),
        ],
        compiler_params=_params(("parallel", "parallel")),
        name="mla_prep",
    )(proj, proj, tab, q_norm, kv_norm, wqn, wqr, wkn, wv)


def _diff_prep_kernel(dq_ref, tab_ref, o_ref):
    c, sa, sb = tab_ref[0], tab_ref[1], tab_ref[2]
    n = DIFF_HEADS * DIFF_QK
    for i in range(4):
        for j in range(n // LANES):
            x = dq_ref[:, n * i + LANES * j:n * i + LANES * (j + 1)].astype(F32)
            r = _rope_tile(x, c, sa, sb, DIFF_ROT // 2)
            if i < 2:
                r = r * (DIFF_QK ** -0.5)
            o_ref[i, 0, :, LANES * j:LANES * (j + 1)] = r.astype(BF16)


def _diff_prep(proj, tab, b, s):
    tm = min(256, s)
    nt = s // tm
    n = DIFF_HEADS * DIFF_QK
    return pl.pallas_call(
        _diff_prep_kernel,
        grid=(b, nt),
        in_specs=[
            pl.BlockSpec((tm, 4 * n), lambda bi, i: (bi * nt + i, 0)),
            pl.BlockSpec((3, tm, LANES), lambda bi, i: (1, bi * nt + i, 0)),
        ],
        out_specs=pl.BlockSpec((4, 1, tm, n), lambda bi, i: (0, bi, i, 0)),
        out_shape=jax.ShapeDtypeStruct((4, b, s, n), BF16),
        compiler_params=_params(("parallel", "parallel")),
        name="diff_prep",
    )(proj, tab)


def _flash_step(q, k, v, m_scr, l_scr, acc_scr):
    s = _dot_nt(q, k)
    m_prev = m_scr[...]
    m_new = jnp.maximum(m_prev, jnp.max(s, axis=1, keepdims=True))
    alpha = jnp.exp(m_prev - m_new)
    p = jnp.exp(s - m_new)
    l_scr[...] = alpha * l_scr[...] + jnp.sum(p, axis=1, keepdims=True)
    acc_scr[...] = alpha * acc_scr[...] + _dot(p.astype(BF16), v)
    m_scr[...] = m_new


def _mla_attn_kernel(q_ref, k_ref, v_ref, o_ref, m_scr, l_scr, acc_scr):
    j = pl.program_id(3)

    @pl.when(j == 0)
    def _():
        m_scr[...] = jnp.full(m_scr.shape, -jnp.inf, F32)
        l_scr[...] = jnp.zeros(l_scr.shape, F32)
        acc_scr[...] = jnp.zeros(acc_scr.shape, F32)

    _flash_step(q_ref[0, 0], k_ref[0, 0], v_ref[0, 0], m_scr, l_scr, acc_scr)

    @pl.when(j == pl.num_programs(3) - 1)
    def _():
        o_ref[0] = (acc_scr[...] / l_scr[...]).astype(o_ref.dtype)


def _mla_attn(q, k, v, tq=512, tk=512):
    b, h, s, _ = q.shape
    tq = min(tq, s)
    tk = min(tk, s)
    return pl.pallas_call(
        _mla_attn_kernel,
        grid=(b, h, s // tq, s // tk),
        in_specs=[
            pl.BlockSpec((1, 1, tq, MLA_QK_PAD), lambda bi, hi, i, j: (bi, hi, i, 0)),
            pl.BlockSpec((1, 1, tk, MLA_QK_PAD), lambda bi, hi, i, j: (bi, hi, j, 0)),
            pl.BlockSpec((1, 1, tk, MLA_V), lambda bi, hi, i, j: (bi, hi, j, 0)),
        ],
        out_specs=pl.BlockSpec((1, tq, MLA_V), lambda bi, hi, i, j: (bi, i, hi)),
        out_shape=jax.ShapeDtypeStruct((b, s, h * MLA_V), BF16),
        scratch_shapes=[pltpu.VMEM((tq, 1), F32), pltpu.VMEM((tq, 1), F32), pltpu.VMEM((tq, MLA_V), F32)],
        compiler_params=_params(("parallel", "parallel", "parallel", "arbitrary")),
        name="mla_attn",
    )(q, k, v)


def _diff_attn_kernel(q1_ref, q2_ref, k1_ref, k2_ref, v_ref, lam_ref, g_ref, o_ref,
                      m1, l1, a1, m2, l2, a2, *, lam_init):
    hi = pl.program_id(1)
    j = pl.program_id(3)
    tq = q1_ref.shape[2]

    @pl.when(j == 0)
    def _():
        for m_scr, l_scr, a_scr in ((m1, l1, a1), (m2, l2, a2)):
            m_scr[...] = jnp.full(m_scr.shape, -jnp.inf, F32)
            l_scr[...] = jnp.zeros(l_scr.shape, F32)
            a_scr[...] = jnp.zeros(a_scr.shape, F32)

    mine = (lax.broadcasted_iota(I32, (tq, LANES), 1) // DIFF_QK) == (hi % 2)
    v = v_ref[...]
    zero = jnp.zeros((), BF16)
    _flash_step(jnp.where(mine, q1_ref[0, 0], zero), k1_ref[0, 0], v, m1, l1, a1)
    _flash_step(jnp.where(mine, q2_ref[0, 0], zero), k2_ref[0, 0], v, m2, l2, a2)

    @pl.when(j == pl.num_programs(3) - 1)
    def _():
        lp = lam_ref[...]
        lam = (jnp.exp(jnp.sum(lp[0:1] * lp[1:2], axis=1, keepdims=True))
               - jnp.exp(jnp.sum(lp[2:3] * lp[3:4], axis=1, keepdims=True)) + lam_init)
        o = a1[...] / l1[...] - lam * (a2[...] / l2[...])
        o_ref[0] = (_rms(o, g_ref[...]) * (1.0 - lam_init)).astype(o_ref.dtype)


def _diff_attn(dqk, proj, col_v, lam_p, sub_norm, lam_init, b, s, tq=512, tk=512):
    h = DIFF_HEADS
    tq = min(tq, s)
    tk = min(tk, s)
    nk = s // tk
    qspec = lambda which: pl.BlockSpec((1, 1, tq, LANES), lambda bi, hi, i, j: (which, bi, i, hi // 2))
    kspec = lambda which: pl.BlockSpec((1, 1, tk, LANES), lambda bi, hi, i, j: (which, bi, j, hi // 2))
    const = lambda bi, hi, i, j: (0, 0)
    return pl.pallas_call(
        functools.partial(_diff_attn_kernel, lam_init=lam_init),
        grid=(b, h, s // tq, nk),
        in_specs=[
            qspec(0), qspec(1), kspec(2), kspec(3),
            pl.BlockSpec((tk, DIFF_V), lambda bi, hi, i, j: (bi * nk + j, col_v + hi)),
            pl.BlockSpec((4, DIFF_QK), const),
            pl.BlockSpec((1, DIFF_V), const),
        ],
        out_specs=pl.BlockSpec((1, tq, DIFF_V), lambda bi, hi, i, j: (bi, i, hi)),
        out_shape=jax.ShapeDtypeStruct((b, s, h * DIFF_V), BF16),
        scratch_shapes=[pltpu.VMEM((tq, 1), F32), pltpu.VMEM((tq, 1), F32), pltpu.VMEM((tq, DIFF_V), F32)] * 2,
        compiler_params=_params(("parallel", "parallel", "parallel", "arbitrary")),
        name="diff_attn",
    )(dqk, dqk, dqk, dqk, proj, lam_p, sub_norm)


def _swa_prep_kernel(x_ref, tab_ref, o_ref):
    c, sa, sb = tab_ref[0], tab_ref[1], tab_ref[2]
    nq = SWA_Q_HEADS * SWA_HD // LANES
    for j in range(x_ref.shape[1] // LANES):
        x = x_ref[:, LANES * j:LANES * (j + 1)].astype(F32)
        r = _rope_tile(x, c, sa, sb, DIFF_ROT // 2)
        if j < nq:
            r = r * (SWA_HD ** -0.5)
        o_ref[:, LANES * j:LANES * (j + 1)] = r.astype(BF16)


def _swa_prep(proj, tab, n_cols):
    t = proj.shape[0]
    tm = min(256, t)
    return pl.pallas_call(
        _swa_prep_kernel,
        grid=(t // tm,),
        in_specs=[
            pl.BlockSpec((tm, n_cols), lambda i: (i, 0)),
            pl.BlockSpec((3, tm, LANES), lambda i: (1, i, 0)),
        ],
        out_specs=pl.BlockSpec((tm, n_cols), lambda i: (i, 0)),
        out_shape=jax.ShapeDtypeStruct((t, n_cols), BF16),
        compiler_params=_params(("parallel",)),
        name="swa_prep",
    )(proj, tab)


def _swa_attn_kernel(sink_ref, q_ref, kp_ref, km_ref, kn_ref, vp_ref, vm_ref, vn_ref, o_ref, *, seq):
    g = pl.program_id(1)
    i = pl.program_id(2)
    tq = q_ref.shape[0]
    span = tq + 2 * WINDOW
    k = jnp.concatenate([kp_ref[...], km_ref[...], kn_ref[...]], axis=0)
    v = jnp.concatenate([vp_ref[...], vm_ref[...], vn_ref[...]], axis=0)
    qpos = i * tq + lax.broadcasted_iota(I32, (tq, span), 0)
    kpos = i * tq - WINDOW + lax.broadcasted_iota(I32, (tq, span), 1)
    valid = (jnp.abs(kpos - qpos) <= WINDOW) & (kpos >= 0) & (kpos < seq)
    lane = lax.broadcasted_iota(I32, (tq, LANES), 1)
    zero = jnp.zeros((), BF16)
    for pair in range(SWA_GROUP // 2):
        qp = q_ref[:, LANES * pair:LANES * (pair + 1)]
        outs = []
        for half in range(2):
            sink = sink_ref[g * SWA_GROUP + 2 * pair + half]
            qh = jnp.where((lane // SWA_HD) == half, qp, zero)
            sc = jnp.where(valid, _dot_nt(qh, k), -jnp.inf)
            m = jnp.maximum(jnp.max(sc, axis=1, keepdims=True), sink)
            p = jnp.exp(sc - m)
            denom = jnp.sum(p, axis=1, keepdims=True) + jnp.exp(sink - m)
            outs.append(_dot(p.astype(BF16), v) / denom)
        o_ref[:, LANES * pair:LANES * (pair + 1)] = jnp.where(lane < SWA_HD, outs[0], outs[1]).astype(o_ref.dtype)


def _swa_attn(qk, proj, sink, b, s, col_k, col_v, tq=256):
    tq = min(tq, s)
    nt = s // tq
    r = tq // WINDOW
    nwb = s // WINDOW
    gw = SWA_GROUP * SWA_HD
    row = lambda bi, i: bi * nt + i
    prev = lambda bi, i: bi * nwb + jnp.maximum(i * r - 1, 0)
    nxt = lambda bi, i: bi * nwb + jnp.minimum((i + 1) * r, nwb - 1)
    grid_spec = pltpu.PrefetchScalarGridSpec(
        num_scalar_prefetch=1,
        grid=(b, SWA_KV_HEADS, nt),
        in_specs=[
            pl.BlockSpec((tq, gw), lambda bi, g, i, sk: (row(bi, i), g)),
            pl.BlockSpec((WINDOW, LANES), lambda bi, g, i, sk: (prev(bi, i), col_k + g)),
            pl.BlockSpec((tq, LANES), lambda bi, g, i, sk: (row(bi, i), col_k + g)),
            pl.BlockSpec((WINDOW, LANES), lambda bi, g, i, sk: (nxt(bi, i), col_k + g)),
            pl.BlockSpec((WINDOW, LANES), lambda bi, g, i, sk: (prev(bi, i), col_v + g)),
            pl.BlockSpec((tq, LANES), lambda bi, g, i, sk: (row(bi, i), col_v + g)),
            pl.BlockSpec((WINDOW, LANES), lambda bi, g, i, sk: (nxt(bi, i), col_v + g)),
        ],
        out_specs=pl.BlockSpec((tq, gw), lambda bi, g, i, sk: (row(bi, i), g)),
    )
    return pl.pallas_call(
        functools.partial(_swa_attn_kernel, seq=s),
        grid_spec=grid_spec,
        out_shape=jax.ShapeDtypeStruct((b * s, SWA_Q_HEADS * SWA_HD), BF16),
        compiler_params=_params(("parallel", "parallel", "parallel")),
        name="swa_attn",
    )(sink, qk, qk, qk, qk, proj, proj, proj)


def _layer_norm(y, g, b):
    mu = jnp.mean(y, -1, keepdims=True)
    d = y - mu
    var = jnp.mean(d * d, -1, keepdims=True)
    return d * lax.rsqrt(var + LN_EPS) * g + b


def _out_ln_kernel(*refs, n_in):
    o_refs = refs[:n_in]
    w_refs = refs[n_in:2 * n_in]
    h_ref, g_ref, b_ref, hf_ref, hb_ref = refs[2 * n_in:]
    mix = _dot(o_refs[0][...], w_refs[0][...])
    for o_ref, w_ref in zip(o_refs[1:], w_refs[1:]):
        mix = mix + _dot(o_ref[...], w_ref[...])
    out = _layer_norm(DEEPNORM_ALPHA * h_ref[...] + mix, g_ref[...], b_ref[...])
    hf_ref[...] = out
    hb_ref[...] = out.astype(BF16)


def _out_ln(os_, ws, h, g, b):
    t, d = h.shape
    tm = min(256, t)
    n_in = len(os_)
    const = lambda i: (0, 0)
    rowblk = lambda a: pl.BlockSpec((tm, a.shape[1]), lambda i: (i, 0))
    return pl.pallas_call(
        functools.partial(_out_ln_kernel, n_in=n_in),
        grid=(t // tm,),
        in_specs=[rowblk(o) for o in os_] + [pl.BlockSpec(w.shape, const) for w in ws]
        + [rowblk(h), pl.BlockSpec((1, d), const), pl.BlockSpec((1, d), const)],
        out_specs=[pl.BlockSpec((tm, d), lambda i: (i, 0))] * 2,
        out_shape=[jax.ShapeDtypeStruct((t, d), F32), jax.ShapeDtypeStruct((t, d), BF16)],
        compiler_params=_params(("parallel",)),
        name="out_proj_ln",
    )(*os_, *ws, h, g, b)


def _router_kernel(h_ref, wh_ref, wl_ref, bias_ref, idx_ref, gate_ref, rank_ref, cnt_ref, cnt_scr):
    i = pl.program_id(0)
    tm = h_ref.shape[0]

    @pl.when(i == 0)
    def _():
        cnt_scr[...] = jnp.zeros(cnt_scr.shape, F32)

    h = h_ref[...]
    hh = h.astype(BF16)
    hl = (h - hh.astype(F32)).astype(BF16)
    logits = _dot(hh, wh_ref[...]) + (_dot(hh, wl_ref[...]) + _dot(hl, wh_ref[...]))
    scores = jax.nn.sigmoid(logits)
    sel = scores + bias_ref[...]
    e_iota = lax.broadcasted_iota(I32, (tm, N_EXPERTS), 1).astype(F32)
    out_iota = lax.broadcasted_iota(I32, (tm, LANES), 1)
    onehots, gates = [], []
    idx_out = jnp.zeros((tm, LANES), F32)
    for k in range(TOP_K):
        m = jnp.max(sel, axis=1, keepdims=True)
        pick = jnp.min(jnp.where(sel == m, e_iota, float(N_EXPERTS)), axis=1, keepdims=True)
        oh = e_iota == pick
        onehots.append(oh)
        gates.append(jnp.sum(jnp.where(oh, scores, 0.0), axis=1, keepdims=True))
        idx_out = jnp.where(out_iota == k, pick, idx_out)
        sel = jnp.where(oh, -jnp.inf, sel)
    maskf = onehots[0].astype(F32)
    for oh in onehots[1:]:
        maskf = maskf + oh.astype(F32)
    tri = (lax.broadcasted_iota(I32, (tm, tm), 1) < lax.broadcasted_iota(I32, (tm, tm), 0)).astype(BF16)
    ranks = cnt_scr[...] + _dot(tri, maskf.astype(BF16))
    gsum = gates[0]
    for gk in gates[1:]:
        gsum = gsum + gk
    gate_out = jnp.zeros((tm, LANES), F32)
    rank_out = jnp.zeros((tm, LANES), F32)
    for k in range(TOP_K):
        gate_out = jnp.where(out_iota == k, gates[k] / gsum * ROUTED_SCALE, gate_out)
        rk = jnp.sum(jnp.where(onehots[k], ranks, 0.0), axis=1, keepdims=True)
        rank_out = jnp.where(out_iota == k, rk, rank_out)
    idx_ref[...] = idx_out.astype(I32)
    gate_ref[...] = gate_out
    rank_ref[...] = rank_out.astype(I32)
    total = cnt_scr[...] + jnp.sum(maskf, axis=0, keepdims=True)
    cnt_scr[...] = total
    cnt_ref[...] = total


def _router(h, w_hi, w_lo, bias):
    t, d = h.shape
    tm = min(256, t)
    const = lambda i: (0, 0)
    out = lambda: pl.BlockSpec((tm, LANES), lambda i: (i, 0))
    return pl.pallas_call(
        _router_kernel,
        grid=(t // tm,),
        in_specs=[
            pl.BlockSpec((tm, d), lambda i: (i, 0)),
            pl.BlockSpec((d, N_EXPERTS), const),
            pl.BlockSpec((d, N_EXPERTS), const),
            pl.BlockSpec((1, N_EXPERTS), const),
        ],
        out_specs=[out(), out(), out(), pl.BlockSpec((1, N_EXPERTS), const)],
        out_shape=[
            jax.ShapeDtypeStruct((t, LANES), I32),
            jax.ShapeDtypeStruct((t, LANES), F32),
            jax.ShapeDtypeStruct((t, LANES), I32),
            jax.ShapeDtypeStruct((1, N_EXPERTS), F32),
        ],
        scratch_shapes=[pltpu.VMEM((1, N_EXPERTS), F32)],
        compiler_params=_params(("arbitrary",)),
        name="router",
    )(h, w_hi, w_lo, bias)


def _dispatch_kernel(dest_ref, x_ref, xs_ref, sem):
    tm = x_ref.shape[0]

    def body(r, carry):
        for k in range(TOP_K):
            d = dest_ref[r * TOP_K + k]
            pltpu.make_async_copy(x_ref.at[pl.ds(r, 1)], xs_ref.at[pl.ds(d, 1)], sem).start()
        return carry

    lax.fori_loop(0, tm, body, 0)
    for k in range(TOP_K):
        pltpu.make_async_copy(x_ref, xs_ref.at[pl.ds(0, tm)], sem).wait()


def _dispatch(h, dest_flat):
    t, d = h.shape
    tm = min(256, t)
    return pl.pallas_call(
        _dispatch_kernel,
        grid=(t // tm,),
        in_specs=[
            pl.BlockSpec((tm * TOP_K,), lambda i: (i,), memory_space=pltpu.SMEM),
            pl.BlockSpec((tm, d), lambda i: (i, 0)),
        ],
        out_specs=pl.BlockSpec(memory_space=pl.ANY),
        out_shape=jax.ShapeDtypeStruct((t * TOP_K, d), h.dtype),
        scratch_shapes=[pltpu.SemaphoreType.DMA(())],
        compiler_params=_params(("arbitrary",)),
        name="moe_dispatch",
    )(dest_flat, h)


def _expert_kernel(tile_ref, exp_ref, lo_ref, hi_ref, first_ref, x_ref, wg_ref, wu_ref, wd_ref, y_ref):
    w = pl.program_id(0)
    tm = x_ref.shape[0]
    lo = lo_ref[w]
    hi = hi_ref[w]

    @pl.when(hi > lo)
    def _():
        x = x_ref[...].astype(BF16)
        gate = _dot(x, wg_ref[0])
        up = _dot(x, wu_ref[0])
        hmid = (gate * jax.nn.sigmoid(gate) * up).astype(BF16)
        y = _dot(hmid, wd_ref[0])
        rows = tile_ref[w] * tm + lax.broadcasted_iota(I32, (tm, 1), 0)
        mine = (rows >= lo) & (rows < hi)
        prev = jnp.where(first_ref[w] == 1, jnp.zeros_like(y), y_ref[...])
        y_ref[...] = jnp.where(mine, y, prev)


def _experts(xs, wg, wu, wd, tile_w, exp_w, lo_w, hi_w, first_w, tm):
    a, d = xs.shape
    nw = tile_w.shape[0]
    grid_spec = pltpu.PrefetchScalarGridSpec(
        num_scalar_prefetch=5,
        grid=(nw,),
        in_specs=[
            pl.BlockSpec((tm, d), lambda w, tl, ex, lo, hi, fi: (tl[w], 0)),
            pl.BlockSpec((1, d, D_EXPERT), lambda w, tl, ex, lo, hi, fi: (ex[w], 0, 0)),
            pl.BlockSpec((1, d, D_EXPERT), lambda w, tl, ex, lo, hi, fi: (ex[w], 0, 0)),
            pl.BlockSpec((1, D_EXPERT, d), lambda w, tl, ex, lo, hi, fi: (ex[w], 0, 0)),
        ],
        out_specs=pl.BlockSpec((tm, d), lambda w, tl, ex, lo, hi, fi: (tl[w], 0)),
    )
    return pl.pallas_call(
        _expert_kernel,
        grid_spec=grid_spec,
        out_shape=jax.ShapeDtypeStruct((a, d), F32),
        compiler_params=_params(("arbitrary",)),
        name="moe_experts",
    )(tile_w, exp_w, lo_w, hi_w, first_w, xs, wg, wu, wd)


def _combine_kernel(dest_ref, ys_ref, gate_ref, h_ref, hb_ref, sg_ref, su_ref, sd_ref, g_ref, b_ref,
                    hf_out, hb_out, buf, sem):
    tm = h_ref.shape[0]

    def body(r, carry):
        for k in range(TOP_K):
            d = dest_ref[r * TOP_K + k]
            pltpu.make_async_copy(ys_ref.at[pl.ds(d, 1)], buf.at[k, pl.ds(r, 1)], sem).start()
        return carry

    lax.fori_loop(0, tm, body, 0)
    x = hb_ref[...]
    gate = _dot(x, sg_ref[...])
    up = _dot(x, su_ref[...])
    acc = _dot((gate * jax.nn.sigmoid(gate) * up).astype(BF16), sd_ref[...])
    acc = acc + DEEPNORM_ALPHA * h_ref[...]
    for k in range(TOP_K):
        pltpu.make_async_copy(ys_ref.at[pl.ds(0, tm)], buf.at[k], sem).wait()
    gates = gate_ref[...]
    for k in range(TOP_K):
        acc = acc + gates[:, k:k + 1] * buf[k]
    out = _layer_norm(acc, g_ref[...], b_ref[...])
    hf_out[...] = out
    hb_out[...] = out.astype(BF16)


def _combine(dest_flat, ys, gate, h, hb, sg, su, sd, g, b):
    t, d = h.shape
    tm = min(256, t)
    const = lambda i: (0, 0)
    row = lambda width: pl.BlockSpec((tm, width), lambda i: (i, 0))
    return pl.pallas_call(
        _combine_kernel,
        grid=(t // tm,),
        in_specs=[
            pl.BlockSpec((tm * TOP_K,), lambda i: (i,), memory_space=pltpu.SMEM),
            pl.BlockSpec(memory_space=pl.ANY),
            row(LANES), row(d), row(d),
            pl.BlockSpec(sg.shape, const), pl.BlockSpec(su.shape, const), pl.BlockSpec(sd.shape, const),
            pl.BlockSpec((1, d), const), pl.BlockSpec((1, d), const),
        ],
        out_specs=[row(d), row(d)],
        out_shape=[jax.ShapeDtypeStruct((t, d), F32), jax.ShapeDtypeStruct((t, d), BF16)],
        scratch_shapes=[pltpu.VMEM((TOP_K, tm, d), F32), pltpu.SemaphoreType.DMA(())],
        compiler_params=_params(("arbitrary",)),
        name="moe_combine",
    )(dest_flat, ys, gate, h, hb, sg, su, sd, g, b)


EXPERT_TILE = 256


def _group_metadata(counts, idx, rank, n_assign):
    tm = min(EXPERT_TILE, n_assign)
    nt = n_assign // tm
    nw = nt + N_EXPERTS - 1
    sizes = counts.reshape(N_EXPERTS).astype(I32)
    ends = jnp.cumsum(sizes)
    starts = ends - sizes
    dest = (starts[idx] + rank).reshape(-1).astype(I32)
    first_tile = starts // tm
    n_e = jnp.where(sizes > 0, (ends - 1) // tm - first_tile + 1, 0)
    cum = jnp.cumsum(n_e)
    off = cum - n_e
    total = cum[-1]
    w = jnp.arange(nw, dtype=I32)
    valid = w < total
    e_w = jnp.minimum(jnp.searchsorted(cum, w, side="right"), N_EXPERTS - 1).astype(I32)
    e_w = jnp.where(valid, e_w, e_w[jnp.maximum(total - 1, 0)])
    tile_w = jnp.where(valid, first_tile[e_w] + (w - off[e_w]), nt - 1).astype(I32)
    lo_w = jnp.where(valid, jnp.maximum(starts[e_w], tile_w * tm), 0).astype(I32)
    hi_w = jnp.where(valid, jnp.minimum(ends[e_w], (tile_w + 1) * tm), 0).astype(I32)
    first_w = jnp.concatenate([jnp.ones((1,), I32), (tile_w[1:] != tile_w[:-1]).astype(I32)])
    return dest, tile_w, e_w, lo_w, hi_w, first_w, tm


def _moe(h, hb, r_hi, r_lo, r_bias, wg, wu, wd, sg, su, sd, g, b):
    t = h.shape[0]
    idx, gate, rank, counts = _router(h, r_hi, r_lo, r_bias)
    dest, tile_w, e_w, lo_w, hi_w, first_w, tm = _group_metadata(
        counts, idx[:, :TOP_K], rank[:, :TOP_K], t * TOP_K)
    xs = _dispatch(h, dest)
    ys = _experts(xs, wg, wu, wd, tile_w, e_w, lo_w, hi_w, first_w, tm)
    return _combine(dest, ys, gate, h, hb, sg, su, sd, g, b)


def _even_weights(w_in, w_uq, w_ukv, w_out):
    o1 = MLA_Q_RANK
    o2 = o1 + MLA_KV_RANK
    o3 = o2 + MLA_ROPE
    o4 = o3 + 4 * DIFF_HEADS * DIFF_QK
    c_q, c_kv, k_r, dq, dv = w_in[:, :o1], w_in[:, o1:o2], w_in[:, o2:o3], w_in[:, o3:o4], w_in[:, o4:]
    dq = dq.reshape(D_MODEL, DIFF_HEADS, 4, DIFF_QK).transpose(0, 2, 1, 3).reshape(D_MODEL, -1)
    pad = jnp.zeros((D_MODEL, 512 - MLA_KV_RANK - MLA_ROPE), w_in.dtype)
    w_proj = jnp.concatenate([dq, dv, c_q, c_kv, k_r, pad], axis=1).astype(BF16)
    uq = w_uq.reshape(MLA_Q_RANK, MLA_HEADS, MLA_NOPE + MLA_ROPE)
    wqn = uq[:, :, :MLA_NOPE].reshape(MLA_Q_RANK, -1).astype(BF16)
    wqr = uq[:, :, MLA_NOPE:].reshape(MLA_Q_RANK, -1).astype(BF16)
    ukv = w_ukv.reshape(MLA_KV_RANK, MLA_HEADS, MLA_NOPE + MLA_V)
    wkn = ukv[:, :, :MLA_NOPE].reshape(MLA_KV_RANK, -1).astype(BF16)
    wv = ukv[:, :, MLA_NOPE:].reshape(MLA_KV_RANK, -1).astype(BF16)
    n_mla = MLA_HEADS * MLA_V
    return w_proj, wqn, wqr, wkn, wv, w_out[:n_mla].astype(BF16), w_out[n_mla:].astype(BF16)


def _odd_weights(w_in, w_out):
    nq = SWA_Q_HEADS * SWA_HD
    nkv = SWA_KV_HEADS * SWA_HD
    dup = lambda w: jnp.concatenate([w.reshape(D_MODEL, SWA_KV_HEADS, 1, SWA_HD)] * 2, axis=2).reshape(D_MODEL, -1)
    w_proj = jnp.concatenate([w_in[:, :nq], dup(w_in[:, nq:nq + nkv]), dup(w_in[:, nq + nkv:])], axis=1)
    return w_proj.astype(BF16), w_out.astype(BF16)


def kernel(x, positions, ab_w_in, mla_q_norm, mla_w_uq, mla_kv_norm, mla_w_ukv, diff_lambda_q1, diff_lambda_k1, diff_lambda_q2, diff_lambda_k2, diff_sub_norm, ab_w_out, swa_w_in, swa_sink, swa_w_out, mix_ln_g, mix_ln_b, ffn_ln_g, ffn_ln_b, router_w, router_bias, exp_w_gate, exp_w_up, exp_w_down, shared_w_gate, shared_w_up, shared_w_down):
    b, s, d = x.shape
    t = b * s
    tab = _rope_tables(positions)
    h = x.reshape(t, d)
    hb = h.astype(BF16)
    row = lambda v: v.reshape(1, -1)
    for layer in range(DEPTH):
        i = layer // 2
        if layer % 2 == 0:
            w_proj, wqn, wqr, wkn, wv, wo_mla, wo_diff = _even_weights(
                ab_w_in[i], mla_w_uq[i], mla_w_ukv[i], ab_w_out[i])
            proj = _matmul(hb, w_proj, "even_in_proj")
            q, k, v = _mla_prep(proj, tab, row(mla_q_norm[i]), row(mla_kv_norm[i]), wqn, wqr, wkn, wv,
                                b, s, col_cq=6, col_ckv=7)
            o_mla = _mla_attn(q, k, v).reshape(t, -1)
            dqk = _diff_prep(proj, tab, b, s)
            lam_p = jnp.stack([diff_lambda_q1[i], diff_lambda_k1[i], diff_lambda_q2[i], diff_lambda_k2[i]])
            lam_init = 0.8 - 0.6 * math.exp(-0.3 * layer)
            o_diff = _diff_attn(dqk, proj, 2048 // DIFF_V, lam_p, row(diff_sub_norm[i]), lam_init,
                                b, s).reshape(t, -1)
            h, hb = _out_ln([o_mla, o_diff], [wo_mla, wo_diff], h, row(mix_ln_g[layer]), row(mix_ln_b[layer]))
        else:
            w_proj, wo = _odd_weights(swa_w_in[i], swa_w_out[i])
            proj = _matmul(hb, w_proj, "odd_in_proj")
            nq = SWA_Q_HEADS * SWA_HD
            n_rot = nq + SWA_KV_HEADS * LANES
            qk = _swa_prep(proj, tab, n_rot)
            o = _swa_attn(qk, proj, swa_sink[i], b, s, col_k=nq // LANES, col_v=n_rot // LANES)
            h, hb = _out_ln([o], [wo], h, row(mix_ln_g[layer]), row(mix_ln_b[layer]))
        rw = router_w[layer]
        r_hi = rw.astype(BF16)
        r_lo = (rw - r_hi.astype(F32)).astype(BF16)
        h, hb = _moe(h, hb, r_hi, r_lo, row(router_bias[layer]),
                     exp_w_gate[layer].astype(BF16), exp_w_up[layer].astype(BF16), exp_w_down[layer].astype(BF16),
                     shared_w_gate[layer].astype(BF16), shared_w_up[layer].astype(BF16),
                     shared_w_down[layer].astype(BF16), row(ffn_ln_g[layer]), row(ffn_ln_b[layer]))
    return h.reshape(b, s, d)
```

```python
import functools
import math

import numpy as np
import jax
import jax.numpy as jnp
from jax import lax
from jax.experimental import pallas as pl
from jax.experimental.pallas import tpu as pltpu

F32 = jnp.float32
BF16 = jnp.bfloat16
I32 = jnp.int32

D_MODEL = 2048
DEPTH = 4
ROPE_THETA = 500000.0
LN_EPS = 1e-5
RMS_EPS = 1e-6
MLA_HEADS = 8
MLA_NOPE = 128
MLA_ROPE = 64
MLA_V = 128
MLA_Q_RANK = 512
MLA_KV_RANK = 256
DIFF_HEADS = 8
DIFF_QK = 64
DIFF_V = 128
DIFF_ROT = DIFF_QK // 4
SWA_Q_HEADS = 32
SWA_KV_HEADS = 4
SWA_GROUP = SWA_Q_HEADS // SWA_KV_HEADS
SWA_HD = 64
WINDOW = 128
N_EXPERTS = 64
TOP_K = 8
D_EXPERT = 256
ROUTED_SCALE = 2.5
DEEPNORM_ALPHA = (2 * DEPTH) ** 0.25
LOG2E = math.log2(math.e)

LANES = 128
MLA_QK_PAD = 256
VMEM_LIMIT = 48 << 20
ATTN_TQ = 512
ATTN_TK = 1024


def _params(sem):
    return pltpu.CompilerParams(dimension_semantics=sem, vmem_limit_bytes=VMEM_LIMIT)


def _dot(a, b):
    return jnp.dot(a, b, preferred_element_type=F32)


def _dot_nt(a, b):
    return lax.dot_general(a, b, (((1,), (1,)), ((), ())), preferred_element_type=F32)


def _tables_kernel(pos_ref, c_ref, out_ref):
    pos = pos_ref[...]
    a64 = pos * c_ref[0:1, :]
    a16 = pos * c_ref[3:4, :]
    s64 = jnp.sin(a64)
    s16 = jnp.sin(a16)
    out_ref[0] = jnp.cos(a64)
    out_ref[1] = s64 * c_ref[1:2, :]
    out_ref[2] = s64 * c_ref[2:3, :]
    out_ref[3] = jnp.cos(a16)
    out_ref[4] = s16 * c_ref[4:5, :]
    out_ref[5] = s16 * c_ref[5:6, :]


def _rope_consts():
    j = np.arange(LANES) % 64
    inv64 = ROPE_THETA ** (-jnp.arange(0, MLA_ROPE, 2, dtype=F32) / MLA_ROPE)
    inv16 = ROPE_THETA ** (-jnp.arange(0, DIFF_ROT, 2, dtype=F32) / DIFF_ROT)
    f64 = inv64[j % 32]
    f16 = jnp.where(j < 16, inv16[j % 8], 0.0)
    rows = [
        f64,
        jnp.asarray(np.where(j < 32, -1.0, 0.0), F32),
        jnp.asarray(np.where(j >= 32, 1.0, 0.0), F32),
        f16,
        jnp.asarray(np.where(j < 8, -1.0, 0.0), F32),
        jnp.asarray(np.where((j >= 8) & (j < 16), 1.0, 0.0), F32),
        jnp.zeros((LANES,), F32),
        jnp.zeros((LANES,), F32),
    ]
    return jnp.stack(rows).astype(F32)


def _rope_tables(positions):
    t = positions.size
    pos = jnp.broadcast_to(positions.reshape(t, 1).astype(F32), (t, LANES))
    tm = min(512, t)
    return pl.pallas_call(
        _tables_kernel,
        grid=(t // tm,),
        in_specs=[pl.BlockSpec((tm, LANES), lambda i: (i, 0)), pl.BlockSpec((8, LANES), lambda i: (0, 0))],
        out_specs=pl.BlockSpec((6, tm, LANES), lambda i: (0, i, 0)),
        out_shape=jax.ShapeDtypeStruct((6, t, LANES), F32),
        compiler_params=_params(("parallel",)),
        name="rope_tables",
    )(pos, _rope_consts())


def _rope_tile(x, c, sa, sb, half):
    return x * c + pltpu.roll(x, LANES - half, 1) * sa + pltpu.roll(x, half, 1) * sb


def _mm_kernel(x_ref, w_ref, o_ref):
    o_ref[...] = _dot(x_ref[...], w_ref[...]).astype(o_ref.dtype)


def _matmul(x, w, name, tm=512, tn=1024):
    m, k = x.shape
    n = w.shape[1]
    tm = min(tm, m)
    tn = min(tn, n)
    return pl.pallas_call(
        _mm_kernel,
        grid=(n // tn, m // tm),
        in_specs=[pl.BlockSpec((tm, k), lambda j, i: (i, 0)), pl.BlockSpec((k, tn), lambda j, i: (0, j))],
        out_specs=pl.BlockSpec((tm, tn), lambda j, i: (i, j)),
        out_shape=jax.ShapeDtypeStruct((m, n), BF16),
        compiler_params=_params(("parallel", "parallel")),
        name=name,
    )(x, w)


def _rms(x, g):
    return x * lax.rsqrt(jnp.mean(x * x, -1, keepdims=True) + RMS_EPS) * g


def _mla_prep_kernel(cq_ref, ckv_ref, tab_ref, qg_ref, kvg_ref, wqn_ref, wqr_ref, wkn_ref, wv_ref,
                     q_ref, k_ref, v_ref, *, scale):
    tm = cq_ref.shape[0]
    cq = cq_ref[...].astype(F32)
    ckv_all = ckv_ref[...].astype(F32)
    ckv = ckv_all[:, :MLA_KV_RANK]
    kr = ckv_all[:, MLA_KV_RANK:MLA_KV_RANK + LANES]
    cqn = _rms(cq, qg_ref[...]).astype(BF16)
    ckvn = _rms(ckv, kvg_ref[...]).astype(BF16)
    qn = _dot(cqn, wqn_ref[...]) * scale
    qr = _dot(cqn, wqr_ref[...]) * scale
    kn = _dot(ckvn, wkn_ref[...])
    vv = _dot(ckvn, wv_ref[...])
    c, sa, sb = tab_ref[0], tab_ref[1], tab_ref[2]
    lo = lax.broadcasted_iota(I32, (tm, LANES), 1) < MLA_ROPE
    half = MLA_ROPE // 2
    krr = jnp.where(lo, _rope_tile(kr, c, sa, sb, half), 0.0).astype(BF16)
    for j in range(MLA_HEADS // 2):
        r = _rope_tile(qr[:, LANES * j:LANES * (j + 1)], c, sa, sb, half)
        q_ref[0, 2 * j, :, LANES:] = jnp.where(lo, r, 0.0).astype(BF16)
        q_ref[0, 2 * j + 1, :, LANES:] = jnp.where(lo, pltpu.roll(r, MLA_ROPE, 1), 0.0).astype(BF16)
    for h in range(MLA_HEADS):
        sl = slice(LANES * h, LANES * (h + 1))
        q_ref[0, h, :, :LANES] = qn[:, sl].astype(BF16)
        k_ref[0, h, :, :LANES] = kn[:, sl].astype(BF16)
        k_ref[0, h, :, LANES:] = krr
        v_ref[0, h] = vv[:, sl].astype(BF16)


def _mla_prep(proj, tab, q_norm, kv_norm, wqn, wqr, wkn, wv, b, s, col_cq, col_ckv):
    tm = min(256, s)
    nt = s // tm
    h = MLA_HEADS
    row = lambda bi, i: bi * nt + i
    const = lambda bi, i: (0, 0)
    return pl.pallas_call(
        functools.partial(_mla_prep_kernel, scale=(MLA_NOPE + MLA_ROPE) ** -0.5 * LOG2E),
        grid=(b, nt),
        in_specs=[
            pl.BlockSpec((tm, MLA_Q_RANK), lambda bi, i: (row(bi, i), col_cq)),
            pl.BlockSpec((tm, 512), lambda bi, i: (row(bi, i), col_ckv)),
            pl.BlockSpec((3, tm, LANES), lambda bi, i: (0, row(bi, i), 0)),
            pl.BlockSpec((1, MLA_Q_RANK), const),
            pl.BlockSpec((1, MLA_KV_RANK), const),
            pl.BlockSpec(wqn.shape, const),
            pl.BlockSpec(wqr.shape, const),
            pl.BlockSpec(wkn.shape, const),
            pl.BlockSpec(wv.shape, const),
        ],
        out_specs=[
            pl.BlockSpec((1, h, tm, MLA_QK_PAD), lambda bi, i: (bi, 0, i, 0)),
            pl.BlockSpec((1, h, tm, MLA_QK_PAD), lambda bi, i: (bi, 0, i, 0)),
            pl.BlockSpec((1, h, tm, MLA_V), lambda bi, i: (bi, 0, i, 0)),
        ],
        out_shape=[
            jax.ShapeDtypeStruct((b, h, s, MLA_QK_PAD), BF16),
            jax.ShapeDtypeStruct((b, h, s, MLA_QK_PAD), BF16),
            jax.ShapeDtypeStruct((b, h, s, MLA_V), BF16),
        ],
        compiler_params=_params(("parallel", "parallel")),
        name="mla_prep",
    )(proj, proj, tab, q_norm, kv_norm, wqn, wqr, wkn, wv)


def _diff_prep_kernel(dq_ref, tab_ref, o_ref):
    c, sa, sb = tab_ref[0], tab_ref[1], tab_ref[2]
    n = DIFF_HEADS * DIFF_QK
    for i in range(4):
        for j in range(n // LANES):
            x = dq_ref[:, n * i + LANES * j:n * i + LANES * (j + 1)].astype(F32)
            r = _rope_tile(x, c, sa, sb, DIFF_ROT // 2)
            if i < 2:
                r = r * (DIFF_QK ** -0.5 * LOG2E)
            o_ref[i, 0, :, LANES * j:LANES * (j + 1)] = r.astype(BF16)


def _diff_prep(proj, tab, b, s):
    tm = min(256, s)
    nt = s // tm
    n = DIFF_HEADS * DIFF_QK
    return pl.pallas_call(
        _diff_prep_kernel,
        grid=(b, nt),
        in_specs=[
            pl.BlockSpec((tm, 4 * n), lambda bi, i: (bi * nt + i, 0)),
            pl.BlockSpec((3, tm, LANES), lambda bi, i: (1, bi * nt + i, 0)),
        ],
        out_specs=pl.BlockSpec((4, 1, tm, n), lambda bi, i: (0, bi, i, 0)),
        out_shape=jax.ShapeDtypeStruct((4, b, s, n), BF16),
        compiler_params=_params(("parallel", "parallel")),
        name="diff_prep",
    )(proj, tab)


def _online_softmax_loop(qs, k_ats, v_at, nk):
    tq = qs[0].shape[0]

    def body(j, carry):
        v = v_at(j)
        out = []
        for q, k_at, (m, l, acc) in zip(qs, k_ats, carry):
            s = _dot_nt(q, k_at(j))
            m_new = jnp.maximum(m, jnp.max(s, axis=1, keepdims=True))
            alpha = jnp.exp2(m - m_new)
            p = jnp.exp2(s - m_new)
            l = alpha * l + jnp.sum(p, axis=1, keepdims=True)
            acc = alpha * acc + _dot(p.astype(BF16), v)
            out.append((m_new, l, acc))
        return tuple(out)

    one = (jnp.full((tq, 1), -jnp.inf, F32), jnp.zeros((tq, 1), F32), jnp.zeros((tq, v_at(0).shape[1]), F32))
    final = lax.fori_loop(0, nk, body, tuple(one for _ in qs), unroll=True)
    return [(l, acc) for _, l, acc in final]


def _mla_attn_kernel(q_ref, k_ref, v_ref, o_ref, *, tk):
    nk = k_ref.shape[2] // tk
    chunk = lambda j: pl.ds(pl.multiple_of(j * tk, tk), tk)
    (l, acc), = _online_softmax_loop(
        [q_ref[0, 0]], [lambda j: k_ref[0, 0, chunk(j), :]], lambda j: v_ref[0, 0, chunk(j), :], nk)
    o_ref[0] = (acc / l).astype(o_ref.dtype)


def _mla_attn(q, k, v):
    b, h, s, _ = q.shape
    tq = min(ATTN_TQ, s)
    return pl.pallas_call(
        functools.partial(_mla_attn_kernel, tk=min(ATTN_TK, s)),
        grid=(b, h, s // tq),
        in_specs=[
            pl.BlockSpec((1, 1, tq, MLA_QK_PAD), lambda bi, hi, i: (bi, hi, i, 0)),
            pl.BlockSpec((1, 1, s, MLA_QK_PAD), lambda bi, hi, i: (bi, hi, 0, 0)),
            pl.BlockSpec((1, 1, s, MLA_V), lambda bi, hi, i: (bi, hi, 0, 0)),
        ],
        out_specs=pl.BlockSpec((1, tq, MLA_V), lambda bi, hi, i: (bi, i, hi)),
        out_shape=jax.ShapeDtypeStruct((b, s, h * MLA_V), BF16),
        compiler_params=_params(("parallel", "parallel", "arbitrary")),
        name="mla_attn",
    )(q, k, v)


def _diff_attn_kernel(q1_ref, q2_ref, k1_ref, k2_ref, v_ref, lam_ref, g_ref, o_ref, *, tk, lam_init):
    hi = pl.program_id(1)
    tq = q1_ref.shape[2]
    nk = v_ref.shape[0] // tk
    chunk = lambda j: pl.ds(pl.multiple_of(j * tk, tk), tk)
    mine = (lax.broadcasted_iota(I32, (tq, LANES), 1) // DIFF_QK) == (hi % 2)
    zero = jnp.zeros((), BF16)
    qs = [jnp.where(mine, q1_ref[0, 0], zero), jnp.where(mine, q2_ref[0, 0], zero)]
    k_ats = [lambda j: k1_ref[0, 0, chunk(j), :], lambda j: k2_ref[0, 0, chunk(j), :]]
    (l1, a1), (l2, a2) = _online_softmax_loop(qs, k_ats, lambda j: v_ref[chunk(j), :], nk)
    lp = lam_ref[...]
    lam = (jnp.exp(jnp.sum(lp[0:1] * lp[1:2], axis=1, keepdims=True))
           - jnp.exp(jnp.sum(lp[2:3] * lp[3:4], axis=1, keepdims=True)) + lam_init)
    o = a1 / l1 - lam * (a2 / l2)
    o_ref[0] = (_rms(o, g_ref[...]) * (1.0 - lam_init)).astype(o_ref.dtype)


def _diff_attn(dqk, proj, col_v, lam_p, sub_norm, lam_init, b, s):
    h = DIFF_HEADS
    tq = min(ATTN_TQ, s)
    qspec = lambda which: pl.BlockSpec((1, 1, tq, LANES), lambda bi, hi, i: (which, bi, i, hi // 2))
    kspec = lambda which: pl.BlockSpec((1, 1, s, LANES), lambda bi, hi, i: (which, bi, 0, hi // 2))
    const = lambda bi, hi, i: (0, 0)
    return pl.pallas_call(
        functools.partial(_diff_attn_kernel, tk=min(ATTN_TK, s), lam_init=lam_init),
        grid=(b, h, s // tq),
        in_specs=[
            qspec(0), qspec(1), kspec(2), kspec(3),
            pl.BlockSpec((s, DIFF_V), lambda bi, hi, i: (bi, col_v + hi)),
            pl.BlockSpec((4, DIFF_QK), const),
            pl.BlockSpec((1, DIFF_V), const),
        ],
        out_specs=pl.BlockSpec((1, tq, DIFF_V), lambda bi, hi, i: (bi, i, hi)),
        out_shape=jax.ShapeDtypeStruct((b, s, h * DIFF_V), BF16),
        compiler_params=_params(("parallel", "parallel", "arbitrary")),
        name="diff_attn",
    )(dqk, dqk, dqk, dqk, proj, lam_p, sub_norm)


def _swa_prep_kernel(x_ref, tab_ref, o_ref):
    c, sa, sb = tab_ref[0], tab_ref[1], tab_ref[2]
    nq = SWA_Q_HEADS * SWA_HD // LANES
    for j in range(x_ref.shape[1] // LANES):
        x = x_ref[:, LANES * j:LANES * (j + 1)].astype(F32)
        r = _rope_tile(x, c, sa, sb, DIFF_ROT // 2)
        if j < nq:
            r = r * (SWA_HD ** -0.5)
        o_ref[:, LANES * j:LANES * (j + 1)] = r.astype(BF16)


def _swa_prep(proj, tab, n_cols):
    t = proj.shape[0]
    tm = min(256, t)
    return pl.pallas_call(
        _swa_prep_kernel,
        grid=(t // tm,),
        in_specs=[
            pl.BlockSpec((tm, n_cols), lambda i: (i, 0)),
            pl.BlockSpec((3, tm, LANES), lambda i: (1, i, 0)),
        ],
        out_specs=pl.BlockSpec((tm, n_cols), lambda i: (i, 0)),
        out_shape=jax.ShapeDtypeStruct((t, n_cols), BF16),
        compiler_params=_params(("parallel",)),
        name="swa_prep",
    )(proj, tab)


def _swa_attn_kernel(sink_ref, q_ref, kp_ref, km_ref, kn_ref, vp_ref, vm_ref, vn_ref, o_ref, *, seq):
    g = pl.program_id(1)
    i = pl.program_id(2)
    tq = q_ref.shape[0]
    span = tq + 2 * WINDOW
    k = jnp.concatenate([kp_ref[...], km_ref[...], kn_ref[...]], axis=0)
    v = jnp.concatenate([vp_ref[...], vm_ref[...], vn_ref[...]], axis=0)
    qpos = i * tq + lax.broadcasted_iota(I32, (tq, span), 0)
    kpos = i * tq - WINDOW + lax.broadcasted_iota(I32, (tq, span), 1)
    valid = (jnp.abs(kpos - qpos) <= WINDOW) & (kpos >= 0) & (kpos < seq)
    lane = lax.broadcasted_iota(I32, (tq, LANES), 1)
    zero = jnp.zeros((), BF16)
    for pair in range(SWA_GROUP // 2):
        qp = q_ref[:, LANES * pair:LANES * (pair + 1)]
        outs = []
        for half in range(2):
            sink = sink_ref[g * SWA_GROUP + 2 * pair + half]
            qh = jnp.where((lane // SWA_HD) == half, qp, zero)
            sc = jnp.where(valid, _dot_nt(qh, k), -jnp.inf)
            m = jnp.maximum(jnp.max(sc, axis=1, keepdims=True), sink)
            p = jnp.exp(sc - m)
            denom = jnp.sum(p, axis=1, keepdims=True) + jnp.exp(sink - m)
            outs.append(_dot(p.astype(BF16), v) / denom)
        o_ref[:, LANES * pair:LANES * (pair + 1)] = jnp.where(lane < SWA_HD, outs[0], outs[1]).astype(o_ref.dtype)


def _swa_attn(qk, proj, sink, b, s, col_k, col_v, tq=256):
    tq = min(tq, s)
    nt = s // tq
    r = tq // WINDOW
    nwb = s // WINDOW
    gw = SWA_GROUP * SWA_HD
    row = lambda bi, i: bi * nt + i
    prev = lambda bi, i: bi * nwb + jnp.maximum(i * r - 1, 0)
    nxt = lambda bi, i: bi * nwb + jnp.minimum((i + 1) * r, nwb - 1)
    grid_spec = pltpu.PrefetchScalarGridSpec(
        num_scalar_prefetch=1,
        grid=(b, SWA_KV_HEADS, nt),
        in_specs=[
            pl.BlockSpec((tq, gw), lambda bi, g, i, sk: (row(bi, i), g)),
            pl.BlockSpec((WINDOW, LANES), lambda bi, g, i, sk: (prev(bi, i), col_k + g)),
            pl.BlockSpec((tq, LANES), lambda bi, g, i, sk: (row(bi, i), col_k + g)),
            pl.BlockSpec((WINDOW, LANES), lambda bi, g, i, sk: (nxt(bi, i), col_k + g)),
            pl.BlockSpec((WINDOW, LANES), lambda bi, g, i, sk: (prev(bi, i), col_v + g)),
            pl.BlockSpec((tq, LANES), lambda bi, g, i, sk: (row(bi, i), col_v + g)),
            pl.BlockSpec((WINDOW, LANES), lambda bi, g, i, sk: (nxt(bi, i), col_v + g)),
        ],
        out_specs=pl.BlockSpec((tq, gw), lambda bi, g, i, sk: (row(bi, i), g)),
    )
    return pl.pallas_call(
        functools.partial(_swa_attn_kernel, seq=s),
        grid_spec=grid_spec,
        out_shape=jax.ShapeDtypeStruct((b * s, SWA_Q_HEADS * SWA_HD), BF16),
        compiler_params=_params(("parallel", "parallel", "parallel")),
        name="swa_attn",
    )(sink, qk, qk, qk, qk, proj, proj, proj)


def _layer_norm(y, g, b):
    mu = jnp.mean(y, -1, keepdims=True)
    d = y - mu
    var = jnp.mean(d * d, -1, keepdims=True)
    return d * lax.rsqrt(var + LN_EPS) * g + b


def _out_ln_kernel(*refs, n_in):
    o_refs = refs[:n_in]
    w_refs = refs[n_in:2 * n_in]
    h_ref, g_ref, b_ref, hf_ref, hb_ref = refs[2 * n_in:]
    mix = _dot(o_refs[0][...], w_refs[0][...])
    for o_ref, w_ref in zip(o_refs[1:], w_refs[1:]):
        mix = mix + _dot(o_ref[...], w_ref[...])
    out = _layer_norm(DEEPNORM_ALPHA * h_ref[...] + mix, g_ref[...], b_ref[...])
    hf_ref[...] = out
    hb_ref[...] = out.astype(BF16)


def _out_ln(os_, ws, h, g, b):
    t, d = h.shape
    tm = min(256, t)
    n_in = len(os_)
    const = lambda i: (0, 0)
    rowblk = lambda a: pl.BlockSpec((tm, a.shape[1]), lambda i: (i, 0))
    return pl.pallas_call(
        functools.partial(_out_ln_kernel, n_in=n_in),
        grid=(t // tm,),
        in_specs=[rowblk(o) for o in os_] + [pl.BlockSpec(w.shape, const) for w in ws]
        + [rowblk(h), pl.BlockSpec((1, d), const), pl.BlockSpec((1, d), const)],
        out_specs=[pl.BlockSpec((tm, d), lambda i: (i, 0))] * 2,
        out_shape=[jax.ShapeDtypeStruct((t, d), F32), jax.ShapeDtypeStruct((t, d), BF16)],
        compiler_params=_params(("parallel",)),
        name="out_proj_ln",
    )(*os_, *ws, h, g, b)


def _router_kernel(h_ref, wh_ref, wl_ref, bias_ref, idx_ref, gate_ref, rank_ref, cnt_ref, cnt_scr):
    i = pl.program_id(0)
    tm = h_ref.shape[0]

    @pl.when(i == 0)
    def _():
        cnt_scr[...] = jnp.zeros(cnt_scr.shape, F32)

    h = h_ref[...]
    hh = h.astype(BF16)
    hl = (h - hh.astype(F32)).astype(BF16)
    logits = _dot(hh, wh_ref[...]) + (_dot(hh, wl_ref[...]) + _dot(hl, wh_ref[...]))
    scores = jax.nn.sigmoid(logits)
    sel = scores + bias_ref[...]
    e_iota = lax.broadcasted_iota(I32, (tm, N_EXPERTS), 1).astype(F32)
    out_iota = lax.broadcasted_iota(I32, (tm, LANES), 1)
    onehots, gates = [], []
    idx_out = jnp.zeros((tm, LANES), F32)
    for k in range(TOP_K):
        m = jnp.max(sel, axis=1, keepdims=True)
        pick = jnp.min(jnp.where(sel == m, e_iota, float(N_EXPERTS)), axis=1, keepdims=True)
        oh = e_iota == pick
        onehots.append(oh)
        gates.append(jnp.sum(jnp.where(oh, scores, 0.0), axis=1, keepdims=True))
        idx_out = jnp.where(out_iota == k, pick, idx_out)
        sel = jnp.where(oh, -jnp.inf, sel)
    maskf = onehots[0].astype(F32)
    for oh in onehots[1:]:
        maskf = maskf + oh.astype(F32)
    tri = (lax.broadcasted_iota(I32, (tm, tm), 1) < lax.broadcasted_iota(I32, (tm, tm), 0)).astype(BF16)
    ranks = cnt_scr[...] + _dot(tri, maskf.astype(BF16))
    gsum = gates[0]
    for gk in gates[1:]:
        gsum = gsum + gk
    gate_out = jnp.zeros((tm, LANES), F32)
    rank_out = jnp.zeros((tm, LANES), F32)
    for k in range(TOP_K):
        gate_out = jnp.where(out_iota == k, gates[k] / gsum * ROUTED_SCALE, gate_out)
        rk = jnp.sum(jnp.where(onehots[k], ranks, 0.0), axis=1, keepdims=True)
        rank_out = jnp.where(out_iota == k, rk, rank_out)
    idx_ref[...] = idx_out.astype(I32)
    gate_ref[...] = gate_out
    rank_ref[...] = rank_out.astype(I32)
    total = cnt_scr[...] + jnp.sum(maskf, axis=0, keepdims=True)
    cnt_scr[...] = total
    cnt_ref[...] = total


def _router(h, w_hi, w_lo, bias):
    t, d = h.shape
    tm = min(256, t)
    const = lambda i: (0, 0)
    out = lambda: pl.BlockSpec((tm, LANES), lambda i: (i, 0))
    return pl.pallas_call(
        _router_kernel,
        grid=(t // tm,),
        in_specs=[
            pl.BlockSpec((tm, d), lambda i: (i, 0)),
            pl.BlockSpec((d, N_EXPERTS), const),
            pl.BlockSpec((d, N_EXPERTS), const),
            pl.BlockSpec((1, N_EXPERTS), const),
        ],
        out_specs=[out(), out(), out(), pl.BlockSpec((1, N_EXPERTS), const)],
        out_shape=[
            jax.ShapeDtypeStruct((t, LANES), I32),
            jax.ShapeDtypeStruct((t, LANES), F32),
            jax.ShapeDtypeStruct((t, LANES), I32),
            jax.ShapeDtypeStruct((1, N_EXPERTS), F32),
        ],
        scratch_shapes=[pltpu.VMEM((1, N_EXPERTS), F32)],
        compiler_params=_params(("arbitrary",)),
        name="router",
    )(h, w_hi, w_lo, bias)


def _dispatch_kernel(dest_ref, x_ref, xs_ref, sem):
    tm = x_ref.shape[0]

    def body(r, carry):
        for k in range(TOP_K):
            d = dest_ref[r * TOP_K + k]
            pltpu.make_async_copy(x_ref.at[pl.ds(r, 1)], xs_ref.at[pl.ds(d, 1)], sem).start(priority=k % 2)
        return carry

    lax.fori_loop(0, tm, body, 0)
    for k in range(TOP_K):
        pltpu.make_async_copy(x_ref, xs_ref.at[pl.ds(0, tm)], sem).wait()


def _dispatch(h, dest_flat):
    t, d = h.shape
    tm = min(256, t)
    return pl.pallas_call(
        _dispatch_kernel,
        grid=(t // tm,),
        in_specs=[
            pl.BlockSpec((tm * TOP_K,), lambda i: (i,), memory_space=pltpu.SMEM),
            pl.BlockSpec((tm, d), lambda i: (i, 0)),
        ],
        out_specs=pl.BlockSpec(memory_space=pl.ANY),
        out_shape=jax.ShapeDtypeStruct((t * TOP_K, d), h.dtype),
        scratch_shapes=[pltpu.SemaphoreType.DMA(())],
        compiler_params=_params(("arbitrary",)),
        name="moe_dispatch",
    )(dest_flat, h)


def _expert_kernel(tile_ref, exp_ref, lo_ref, hi_ref, first_ref, x_ref, wg_ref, wu_ref, wd_ref, y_ref,
                   wg_bf, wu_bf, wd_bf):
    w = pl.program_id(0)
    tm = x_ref.shape[0]
    lo = lo_ref[w]
    hi = hi_ref[w]

    @pl.when((w == 0) | (exp_ref[w] != exp_ref[jnp.maximum(w - 1, 0)]))
    def _():
        wg_bf[...] = wg_ref[0, 0].astype(BF16)
        wu_bf[...] = wu_ref[0, 0].astype(BF16)
        wd_bf[...] = wd_ref[0, 0].astype(BF16)

    @pl.when(hi > lo)
    def _():
        x = x_ref[...].astype(BF16)
        gate = _dot(x, wg_bf[...])
        up = _dot(x, wu_bf[...])
        hmid = (gate * jax.nn.sigmoid(gate) * up).astype(BF16)
        y = _dot(hmid, wd_bf[...])
        rows = tile_ref[w] * tm + lax.broadcasted_iota(I32, (tm, 1), 0)
        mine = (rows >= lo) & (rows < hi)
        prev = jnp.where(first_ref[w] == 1, jnp.zeros_like(y), y_ref[...])
        y_ref[...] = jnp.where(mine, y, prev)


def _experts(xs, wg, wu, wd, layer, tile_w, exp_w, lo_w, hi_w, first_w, tm):
    a, d = xs.shape
    nw = tile_w.shape[0]
    grid_spec = pltpu.PrefetchScalarGridSpec(
        num_scalar_prefetch=5,
        grid=(nw,),
        in_specs=[
            pl.BlockSpec((tm, d), lambda w, tl, ex, lo, hi, fi: (tl[w], 0)),
            pl.BlockSpec((1, 1, d, D_EXPERT), lambda w, tl, ex, lo, hi, fi: (layer, ex[w], 0, 0)),
            pl.BlockSpec((1, 1, d, D_EXPERT), lambda w, tl, ex, lo, hi, fi: (layer, ex[w], 0, 0)),
            pl.BlockSpec((1, 1, D_EXPERT, d), lambda w, tl, ex, lo, hi, fi: (layer, ex[w], 0, 0)),
        ],
        out_specs=pl.BlockSpec((tm, d), lambda w, tl, ex, lo, hi, fi: (tl[w], 0)),
        scratch_shapes=[pltpu.VMEM((d, D_EXPERT), BF16), pltpu.VMEM((d, D_EXPERT), BF16),
                        pltpu.VMEM((D_EXPERT, d), BF16)],
    )
    return pl.pallas_call(
        _expert_kernel,
        grid_spec=grid_spec,
        out_shape=jax.ShapeDtypeStruct((a, d), F32),
        compiler_params=_params(("arbitrary",)),
        name="moe_experts",
    )(tile_w, exp_w, lo_w, hi_w, first_w, xs, wg, wu, wd)


def _combine_kernel(dest_ref, ys_ref, gate_ref, h_ref, hb_ref, sg_ref, su_ref, sd_ref, g_ref, b_ref,
                    hf_out, hb_out, buf, sem):
    tm = h_ref.shape[0]

    def body(r, carry):
        for k in range(TOP_K):
            d = dest_ref[r * TOP_K + k]
            pltpu.make_async_copy(ys_ref.at[pl.ds(d, 1)], buf.at[k, pl.ds(r, 1)], sem).start(priority=k % 2)
        return carry

    lax.fori_loop(0, tm, body, 0)
    x = hb_ref[...]
    gate = _dot(x, sg_ref[...])
    up = _dot(x, su_ref[...])
    acc = _dot((gate * jax.nn.sigmoid(gate) * up).astype(BF16), sd_ref[...])
    acc = acc + DEEPNORM_ALPHA * h_ref[...]
    for k in range(TOP_K):
        pltpu.make_async_copy(ys_ref.at[pl.ds(0, tm)], buf.at[k], sem).wait()
    gates = gate_ref[...]
    for k in range(TOP_K):
        acc = acc + gates[:, k:k + 1] * buf[k]
    out = _layer_norm(acc, g_ref[...], b_ref[...])
    hf_out[...] = out
    hb_out[...] = out.astype(BF16)


def _combine(dest_flat, ys, gate, h, hb, sg, su, sd, g, b):
    t, d = h.shape
    tm = min(256, t)
    const = lambda i: (0, 0)
    row = lambda width: pl.BlockSpec((tm, width), lambda i: (i, 0))
    return pl.pallas_call(
        _combine_kernel,
        grid=(t // tm,),
        in_specs=[
            pl.BlockSpec((tm * TOP_K,), lambda i: (i,), memory_space=pltpu.SMEM),
            pl.BlockSpec(memory_space=pl.ANY),
            row(LANES), row(d), row(d),
            pl.BlockSpec(sg.shape, const), pl.BlockSpec(su.shape, const), pl.BlockSpec(sd.shape, const),
            pl.BlockSpec((1, d), const), pl.BlockSpec((1, d), const),
        ],
        out_specs=[row(d), row(d)],
        out_shape=[jax.ShapeDtypeStruct((t, d), F32), jax.ShapeDtypeStruct((t, d), BF16)],
        scratch_shapes=[pltpu.VMEM((TOP_K, tm, d), F32), pltpu.SemaphoreType.DMA(())],
        compiler_params=_params(("arbitrary",)),
        name="moe_combine",
    )(dest_flat, ys, gate, h, hb, sg, su, sd, g, b)


EXPERT_TILE = 256


def _group_metadata(counts, idx, rank, n_assign):
    tm = min(EXPERT_TILE, n_assign)
    nt = n_assign // tm
    nw = nt + N_EXPERTS - 1
    sizes = counts.reshape(N_EXPERTS).astype(I32)
    ends = jnp.cumsum(sizes)
    starts = ends - sizes
    dest = (starts[idx] + rank).reshape(-1).astype(I32)
    first_tile = starts // tm
    n_e = jnp.where(sizes > 0, (ends - 1) // tm - first_tile + 1, 0)
    cum = jnp.cumsum(n_e)
    off = cum - n_e
    total = cum[-1]
    w = jnp.arange(nw, dtype=I32)
    valid = w < total
    e_w = jnp.minimum(jnp.sum(cum[None, :] <= w[:, None], axis=1), N_EXPERTS - 1).astype(I32)
    e_w = jnp.where(valid, e_w, e_w[jnp.maximum(total - 1, 0)])
    tile_w = jnp.where(valid, first_tile[e_w] + (w - off[e_w]), nt - 1).astype(I32)
    lo_w = jnp.where(valid, jnp.maximum(starts[e_w], tile_w * tm), 0).astype(I32)
    hi_w = jnp.where(valid, jnp.minimum(ends[e_w], (tile_w + 1) * tm), 0).astype(I32)
    first_w = jnp.concatenate([jnp.ones((1,), I32), (tile_w[1:] != tile_w[:-1]).astype(I32)])
    return dest, tile_w, e_w, lo_w, hi_w, first_w, tm


def _moe(h, hb, r_hi, r_lo, r_bias, wg, wu, wd, layer, sg, su, sd, g, b):
    t = h.shape[0]
    idx, gate, rank, counts = _router(h, r_hi, r_lo, r_bias)
    dest, tile_w, e_w, lo_w, hi_w, first_w, tm = _group_metadata(
        counts, idx[:, :TOP_K], rank[:, :TOP_K], t * TOP_K)
    xs = _dispatch(h, dest)
    ys = _experts(xs, wg, wu, wd, layer, tile_w, e_w, lo_w, hi_w, first_w, tm)
    return _combine(dest, ys, gate, h, hb, sg, su, sd, g, b)


def _even_weights(w_in, w_uq, w_ukv, w_out):
    o1 = MLA_Q_RANK
    o2 = o1 + MLA_KV_RANK
    o3 = o2 + MLA_ROPE
    o4 = o3 + 4 * DIFF_HEADS * DIFF_QK
    c_q, c_kv, k_r, dq, dv = w_in[:, :o1], w_in[:, o1:o2], w_in[:, o2:o3], w_in[:, o3:o4], w_in[:, o4:]
    dq = dq.reshape(D_MODEL, DIFF_HEADS, 4, DIFF_QK).transpose(0, 2, 1, 3).reshape(D_MODEL, -1)
    pad = jnp.zeros((D_MODEL, 512 - MLA_KV_RANK - MLA_ROPE), w_in.dtype)
    w_proj = jnp.concatenate([dq, dv, c_q, c_kv, k_r, pad], axis=1).astype(BF16)
    uq = w_uq.reshape(MLA_Q_RANK, MLA_HEADS, MLA_NOPE + MLA_ROPE)
    wqn = uq[:, :, :MLA_NOPE].reshape(MLA_Q_RANK, -1).astype(BF16)
    wqr = uq[:, :, MLA_NOPE:].reshape(MLA_Q_RANK, -1).astype(BF16)
    ukv = w_ukv.reshape(MLA_KV_RANK, MLA_HEADS, MLA_NOPE + MLA_V)
    wkn = ukv[:, :, :MLA_NOPE].reshape(MLA_KV_RANK, -1).astype(BF16)
    wv = ukv[:, :, MLA_NOPE:].reshape(MLA_KV_RANK, -1).astype(BF16)
    n_mla = MLA_HEADS * MLA_V
    return w_proj, wqn, wqr, wkn, wv, w_out[:n_mla].astype(BF16), w_out[n_mla:].astype(BF16)


def _odd_weights(w_in, w_out):
    nq = SWA_Q_HEADS * SWA_HD
    nkv = SWA_KV_HEADS * SWA_HD
    dup = lambda w: jnp.concatenate([w.reshape(D_MODEL, SWA_KV_HEADS, 1, SWA_HD)] * 2, axis=2).reshape(D_MODEL, -1)
    w_proj = jnp.concatenate([w_in[:, :nq], dup(w_in[:, nq:nq + nkv]), dup(w_in[:, nq + nkv:])], axis=1)
    return w_proj.astype(BF16), w_out.astype(BF16)


def kernel(x, positions, ab_w_in, mla_q_norm, mla_w_uq, mla_kv_norm, mla_w_ukv, diff_lambda_q1, diff_lambda_k1, diff_lambda_q2, diff_lambda_k2, diff_sub_norm, ab_w_out, swa_w_in, swa_sink, swa_w_out, mix_ln_g, mix_ln_b, ffn_ln_g, ffn_ln_b, router_w, router_bias, exp_w_gate, exp_w_up, exp_w_down, shared_w_gate, shared_w_up, shared_w_down):
    b, s, d = x.shape
    t = b * s
    tab = _rope_tables(positions)
    h = x.reshape(t, d)
    hb = h.astype(BF16)
    row = lambda v: v.reshape(1, -1)
    for layer in range(DEPTH):
        i = layer // 2
        if layer % 2 == 0:
            w_proj, wqn, wqr, wkn, wv, wo_mla, wo_diff = _even_weights(
                ab_w_in[i], mla_w_uq[i], mla_w_ukv[i], ab_w_out[i])
            proj = _matmul(hb, w_proj, "even_in_proj")
            q, k, v = _mla_prep(proj, tab, row(mla_q_norm[i]), row(mla_kv_norm[i]), wqn, wqr, wkn, wv,
                                b, s, col_cq=6, col_ckv=7)
            o_mla = _mla_attn(q, k, v).reshape(t, -1)
            dqk = _diff_prep(proj, tab, b, s)
            lam_p = jnp.stack([diff_lambda_q1[i], diff_lambda_k1[i], diff_lambda_q2[i], diff_lambda_k2[i]])
            lam_init = 0.8 - 0.6 * math.exp(-0.3 * layer)
            o_diff = _diff_attn(dqk, proj, 2048 // DIFF_V, lam_p, row(diff_sub_norm[i]), lam_init,
                                b, s).reshape(t, -1)
            h, hb = _out_ln([o_mla, o_diff], [wo_mla, wo_diff], h, row(mix_ln_g[layer]), row(mix_ln_b[layer]))
        else:
            w_proj, wo = _odd_weights(swa_w_in[i], swa_w_out[i])
            proj = _matmul(hb, w_proj, "odd_in_proj")
            nq = SWA_Q_HEADS * SWA_HD
            n_rot = nq + SWA_KV_HEADS * LANES
            qk = _swa_prep(proj, tab, n_rot)
            o = _swa_attn(qk, proj, swa_sink[i], b, s, col_k=nq // LANES, col_v=n_rot // LANES)
            h, hb = _out_ln([o], [wo], h, row(mix_ln_g[layer]), row(mix_ln_b[layer]))
        rw = router_w[layer]
        r_hi = rw.astype(BF16)
        r_lo = (rw - r_hi.astype(F32)).astype(BF16)
        h, hb = _moe(h, hb, r_hi, r_lo, row(router_bias[layer]),
                     exp_w_gate, exp_w_up, exp_w_down, layer,
                     shared_w_gate[layer].astype(BF16), shared_w_up[layer].astype(BF16),
                     shared_w_down[layer].astype(BF16), row(ffn_ln_g[layer]), row(ffn_ln_b[layer]))
    return h.reshape(b, s, d)
```

```python
import functools
import math

import numpy as np
import jax
import jax.numpy as jnp
from jax import lax
from jax.experimental import pallas as pl
from jax.experimental.pallas import tpu as pltpu

F32 = jnp.float32
BF16 = jnp.bfloat16
I32 = jnp.int32

D_MODEL = 2048
DEPTH = 4
ROPE_THETA = 500000.0
LN_EPS = 1e-5
RMS_EPS = 1e-6
MLA_HEADS = 8
MLA_NOPE = 128
MLA_ROPE = 64
MLA_V = 128
MLA_Q_RANK = 512
MLA_KV_RANK = 256
DIFF_HEADS = 8
DIFF_QK = 64
DIFF_V = 128
DIFF_ROT = DIFF_QK // 4
SWA_Q_HEADS = 32
SWA_KV_HEADS = 4
SWA_GROUP = SWA_Q_HEADS // SWA_KV_HEADS
SWA_HD = 64
WINDOW = 128
N_EXPERTS = 64
TOP_K = 8
D_EXPERT = 256
ROUTED_SCALE = 2.5
DEEPNORM_ALPHA = (2 * DEPTH) ** 0.25
LOG2E = math.log2(math.e)

LANES = 128
MLA_QK_PAD = 256
VMEM_LIMIT = 48 << 20
ATTN_TQ = 512
ATTN_TK = 2048


def _params(sem):
    return pltpu.CompilerParams(dimension_semantics=sem, vmem_limit_bytes=VMEM_LIMIT)


def _dot(a, b):
    return jnp.dot(a, b, preferred_element_type=F32)


def _dot_nt(a, b):
    return lax.dot_general(a, b, (((1,), (1,)), ((), ())), preferred_element_type=F32)


def _tables_kernel(pos_ref, c_ref, out_ref):
    pos = pos_ref[...]
    a64 = pos * c_ref[0:1, :]
    a16 = pos * c_ref[3:4, :]
    s64 = jnp.sin(a64)
    s16 = jnp.sin(a16)
    out_ref[0] = jnp.cos(a64)
    out_ref[1] = s64 * c_ref[1:2, :]
    out_ref[2] = s64 * c_ref[2:3, :]
    out_ref[3] = jnp.cos(a16)
    out_ref[4] = s16 * c_ref[4:5, :]
    out_ref[5] = s16 * c_ref[5:6, :]


def _rope_consts():
    j = np.arange(LANES) % 64
    inv64 = ROPE_THETA ** (-jnp.arange(0, MLA_ROPE, 2, dtype=F32) / MLA_ROPE)
    inv16 = ROPE_THETA ** (-jnp.arange(0, DIFF_ROT, 2, dtype=F32) / DIFF_ROT)
    f64 = inv64[j % 32]
    f16 = jnp.where(j < 16, inv16[j % 8], 0.0)
    rows = [
        f64,
        jnp.asarray(np.where(j < 32, -1.0, 0.0), F32),
        jnp.asarray(np.where(j >= 32, 1.0, 0.0), F32),
        f16,
        jnp.asarray(np.where(j < 8, -1.0, 0.0), F32),
        jnp.asarray(np.where((j >= 8) & (j < 16), 1.0, 0.0), F32),
        jnp.zeros((LANES,), F32),
        jnp.zeros((LANES,), F32),
    ]
    return jnp.stack(rows).astype(F32)


def _rope_tables(positions):
    t = positions.size
    pos = jnp.broadcast_to(positions.reshape(t, 1).astype(F32), (t, LANES))
    tm = min(512, t)
    return pl.pallas_call(
        _tables_kernel,
        grid=(t // tm,),
        in_specs=[pl.BlockSpec((tm, LANES), lambda i: (i, 0)), pl.BlockSpec((8, LANES), lambda i: (0, 0))],
        out_specs=pl.BlockSpec((6, tm, LANES), lambda i: (0, i, 0)),
        out_shape=jax.ShapeDtypeStruct((6, t, LANES), F32),
        compiler_params=_params(("parallel",)),
        name="rope_tables",
    )(pos, _rope_consts())


def _rope_tile(x, c, sa, sb, half):
    return x * c + pltpu.roll(x, LANES - half, 1) * sa + pltpu.roll(x, half, 1) * sb


def _mm_kernel(x_ref, w_ref, o_ref):
    o_ref[...] = _dot(x_ref[...], w_ref[...]).astype(o_ref.dtype)


def _matmul(x, w, name, tm=512, tn=1024):
    m, k = x.shape
    n = w.shape[1]
    tm = min(tm, m)
    tn = min(tn, n)
    return pl.pallas_call(
        _mm_kernel,
        grid=(n // tn, m // tm),
        in_specs=[pl.BlockSpec((tm, k), lambda j, i: (i, 0)), pl.BlockSpec((k, tn), lambda j, i: (0, j))],
        out_specs=pl.BlockSpec((tm, tn), lambda j, i: (i, j)),
        out_shape=jax.ShapeDtypeStruct((m, n), BF16),
        compiler_params=_params(("parallel", "parallel")),
        name=name,
    )(x, w)


def _rms(x, g):
    return x * lax.rsqrt(jnp.mean(x * x, -1, keepdims=True) + RMS_EPS) * g


def _mla_prep_kernel(cq_ref, ckv_ref, tab_ref, qg_ref, kvg_ref, wqn_ref, wqr_ref, wkn_ref, wv_ref,
                     q_ref, k_ref, v_ref, *, scale):
    tm = cq_ref.shape[0]
    cq = cq_ref[...].astype(F32)
    ckv_all = ckv_ref[...].astype(F32)
    ckv = ckv_all[:, :MLA_KV_RANK]
    kr = ckv_all[:, MLA_KV_RANK:MLA_KV_RANK + LANES]
    cqn = _rms(cq, qg_ref[...]).astype(BF16)
    ckvn = _rms(ckv, kvg_ref[...]).astype(BF16)
    qn = _dot(cqn, wqn_ref[...]) * scale
    qr = _dot(cqn, wqr_ref[...]) * scale
    kn = _dot(ckvn, wkn_ref[...])
    vv = _dot(ckvn, wv_ref[...])
    c, sa, sb = tab_ref[0], tab_ref[1], tab_ref[2]
    lo = lax.broadcasted_iota(I32, (tm, LANES), 1) < MLA_ROPE
    half = MLA_ROPE // 2
    krr = jnp.where(lo, _rope_tile(kr, c, sa, sb, half), 0.0).astype(BF16)
    for j in range(MLA_HEADS // 2):
        r = _rope_tile(qr[:, LANES * j:LANES * (j + 1)], c, sa, sb, half)
        q_ref[0, 2 * j, :, LANES:] = jnp.where(lo, r, 0.0).astype(BF16)
        q_ref[0, 2 * j + 1, :, LANES:] = jnp.where(lo, pltpu.roll(r, MLA_ROPE, 1), 0.0).astype(BF16)
    for h in range(MLA_HEADS):
        sl = slice(LANES * h, LANES * (h + 1))
        q_ref[0, h, :, :LANES] = qn[:, sl].astype(BF16)
        k_ref[0, h, :, :LANES] = kn[:, sl].astype(BF16)
        k_ref[0, h, :, LANES:] = krr
        v_ref[0, h] = vv[:, sl].astype(BF16)


def _mla_prep(proj, tab, q_norm, kv_norm, wqn, wqr, wkn, wv, b, s, col_cq, col_ckv):
    tm = min(256, s)
    nt = s // tm
    h = MLA_HEADS
    row = lambda bi, i: bi * nt + i
    const = lambda bi, i: (0, 0)
    return pl.pallas_call(
        functools.partial(_mla_prep_kernel, scale=(MLA_NOPE + MLA_ROPE) ** -0.5 * LOG2E),
        grid=(b, nt),
        in_specs=[
            pl.BlockSpec((tm, MLA_Q_RANK), lambda bi, i: (row(bi, i), col_cq)),
            pl.BlockSpec((tm, 512), lambda bi, i: (row(bi, i), col_ckv)),
            pl.BlockSpec((3, tm, LANES), lambda bi, i: (0, row(bi, i), 0)),
            pl.BlockSpec((1, MLA_Q_RANK), const),
            pl.BlockSpec((1, MLA_KV_RANK), const),
            pl.BlockSpec(wqn.shape, const),
            pl.BlockSpec(wqr.shape, const),
            pl.BlockSpec(wkn.shape, const),
            pl.BlockSpec(wv.shape, const),
        ],
        out_specs=[
            pl.BlockSpec((1, h, tm, MLA_QK_PAD), lambda bi, i: (bi, 0, i, 0)),
            pl.BlockSpec((1, h, tm, MLA_QK_PAD), lambda bi, i: (bi, 0, i, 0)),
            pl.BlockSpec((1, h, tm, MLA_V), lambda bi, i: (bi, 0, i, 0)),
        ],
        out_shape=[
            jax.ShapeDtypeStruct((b, h, s, MLA_QK_PAD), BF16),
            jax.ShapeDtypeStruct((b, h, s, MLA_QK_PAD), BF16),
            jax.ShapeDtypeStruct((b, h, s, MLA_V), BF16),
        ],
        compiler_params=_params(("parallel", "parallel")),
        name="mla_prep",
    )(proj, proj, tab, q_norm, kv_norm, wqn, wqr, wkn, wv)


def _diff_prep_kernel(dq_ref, tab_ref, o_ref):
    c, sa, sb = tab_ref[0], tab_ref[1], tab_ref[2]
    n = DIFF_HEADS * DIFF_QK
    for i in range(4):
        for j in range(n // LANES):
            x = dq_ref[:, n * i + LANES * j:n * i + LANES * (j + 1)].astype(F32)
            r = _rope_tile(x, c, sa, sb, DIFF_ROT // 2)
            if i < 2:
                r = r * (DIFF_QK ** -0.5 * LOG2E)
            o_ref[i, 0, :, LANES * j:LANES * (j + 1)] = r.astype(BF16)


def _diff_prep(proj, tab, b, s):
    tm = min(256, s)
    nt = s // tm
    n = DIFF_HEADS * DIFF_QK
    return pl.pallas_call(
        _diff_prep_kernel,
        grid=(b, nt),
        in_specs=[
            pl.BlockSpec((tm, 4 * n), lambda bi, i: (bi * nt + i, 0)),
            pl.BlockSpec((3, tm, LANES), lambda bi, i: (1, bi * nt + i, 0)),
        ],
        out_specs=pl.BlockSpec((4, 1, tm, n), lambda bi, i: (0, bi, i, 0)),
        out_shape=jax.ShapeDtypeStruct((4, b, s, n), BF16),
        compiler_params=_params(("parallel", "parallel")),
        name="diff_prep",
    )(proj, tab)


def _online_softmax_loop(qs, k_ats, v_at, nk):
    tq = qs[0].shape[0]

    def body(j, carry):
        v = v_at(j)
        out = []
        for q, k_at, (m, l, acc) in zip(qs, k_ats, carry):
            s = _dot_nt(q, k_at(j))
            m_new = jnp.maximum(m, jnp.max(s, axis=1, keepdims=True))
            alpha = jnp.exp2(m - m_new)
            p = jnp.exp2(s - m_new)
            l = alpha * l + jnp.sum(p, axis=1, keepdims=True)
            acc = alpha * acc + _dot(p.astype(BF16), v)
            out.append((m_new, l, acc))
        return tuple(out)

    one = (jnp.full((tq, 1), -jnp.inf, F32), jnp.zeros((tq, 1), F32), jnp.zeros((tq, v_at(0).shape[1]), F32))
    final = lax.fori_loop(0, nk, body, tuple(one for _ in qs), unroll=True)
    return [(l, acc) for _, l, acc in final]


def _mla_attn_kernel(q_ref, k_ref, v_ref, o_ref, *, tk):
    nk = k_ref.shape[2] // tk
    chunk = lambda j: pl.ds(pl.multiple_of(j * tk, tk), tk)
    (l, acc), = _online_softmax_loop(
        [q_ref[0, 0]], [lambda j: k_ref[0, 0, chunk(j), :]], lambda j: v_ref[0, 0, chunk(j), :], nk)
    o_ref[0] = (acc / l).astype(o_ref.dtype)


def _mla_attn(q, k, v):
    b, h, s, _ = q.shape
    tq = min(ATTN_TQ, s)
    return pl.pallas_call(
        functools.partial(_mla_attn_kernel, tk=min(ATTN_TK, s)),
        grid=(b, h, s // tq),
        in_specs=[
            pl.BlockSpec((1, 1, tq, MLA_QK_PAD), lambda bi, hi, i: (bi, hi, i, 0)),
            pl.BlockSpec((1, 1, s, MLA_QK_PAD), lambda bi, hi, i: (bi, hi, 0, 0)),
            pl.BlockSpec((1, 1, s, MLA_V), lambda bi, hi, i: (bi, hi, 0, 0)),
        ],
        out_specs=pl.BlockSpec((1, tq, MLA_V), lambda bi, hi, i: (bi, i, hi)),
        out_shape=jax.ShapeDtypeStruct((b, s, h * MLA_V), BF16),
        compiler_params=_params(("parallel", "parallel", "arbitrary")),
        name="mla_attn",
    )(q, k, v)


def _diff_attn_kernel(q1_ref, q2_ref, k1_ref, k2_ref, v_ref, lam_ref, g_ref, o_ref, *, tk, lam_init):
    hi = pl.program_id(1)
    tq = q1_ref.shape[2]
    nk = v_ref.shape[0] // tk
    chunk = lambda j: pl.ds(pl.multiple_of(j * tk, tk), tk)
    mine = (lax.broadcasted_iota(I32, (tq, LANES), 1) // DIFF_QK) == (hi % 2)
    zero = jnp.zeros((), BF16)
    qs = [jnp.where(mine, q1_ref[0, 0], zero), jnp.where(mine, q2_ref[0, 0], zero)]
    k_ats = [lambda j: k1_ref[0, 0, chunk(j), :], lambda j: k2_ref[0, 0, chunk(j), :]]
    (l1, a1), (l2, a2) = _online_softmax_loop(qs, k_ats, lambda j: v_ref[chunk(j), :], nk)
    lp = lam_ref[...]
    lam = (jnp.exp(jnp.sum(lp[0:1] * lp[1:2], axis=1, keepdims=True))
           - jnp.exp(jnp.sum(lp[2:3] * lp[3:4], axis=1, keepdims=True)) + lam_init)
    o = a1 / l1 - lam * (a2 / l2)
    o_ref[0] = (_rms(o, g_ref[...]) * (1.0 - lam_init)).astype(o_ref.dtype)


def _diff_attn(dqk, proj, col_v, lam_p, sub_norm, lam_init, b, s):
    h = DIFF_HEADS
    tq = min(ATTN_TQ, s)
    qspec = lambda which: pl.BlockSpec((1, 1, tq, LANES), lambda bi, hi, i: (which, bi, i, hi // 2))
    kspec = lambda which: pl.BlockSpec((1, 1, s, LANES), lambda bi, hi, i: (which, bi, 0, hi // 2))
    const = lambda bi, hi, i: (0, 0)
    return pl.pallas_call(
        functools.partial(_diff_attn_kernel, tk=min(ATTN_TK, s), lam_init=lam_init),
        grid=(b, h, s // tq),
        in_specs=[
            qspec(0), qspec(1), kspec(2), kspec(3),
            pl.BlockSpec((s, DIFF_V), lambda bi, hi, i: (bi, col_v + hi)),
            pl.BlockSpec((4, DIFF_QK), const),
            pl.BlockSpec((1, DIFF_V), const),
        ],
        out_specs=pl.BlockSpec((1, tq, DIFF_V), lambda bi, hi, i: (bi, i, hi)),
        out_shape=jax.ShapeDtypeStruct((b, s, h * DIFF_V), BF16),
        compiler_params=_params(("parallel", "parallel", "arbitrary")),
        name="diff_attn",
    )(dqk, dqk, dqk, dqk, proj, lam_p, sub_norm)


def _swa_prep_kernel(x_ref, tab_ref, o_ref):
    c, sa, sb = tab_ref[0], tab_ref[1], tab_ref[2]
    nq = SWA_Q_HEADS * SWA_HD // LANES
    for j in range(x_ref.shape[1] // LANES):
        x = x_ref[:, LANES * j:LANES * (j + 1)].astype(F32)
        r = _rope_tile(x, c, sa, sb, DIFF_ROT // 2)
        if j < nq:
            r = r * (SWA_HD ** -0.5)
        o_ref[:, LANES * j:LANES * (j + 1)] = r.astype(BF16)


def _swa_prep(proj, tab, n_cols):
    t = proj.shape[0]
    tm = min(256, t)
    return pl.pallas_call(
        _swa_prep_kernel,
        grid=(t // tm,),
        in_specs=[
            pl.BlockSpec((tm, n_cols), lambda i: (i, 0)),
            pl.BlockSpec((3, tm, LANES), lambda i: (1, i, 0)),
        ],
        out_specs=pl.BlockSpec((tm, n_cols), lambda i: (i, 0)),
        out_shape=jax.ShapeDtypeStruct((t, n_cols), BF16),
        compiler_params=_params(("parallel",)),
        name="swa_prep",
    )(proj, tab)


def _swa_attn_kernel(sink_ref, q_ref, kp_ref, km_ref, kn_ref, vp_ref, vm_ref, vn_ref, o_ref, *, seq):
    g = pl.program_id(1)
    i = pl.program_id(2)
    tq = q_ref.shape[0]
    span = tq + 2 * WINDOW
    k = jnp.concatenate([kp_ref[...], km_ref[...], kn_ref[...]], axis=0)
    v = jnp.concatenate([vp_ref[...], vm_ref[...], vn_ref[...]], axis=0)
    qpos = i * tq + lax.broadcasted_iota(I32, (tq, span), 0)
    kpos = i * tq - WINDOW + lax.broadcasted_iota(I32, (tq, span), 1)
    valid = (jnp.abs(kpos - qpos) <= WINDOW) & (kpos >= 0) & (kpos < seq)
    lane = lax.broadcasted_iota(I32, (tq, LANES), 1)
    zero = jnp.zeros((), BF16)
    for pair in range(SWA_GROUP // 2):
        qp = q_ref[:, LANES * pair:LANES * (pair + 1)]
        outs = []
        for half in range(2):
            sink = sink_ref[g * SWA_GROUP + 2 * pair + half]
            qh = jnp.where((lane // SWA_HD) == half, qp, zero)
            sc = jnp.where(valid, _dot_nt(qh, k), -jnp.inf)
            m = jnp.maximum(jnp.max(sc, axis=1, keepdims=True), sink)
            p = jnp.exp(sc - m)
            denom = jnp.sum(p, axis=1, keepdims=True) + jnp.exp(sink - m)
            outs.append(_dot(p.astype(BF16), v) / denom)
        o_ref[:, LANES * pair:LANES * (pair + 1)] = jnp.where(lane < SWA_HD, outs[0], outs[1]).astype(o_ref.dtype)


def _swa_attn(qk, proj, sink, b, s, col_k, col_v, tq=256):
    tq = min(tq, s)
    nt = s // tq
    r = tq // WINDOW
    nwb = s // WINDOW
    gw = SWA_GROUP * SWA_HD
    row = lambda bi, i: bi * nt + i
    prev = lambda bi, i: bi * nwb + jnp.maximum(i * r - 1, 0)
    nxt = lambda bi, i: bi * nwb + jnp.minimum((i + 1) * r, nwb - 1)
    grid_spec = pltpu.PrefetchScalarGridSpec(
        num_scalar_prefetch=1,
        grid=(b, SWA_KV_HEADS, nt),
        in_specs=[
            pl.BlockSpec((tq, gw), lambda bi, g, i, sk: (row(bi, i), g)),
            pl.BlockSpec((WINDOW, LANES), lambda bi, g, i, sk: (prev(bi, i), col_k + g)),
            pl.BlockSpec((tq, LANES), lambda bi, g, i, sk: (row(bi, i), col_k + g)),
            pl.BlockSpec((WINDOW, LANES), lambda bi, g, i, sk: (nxt(bi, i), col_k + g)),
            pl.BlockSpec((WINDOW, LANES), lambda bi, g, i, sk: (prev(bi, i), col_v + g)),
            pl.BlockSpec((tq, LANES), lambda bi, g, i, sk: (row(bi, i), col_v + g)),
            pl.BlockSpec((WINDOW, LANES), lambda bi, g, i, sk: (nxt(bi, i), col_v + g)),
        ],
        out_specs=pl.BlockSpec((tq, gw), lambda bi, g, i, sk: (row(bi, i), g)),
    )
    return pl.pallas_call(
        functools.partial(_swa_attn_kernel, seq=s),
        grid_spec=grid_spec,
        out_shape=jax.ShapeDtypeStruct((b * s, SWA_Q_HEADS * SWA_HD), BF16),
        compiler_params=_params(("parallel", "parallel", "parallel")),
        name="swa_attn",
    )(sink, qk, qk, qk, qk, proj, proj, proj)


def _layer_norm(y, g, b):
    mu = jnp.mean(y, -1, keepdims=True)
    d = y - mu
    var = jnp.mean(d * d, -1, keepdims=True)
    return d * lax.rsqrt(var + LN_EPS) * g + b


PACK_SUBLANES = D_MODEL // 2 // LANES
HI_MASK = -65536


def _pack_words(lo, hi):
    lb = lax.bitcast_convert_type(lo.astype(BF16).astype(F32), I32)
    hb = lax.bitcast_convert_type(hi.astype(BF16).astype(F32), I32)
    return (hb & HI_MASK) | lax.shift_right_logical(lb, 16)


def _unpack_words(w):
    return lax.bitcast_convert_type(w << 16, F32), lax.bitcast_convert_type(w & HI_MASK, F32)


def _word_rows(j, tm, lead=()):
    return lead + (pl.ds(j, tm, stride=PACK_SUBLANES), slice(None))


def _store_packed(ref, y, keep=None, clear=None):
    tm, d = y.shape
    for j in range(PACK_SUBLANES):
        w = _pack_words(y[:, LANES * j:LANES * (j + 1)], y[:, d // 2 + LANES * j:d // 2 + LANES * (j + 1)])
        if keep is not None:
            w = jnp.where(keep, w, jnp.where(clear, 0, ref[_word_rows(j, tm)]))
        ref[_word_rows(j, tm)] = w


def _load_packed_bf16(ref, tm, lead=()):
    los, his = [], []
    for j in range(PACK_SUBLANES):
        lo, hi = _unpack_words(ref[_word_rows(j, tm, lead)])
        los.append(lo.astype(BF16))
        his.append(hi.astype(BF16))
    return jnp.concatenate(los + his, axis=1)


def _out_ln_kernel(*refs, n_in):
    o_refs = refs[:n_in]
    w_refs = refs[n_in:2 * n_in]
    h_ref, g_ref, b_ref, hf_ref, hp_ref = refs[2 * n_in:]
    mix = _dot(o_refs[0][...], w_refs[0][...])
    for o_ref, w_ref in zip(o_refs[1:], w_refs[1:]):
        mix = mix + _dot(o_ref[...], w_ref[...])
    out = _layer_norm(DEEPNORM_ALPHA * h_ref[...] + mix, g_ref[...], b_ref[...])
    hf_ref[...] = out
    _store_packed(hp_ref, out)


def _out_ln(os_, ws, h, g, b):
    t, d = h.shape
    tm = min(256, t)
    n_in = len(os_)
    const = lambda i: (0, 0)
    rowblk = lambda a: pl.BlockSpec((tm, a.shape[1]), lambda i: (i, 0))
    return pl.pallas_call(
        functools.partial(_out_ln_kernel, n_in=n_in),
        grid=(t // tm,),
        in_specs=[rowblk(o) for o in os_] + [pl.BlockSpec(w.shape, const) for w in ws]
        + [rowblk(h), pl.BlockSpec((1, d), const), pl.BlockSpec((1, d), const)],
        out_specs=[pl.BlockSpec((tm, d), lambda i: (i, 0)),
                   pl.BlockSpec((tm * PACK_SUBLANES, LANES), lambda i: (i, 0))],
        out_shape=[jax.ShapeDtypeStruct((t, d), F32), jax.ShapeDtypeStruct((t * PACK_SUBLANES, LANES), I32)],
        compiler_params=_params(("parallel",)),
        name="out_proj_ln",
    )(*os_, *ws, h, g, b)


def _router_kernel(h_ref, wh_ref, wl_ref, bias_ref, idx_ref, gate_ref, rank_ref, cnt_ref, cnt_scr):
    i = pl.program_id(0)
    tm = h_ref.shape[0]

    @pl.when(i == 0)
    def _():
        cnt_scr[...] = jnp.zeros(cnt_scr.shape, F32)

    h = h_ref[...]
    hh = h.astype(BF16)
    hl = (h - hh.astype(F32)).astype(BF16)
    logits = _dot(hh, wh_ref[...]) + (_dot(hh, wl_ref[...]) + _dot(hl, wh_ref[...]))
    scores = jax.nn.sigmoid(logits)
    sel = scores + bias_ref[...]
    e_iota = lax.broadcasted_iota(I32, (tm, N_EXPERTS), 1).astype(F32)
    out_iota = lax.broadcasted_iota(I32, (tm, LANES), 1)
    onehots, gates = [], []
    idx_out = jnp.zeros((tm, LANES), F32)
    for k in range(TOP_K):
        m = jnp.max(sel, axis=1, keepdims=True)
        pick = jnp.min(jnp.where(sel == m, e_iota, float(N_EXPERTS)), axis=1, keepdims=True)
        oh = e_iota == pick
        onehots.append(oh)
        gates.append(jnp.sum(jnp.where(oh, scores, 0.0), axis=1, keepdims=True))
        idx_out = jnp.where(out_iota == k, pick, idx_out)
        sel = jnp.where(oh, -jnp.inf, sel)
    maskf = onehots[0].astype(F32)
    for oh in onehots[1:]:
        maskf = maskf + oh.astype(F32)
    tri = (lax.broadcasted_iota(I32, (tm, tm), 1) < lax.broadcasted_iota(I32, (tm, tm), 0)).astype(BF16)
    ranks = cnt_scr[...] + _dot(tri, maskf.astype(BF16))
    gsum = gates[0]
    for gk in gates[1:]:
        gsum = gsum + gk
    gate_out = jnp.zeros((tm, LANES), F32)
    rank_out = jnp.zeros((tm, LANES), F32)
    for k in range(TOP_K):
        gate_out = jnp.where(out_iota == k, gates[k] / gsum * ROUTED_SCALE, gate_out)
        rk = jnp.sum(jnp.where(onehots[k], ranks, 0.0), axis=1, keepdims=True)
        rank_out = jnp.where(out_iota == k, rk, rank_out)
    idx_ref[...] = idx_out.astype(I32)
    gate_ref[...] = gate_out
    rank_ref[...] = rank_out.astype(I32)
    total = cnt_scr[...] + jnp.sum(maskf, axis=0, keepdims=True)
    cnt_scr[...] = total
    cnt_ref[...] = total


def _router(h, w_hi, w_lo, bias):
    t, d = h.shape
    tm = min(256, t)
    const = lambda i: (0, 0)
    out = lambda: pl.BlockSpec((tm, LANES), lambda i: (i, 0))
    return pl.pallas_call(
        _router_kernel,
        grid=(t // tm,),
        in_specs=[
            pl.BlockSpec((tm, d), lambda i: (i, 0)),
            pl.BlockSpec((d, N_EXPERTS), const),
            pl.BlockSpec((d, N_EXPERTS), const),
            pl.BlockSpec((1, N_EXPERTS), const),
        ],
        out_specs=[out(), out(), out(), pl.BlockSpec((1, N_EXPERTS), const)],
        out_shape=[
            jax.ShapeDtypeStruct((t, LANES), I32),
            jax.ShapeDtypeStruct((t, LANES), F32),
            jax.ShapeDtypeStruct((t, LANES), I32),
            jax.ShapeDtypeStruct((1, N_EXPERTS), F32),
        ],
        scratch_shapes=[pltpu.VMEM((1, N_EXPERTS), F32)],
        compiler_params=_params(("arbitrary",)),
        name="router",
    )(h, w_hi, w_lo, bias)


def _token_tile(ref, r, lead=()):
    return ref.at[lead + (pl.ds(pl.multiple_of(r * PACK_SUBLANES, PACK_SUBLANES), PACK_SUBLANES),)]


def _dispatch_kernel(dest_ref, x_ref, xs_ref, sem):
    tm = x_ref.shape[0] // PACK_SUBLANES

    def body(r, carry):
        src = _token_tile(x_ref, r)
        for k in range(TOP_K):
            pltpu.make_async_copy(src, xs_ref.at[dest_ref[r * TOP_K + k]], sem).start(priority=k % 2)
        return carry

    lax.fori_loop(0, tm, body, 0)
    for k in range(TOP_K):
        pltpu.make_async_copy(xs_ref.at[pl.ds(0, tm)], xs_ref.at[pl.ds(0, tm)], sem).wait()


def _dispatch(hp, dest_flat):
    t = hp.shape[0] // PACK_SUBLANES
    tm = min(256, t)
    return pl.pallas_call(
        _dispatch_kernel,
        grid=(t // tm,),
        in_specs=[
            pl.BlockSpec((tm * TOP_K,), lambda i: (i,), memory_space=pltpu.SMEM),
            pl.BlockSpec((tm * PACK_SUBLANES, LANES), lambda i: (i, 0)),
        ],
        out_specs=pl.BlockSpec(memory_space=pl.ANY),
        out_shape=jax.ShapeDtypeStruct((t * TOP_K, PACK_SUBLANES, LANES), I32),
        scratch_shapes=[pltpu.SemaphoreType.DMA(())],
        compiler_params=_params(("arbitrary",)),
        name="moe_dispatch",
    )(dest_flat, hp)


def _expert_kernel(tile_ref, exp_ref, lo_ref, hi_ref, first_ref, x_ref, wg_ref, wu_ref, wd_ref, y_ref,
                   wg_bf, wu_bf, wd_bf):
    w = pl.program_id(0)
    tm = x_ref.shape[0] // PACK_SUBLANES
    lo = lo_ref[w]
    hi = hi_ref[w]

    @pl.when((w == 0) | (exp_ref[w] != exp_ref[jnp.maximum(w - 1, 0)]))
    def _():
        wg_bf[...] = wg_ref[0, 0].astype(BF16)
        wu_bf[...] = wu_ref[0, 0].astype(BF16)
        wd_bf[...] = wd_ref[0, 0].astype(BF16)

    @pl.when(hi > lo)
    def _():
        x = _load_packed_bf16(x_ref, tm)
        gate = _dot(x, wg_bf[...])
        up = _dot(x, wu_bf[...])
        hmid = (gate * jax.nn.sigmoid(gate) * up).astype(BF16)
        y = _dot(hmid, wd_bf[...])
        rows = tile_ref[w] * tm + lax.broadcasted_iota(I32, (tm, 1), 0)
        mine = (rows >= lo) & (rows < hi)
        _store_packed(y_ref, y, keep=mine, clear=first_ref[w] == 1)


def _experts(xs, wg, wu, wd, layer, tile_w, exp_w, lo_w, hi_w, first_w, tm):
    d = D_MODEL
    rows = pl.BlockSpec((tm * PACK_SUBLANES, LANES), lambda w, tl, ex, lo, hi, fi: (tl[w], 0))
    nw = tile_w.shape[0]
    grid_spec = pltpu.PrefetchScalarGridSpec(
        num_scalar_prefetch=5,
        grid=(nw,),
        in_specs=[
            rows,
            pl.BlockSpec((1, 1, d, D_EXPERT), lambda w, tl, ex, lo, hi, fi: (layer, ex[w], 0, 0)),
            pl.BlockSpec((1, 1, d, D_EXPERT), lambda w, tl, ex, lo, hi, fi: (layer, ex[w], 0, 0)),
            pl.BlockSpec((1, 1, D_EXPERT, d), lambda w, tl, ex, lo, hi, fi: (layer, ex[w], 0, 0)),
        ],
        out_specs=rows,
        scratch_shapes=[pltpu.VMEM((d, D_EXPERT), BF16), pltpu.VMEM((d, D_EXPERT), BF16),
                        pltpu.VMEM((D_EXPERT, d), BF16)],
    )
    return pl.pallas_call(
        _expert_kernel,
        grid_spec=grid_spec,
        out_shape=jax.ShapeDtypeStruct(xs.shape, I32),
        compiler_params=_params(("arbitrary",)),
        name="moe_experts",
    )(tile_w, exp_w, lo_w, hi_w, first_w, xs, wg, wu, wd)


def _combine_kernel(dest_ref, ys_ref, gate_ref, h_ref, hp_ref, sg_ref, su_ref, sd_ref, g_ref, b_ref,
                    hf_out, hb_out, buf, sem):
    tm, d = h_ref.shape

    def body(r, carry):
        for k in range(TOP_K):
            pltpu.make_async_copy(
                ys_ref.at[dest_ref[r * TOP_K + k]], _token_tile(buf, r, (k,)), sem).start(priority=k % 2)
        return carry

    lax.fori_loop(0, tm, body, 0)
    x = _load_packed_bf16(hp_ref, tm)
    gate = _dot(x, sg_ref[...])
    up = _dot(x, su_ref[...])
    acc = _dot((gate * jax.nn.sigmoid(gate) * up).astype(BF16), sd_ref[...])
    acc = acc + DEEPNORM_ALPHA * h_ref[...]
    for k in range(TOP_K):
        pltpu.make_async_copy(ys_ref.at[pl.ds(0, tm)], ys_ref.at[pl.ds(0, tm)], sem).wait()
    gates = gate_ref[...]
    los = [None] * PACK_SUBLANES
    his = [None] * PACK_SUBLANES
    for k in range(TOP_K):
        gk = gates[:, k:k + 1]
        for j in range(PACK_SUBLANES):
            lo, hi = _unpack_words(buf[_word_rows(j, tm, (k,))])
            los[j] = gk * lo if k == 0 else los[j] + gk * lo
            his[j] = gk * hi if k == 0 else his[j] + gk * hi
    out = _layer_norm(acc + jnp.concatenate(los + his, axis=1), g_ref[...], b_ref[...])
    hf_out[...] = out
    hb_out[...] = out.astype(BF16)


def _combine(dest_flat, ys, gate, h, hp, sg, su, sd, g, b):
    t, d = h.shape
    tm = min(256, t)
    const = lambda i: (0, 0)
    row = lambda width: pl.BlockSpec((tm, width), lambda i: (i, 0))
    return pl.pallas_call(
        _combine_kernel,
        grid=(t // tm,),
        in_specs=[
            pl.BlockSpec((tm * TOP_K,), lambda i: (i,), memory_space=pltpu.SMEM),
            pl.BlockSpec(memory_space=pl.ANY),
            row(LANES), row(d), pl.BlockSpec((tm * PACK_SUBLANES, LANES), lambda i: (i, 0)),
            pl.BlockSpec(sg.shape, const), pl.BlockSpec(su.shape, const), pl.BlockSpec(sd.shape, const),
            pl.BlockSpec((1, d), const), pl.BlockSpec((1, d), const),
        ],
        out_specs=[row(d), row(d)],
        out_shape=[jax.ShapeDtypeStruct((t, d), F32), jax.ShapeDtypeStruct((t, d), BF16)],
        scratch_shapes=[pltpu.VMEM((TOP_K, tm * PACK_SUBLANES, LANES), I32), pltpu.SemaphoreType.DMA(())],
        compiler_params=_params(("arbitrary",)),
        name="moe_combine",
    )(dest_flat, ys, gate, h, hp, sg, su, sd, g, b)


EXPERT_TILE = 256


def _group_metadata(counts, idx, rank, n_assign):
    tm = min(EXPERT_TILE, n_assign)
    nt = n_assign // tm
    nw = nt + N_EXPERTS - 1
    sizes = counts.reshape(N_EXPERTS).astype(I32)
    ends = jnp.cumsum(sizes)
    starts = ends - sizes
    my_start = jnp.sum(jnp.where(idx[..., None] == jnp.arange(N_EXPERTS, dtype=I32), starts, 0), axis=-1)
    dest = (my_start + rank).reshape(-1).astype(I32)
    first_tile = starts // tm
    n_e = jnp.where(sizes > 0, (ends - 1) // tm - first_tile + 1, 0)
    cum = jnp.cumsum(n_e)
    off = cum - n_e
    total = cum[-1]
    w = jnp.arange(nw, dtype=I32)
    valid = w < total
    e_w = jnp.minimum(jnp.sum(cum[None, :] <= w[:, None], axis=1), N_EXPERTS - 1).astype(I32)
    e_w = jnp.where(valid, e_w, e_w[jnp.maximum(total - 1, 0)])
    tile_w = jnp.where(valid, first_tile[e_w] + (w - off[e_w]), nt - 1).astype(I32)
    lo_w = jnp.where(valid, jnp.maximum(starts[e_w], tile_w * tm), 0).astype(I32)
    hi_w = jnp.where(valid, jnp.minimum(ends[e_w], (tile_w + 1) * tm), 0).astype(I32)
    first_w = jnp.concatenate([jnp.ones((1,), I32), (tile_w[1:] != tile_w[:-1]).astype(I32)])
    return dest, tile_w, e_w, lo_w, hi_w, first_w, tm


def _moe(h, hp, r_hi, r_lo, r_bias, wg, wu, wd, layer, sg, su, sd, g, b):
    t = h.shape[0]
    a = t * TOP_K
    idx, gate, rank, counts = _router(h, r_hi, r_lo, r_bias)
    dest, tile_w, e_w, lo_w, hi_w, first_w, tm = _group_metadata(counts, idx[:, :TOP_K], rank[:, :TOP_K], a)
    xs = _dispatch(hp, dest)
    ys = _experts(xs.reshape(a * PACK_SUBLANES, LANES), wg, wu, wd, layer, tile_w, e_w, lo_w, hi_w, first_w, tm)
    return _combine(dest, ys.reshape(a, PACK_SUBLANES, LANES), gate, h, hp, sg, su, sd, g, b)


def _even_weights(w_in, w_uq, w_ukv, w_out):
    o1 = MLA_Q_RANK
    o2 = o1 + MLA_KV_RANK
    o3 = o2 + MLA_ROPE
    o4 = o3 + 4 * DIFF_HEADS * DIFF_QK
    c_q, c_kv, k_r, dq, dv = w_in[:, :o1], w_in[:, o1:o2], w_in[:, o2:o3], w_in[:, o3:o4], w_in[:, o4:]
    dq = dq.reshape(D_MODEL, DIFF_HEADS, 4, DIFF_QK).transpose(0, 2, 1, 3).reshape(D_MODEL, -1)
    pad = jnp.zeros((D_MODEL, 512 - MLA_KV_RANK - MLA_ROPE), w_in.dtype)
    w_proj = jnp.concatenate([dq, dv, c_q, c_kv, k_r, pad], axis=1).astype(BF16)
    uq = w_uq.reshape(MLA_Q_RANK, MLA_HEADS, MLA_NOPE + MLA_ROPE)
    wqn = uq[:, :, :MLA_NOPE].reshape(MLA_Q_RANK, -1).astype(BF16)
    wqr = uq[:, :, MLA_NOPE:].reshape(MLA_Q_RANK, -1).astype(BF16)
    ukv = w_ukv.reshape(MLA_KV_RANK, MLA_HEADS, MLA_NOPE + MLA_V)
    wkn = ukv[:, :, :MLA_NOPE].reshape(MLA_KV_RANK, -1).astype(BF16)
    wv = ukv[:, :, MLA_NOPE:].reshape(MLA_KV_RANK, -1).astype(BF16)
    n_mla = MLA_HEADS * MLA_V
    return w_proj, wqn, wqr, wkn, wv, w_out[:n_mla].astype(BF16), w_out[n_mla:].astype(BF16)


def _odd_weights(w_in, w_out):
    nq = SWA_Q_HEADS * SWA_HD
    nkv = SWA_KV_HEADS * SWA_HD
    dup = lambda w: jnp.concatenate([w.reshape(D_MODEL, SWA_KV_HEADS, 1, SWA_HD)] * 2, axis=2).reshape(D_MODEL, -1)
    w_proj = jnp.concatenate([w_in[:, :nq], dup(w_in[:, nq:nq + nkv]), dup(w_in[:, nq + nkv:])], axis=1)
    return w_proj.astype(BF16), w_out.astype(BF16)


def kernel(x, positions, ab_w_in, mla_q_norm, mla_w_uq, mla_kv_norm, mla_w_ukv, diff_lambda_q1, diff_lambda_k1, diff_lambda_q2, diff_lambda_k2, diff_sub_norm, ab_w_out, swa_w_in, swa_sink, swa_w_out, mix_ln_g, mix_ln_b, ffn_ln_g, ffn_ln_b, router_w, router_bias, exp_w_gate, exp_w_up, exp_w_down, shared_w_gate, shared_w_up, shared_w_down):
    b, s, d = x.shape
    t = b * s
    tab = _rope_tables(positions)
    h = x.reshape(t, d)
    hb = h.astype(BF16)
    row = lambda v: v.reshape(1, -1)
    for layer in range(DEPTH):
        i = layer // 2
        if layer % 2 == 0:
            w_proj, wqn, wqr, wkn, wv, wo_mla, wo_diff = _even_weights(
                ab_w_in[i], mla_w_uq[i], mla_w_ukv[i], ab_w_out[i])
            proj = _matmul(hb, w_proj, "even_in_proj")
            q, k, v = _mla_prep(proj, tab, row(mla_q_norm[i]), row(mla_kv_norm[i]), wqn, wqr, wkn, wv,
                                b, s, col_cq=6, col_ckv=7)
            o_mla = _mla_attn(q, k, v).reshape(t, -1)
            dqk = _diff_prep(proj, tab, b, s)
            lam_p = jnp.stack([diff_lambda_q1[i], diff_lambda_k1[i], diff_lambda_q2[i], diff_lambda_k2[i]])
            lam_init = 0.8 - 0.6 * math.exp(-0.3 * layer)
            o_diff = _diff_attn(dqk, proj, 2048 // DIFF_V, lam_p, row(diff_sub_norm[i]), lam_init,
                                b, s).reshape(t, -1)
            h, hp = _out_ln([o_mla, o_diff], [wo_mla, wo_diff], h, row(mix_ln_g[layer]), row(mix_ln_b[layer]))
        else:
            w_proj, wo = _odd_weights(swa_w_in[i], swa_w_out[i])
            proj = _matmul(hb, w_proj, "odd_in_proj")
            nq = SWA_Q_HEADS * SWA_HD
            n_rot = nq + SWA_KV_HEADS * LANES
            qk = _swa_prep(proj, tab, n_rot)
            o = _swa_attn(qk, proj, swa_sink[i], b, s, col_k=nq // LANES, col_v=n_rot // LANES)
            h, hp = _out_ln([o], [wo], h, row(mix_ln_g[layer]), row(mix_ln_b[layer]))
        rw = router_w[layer]
        r_hi = rw.astype(BF16)
        r_lo = (rw - r_hi.astype(F32)).astype(BF16)
        h, hb = _moe(h, hp, r_hi, r_lo, row(router_bias[layer]),
                     exp_w_gate, exp_w_up, exp_w_down, layer,
                     shared_w_gate[layer].astype(BF16), shared_w_up[layer].astype(BF16),
                     shared_w_down[layer].astype(BF16), row(ffn_ln_g[layer]), row(ffn_ln_b[layer]))
    return h.reshape(b, s, d)
```

```python
import functools
import math

import numpy as np
import jax
import jax.numpy as jnp
from jax import lax
from jax.experimental import pallas as pl
from jax.experimental.pallas import tpu as pltpu

F32 = jnp.float32
BF16 = jnp.bfloat16
I32 = jnp.int32

D_MODEL = 2048
DEPTH = 4
ROPE_THETA = 500000.0
LN_EPS = 1e-5
RMS_EPS = 1e-6
MLA_HEADS = 8
MLA_NOPE = 128
MLA_ROPE = 64
MLA_V = 128
MLA_Q_RANK = 512
MLA_KV_RANK = 256
DIFF_HEADS = 8
DIFF_QK = 64
DIFF_V = 128
DIFF_ROT = DIFF_QK // 4
SWA_Q_HEADS = 32
SWA_KV_HEADS = 4
SWA_GROUP = SWA_Q_HEADS // SWA_KV_HEADS
SWA_HD = 64
WINDOW = 128
N_EXPERTS = 64
TOP_K = 8
D_EXPERT = 256
ROUTED_SCALE = 2.5
DEEPNORM_ALPHA = (2 * DEPTH) ** 0.25
LOG2E = math.log2(math.e)

LANES = 128
MLA_QK_PAD = 256
VMEM_LIMIT = 48 << 20
ATTN_TQ = 512
ATTN_TK = 2048


def _params(sem):
    return pltpu.CompilerParams(dimension_semantics=sem, vmem_limit_bytes=VMEM_LIMIT)


def _dot(a, b):
    return jnp.dot(a, b, preferred_element_type=F32)


def _dot_nt(a, b):
    return lax.dot_general(a, b, (((1,), (1,)), ((), ())), preferred_element_type=F32)


def _tables_kernel(pos_ref, c_ref, out_ref):
    pos = pos_ref[...]
    a64 = pos * c_ref[0:1, :]
    a16 = pos * c_ref[3:4, :]
    s64 = jnp.sin(a64)
    s16 = jnp.sin(a16)
    out_ref[0] = jnp.cos(a64)
    out_ref[1] = s64 * c_ref[1:2, :]
    out_ref[2] = s64 * c_ref[2:3, :]
    out_ref[3] = jnp.cos(a16)
    out_ref[4] = s16 * c_ref[4:5, :]
    out_ref[5] = s16 * c_ref[5:6, :]


def _rope_consts():
    j = np.arange(LANES) % 64
    inv64 = ROPE_THETA ** (-jnp.arange(0, MLA_ROPE, 2, dtype=F32) / MLA_ROPE)
    inv16 = ROPE_THETA ** (-jnp.arange(0, DIFF_ROT, 2, dtype=F32) / DIFF_ROT)
    f64 = inv64[j % 32]
    f16 = jnp.where(j < 16, inv16[j % 8], 0.0)
    rows = [
        f64,
        jnp.asarray(np.where(j < 32, -1.0, 0.0), F32),
        jnp.asarray(np.where(j >= 32, 1.0, 0.0), F32),
        f16,
        jnp.asarray(np.where(j < 8, -1.0, 0.0), F32),
        jnp.asarray(np.where((j >= 8) & (j < 16), 1.0, 0.0), F32),
        jnp.zeros((LANES,), F32),
        jnp.zeros((LANES,), F32),
    ]
    return jnp.stack(rows).astype(F32)


def _rope_tables(positions):
    t = positions.size
    pos = jnp.broadcast_to(positions.reshape(t, 1).astype(F32), (t, LANES))
    tm = min(512, t)
    return pl.pallas_call(
        _tables_kernel,
        grid=(t // tm,),
        in_specs=[pl.BlockSpec((tm, LANES), lambda i: (i, 0)), pl.BlockSpec((8, LANES), lambda i: (0, 0))],
        out_specs=pl.BlockSpec((6, tm, LANES), lambda i: (0, i, 0)),
        out_shape=jax.ShapeDtypeStruct((6, t, LANES), F32),
        compiler_params=_params(("parallel",)),
        name="rope_tables",
    )(pos, _rope_consts())


def _rope_tile(x, c, sa, sb, half):
    return x * c + pltpu.roll(x, LANES - half, 1) * sa + pltpu.roll(x, half, 1) * sb


def _mm_kernel(x_ref, w_ref, o_ref):
    o_ref[...] = _dot(x_ref[...], w_ref[...]).astype(o_ref.dtype)


def _matmul(x, w, name, tm=512, tn=1024):
    m, k = x.shape
    n = w.shape[1]
    tm = min(tm, m)
    tn = min(tn, n)
    return pl.pallas_call(
        _mm_kernel,
        grid=(n // tn, m // tm),
        in_specs=[pl.BlockSpec((tm, k), lambda j, i: (i, 0)), pl.BlockSpec((k, tn), lambda j, i: (0, j))],
        out_specs=pl.BlockSpec((tm, tn), lambda j, i: (i, j)),
        out_shape=jax.ShapeDtypeStruct((m, n), BF16),
        compiler_params=_params(("parallel", "parallel")),
        name=name,
    )(x, w)


def _rms(x, g):
    return x * lax.rsqrt(jnp.mean(x * x, -1, keepdims=True) + RMS_EPS) * g


def _mla_prep_kernel(cq_ref, ckv_ref, tab_ref, qg_ref, kvg_ref, wqn_ref, wqr_ref, wkn_ref, wv_ref,
                     q_ref, k_ref, v_ref, *, scale):
    tm = cq_ref.shape[0]
    cq = cq_ref[...].astype(F32)
    ckv_all = ckv_ref[...].astype(F32)
    ckv = ckv_all[:, :MLA_KV_RANK]
    kr = ckv_all[:, MLA_KV_RANK:MLA_KV_RANK + LANES]
    cqn = _rms(cq, qg_ref[...]).astype(BF16)
    ckvn = _rms(ckv, kvg_ref[...]).astype(BF16)
    qn = _dot(cqn, wqn_ref[...]) * scale
    qr = _dot(cqn, wqr_ref[...]) * scale
    kn = _dot(ckvn, wkn_ref[...])
    vv = _dot(ckvn, wv_ref[...])
    c, sa, sb = tab_ref[0], tab_ref[1], tab_ref[2]
    lo = lax.broadcasted_iota(I32, (tm, LANES), 1) < MLA_ROPE
    half = MLA_ROPE // 2
    krr = jnp.where(lo, _rope_tile(kr, c, sa, sb, half), 0.0).astype(BF16)
    for j in range(MLA_HEADS // 2):
        r = _rope_tile(qr[:, LANES * j:LANES * (j + 1)], c, sa, sb, half)
        q_ref[0, 2 * j, :, LANES:] = jnp.where(lo, r, 0.0).astype(BF16)
        q_ref[0, 2 * j + 1, :, LANES:] = jnp.where(lo, pltpu.roll(r, MLA_ROPE, 1), 0.0).astype(BF16)
    for h in range(MLA_HEADS):
        sl = slice(LANES * h, LANES * (h + 1))
        q_ref[0, h, :, :LANES] = qn[:, sl].astype(BF16)
        k_ref[0, h, :, :LANES] = kn[:, sl].astype(BF16)
        k_ref[0, h, :, LANES:] = krr
        v_ref[0, h] = vv[:, sl].astype(BF16)


def _mla_prep(proj, tab, q_norm, kv_norm, wqn, wqr, wkn, wv, b, s, col_cq, col_ckv):
    tm = min(256, s)
    nt = s // tm
    h = MLA_HEADS
    row = lambda bi, i: bi * nt + i
    const = lambda bi, i: (0, 0)
    return pl.pallas_call(
        functools.partial(_mla_prep_kernel, scale=(MLA_NOPE + MLA_ROPE) ** -0.5 * LOG2E),
        grid=(b, nt),
        in_specs=[
            pl.BlockSpec((tm, MLA_Q_RANK), lambda bi, i: (row(bi, i), col_cq)),
            pl.BlockSpec((tm, 512), lambda bi, i: (row(bi, i), col_ckv)),
            pl.BlockSpec((3, tm, LANES), lambda bi, i: (0, row(bi, i), 0)),
            pl.BlockSpec((1, MLA_Q_RANK), const),
            pl.BlockSpec((1, MLA_KV_RANK), const),
            pl.BlockSpec(wqn.shape, const),
            pl.BlockSpec(wqr.shape, const),
            pl.BlockSpec(wkn.shape, const),
            pl.BlockSpec(wv.shape, const),
        ],
        out_specs=[
            pl.BlockSpec((1, h, tm, MLA_QK_PAD), lambda bi, i: (bi, 0, i, 0)),
            pl.BlockSpec((1, h, tm, MLA_QK_PAD), lambda bi, i: (bi, 0, i, 0)),
            pl.BlockSpec((1, h, tm, MLA_V), lambda bi, i: (bi, 0, i, 0)),
        ],
        out_shape=[
            jax.ShapeDtypeStruct((b, h, s, MLA_QK_PAD), BF16),
            jax.ShapeDtypeStruct((b, h, s, MLA_QK_PAD), BF16),
            jax.ShapeDtypeStruct((b, h, s, MLA_V), BF16),
        ],
        compiler_params=_params(("parallel", "parallel")),
        name="mla_prep",
    )(proj, proj, tab, q_norm, kv_norm, wqn, wqr, wkn, wv)


def _diff_prep_kernel(dq_ref, tab_ref, o_ref):
    c, sa, sb = tab_ref[0], tab_ref[1], tab_ref[2]
    n = DIFF_HEADS * DIFF_QK
    for i in range(4):
        for j in range(n // LANES):
            x = dq_ref[:, n * i + LANES * j:n * i + LANES * (j + 1)].astype(F32)
            r = _rope_tile(x, c, sa, sb, DIFF_ROT // 2)
            if i < 2:
                r = r * (DIFF_QK ** -0.5 * LOG2E)
            o_ref[i, 0, :, LANES * j:LANES * (j + 1)] = r.astype(BF16)


def _diff_prep(proj, tab, b, s):
    tm = min(256, s)
    nt = s // tm
    n = DIFF_HEADS * DIFF_QK
    return pl.pallas_call(
        _diff_prep_kernel,
        grid=(b, nt),
        in_specs=[
            pl.BlockSpec((tm, 4 * n), lambda bi, i: (bi * nt + i, 0)),
            pl.BlockSpec((3, tm, LANES), lambda bi, i: (1, bi * nt + i, 0)),
        ],
        out_specs=pl.BlockSpec((4, 1, tm, n), lambda bi, i: (0, bi, i, 0)),
        out_shape=jax.ShapeDtypeStruct((4, b, s, n), BF16),
        compiler_params=_params(("parallel", "parallel")),
        name="diff_prep",
    )(proj, tab)


def _online_softmax_loop(qs, k_ats, v_at, nk):
    tq = qs[0].shape[0]

    def body(j, carry):
        v = v_at(j)
        out = []
        for q, k_at, (m, l, acc) in zip(qs, k_ats, carry):
            s = _dot_nt(q, k_at(j))
            m_new = jnp.maximum(m, jnp.max(s, axis=1, keepdims=True))
            alpha = jnp.exp2(m - m_new)
            p = jnp.exp2(s - m_new)
            l = alpha * l + jnp.sum(p, axis=1, keepdims=True)
            acc = alpha * acc + _dot(p.astype(BF16), v)
            out.append((m_new, l, acc))
        return tuple(out)

    one = (jnp.full((tq, 1), -jnp.inf, F32), jnp.zeros((tq, 1), F32), jnp.zeros((tq, v_at(0).shape[1]), F32))
    final = lax.fori_loop(0, nk, body, tuple(one for _ in qs), unroll=True)
    return [(l, acc) for _, l, acc in final]


def _mla_attn_kernel(q_ref, k_ref, v_ref, o_ref, *, tk):
    nk = k_ref.shape[2] // tk
    chunk = lambda j: pl.ds(pl.multiple_of(j * tk, tk), tk)
    (l, acc), = _online_softmax_loop(
        [q_ref[0, 0]], [lambda j: k_ref[0, 0, chunk(j), :]], lambda j: v_ref[0, 0, chunk(j), :], nk)
    o_ref[0] = (acc / l).astype(o_ref.dtype)


def _mla_attn(q, k, v):
    b, h, s, _ = q.shape
    tq = min(ATTN_TQ, s)
    return pl.pallas_call(
        functools.partial(_mla_attn_kernel, tk=min(ATTN_TK, s)),
        grid=(b, h, s // tq),
        in_specs=[
            pl.BlockSpec((1, 1, tq, MLA_QK_PAD), lambda bi, hi, i: (bi, hi, i, 0)),
            pl.BlockSpec((1, 1, s, MLA_QK_PAD), lambda bi, hi, i: (bi, hi, 0, 0)),
            pl.BlockSpec((1, 1, s, MLA_V), lambda bi, hi, i: (bi, hi, 0, 0)),
        ],
        out_specs=pl.BlockSpec((1, tq, MLA_V), lambda bi, hi, i: (bi, i, hi)),
        out_shape=jax.ShapeDtypeStruct((b, s, h * MLA_V), BF16),
        compiler_params=_params(("parallel", "parallel", "arbitrary")),
        name="mla_attn",
    )(q, k, v)


def _diff_attn_kernel(q1_ref, q2_ref, k1_ref, k2_ref, v_ref, lam_ref, g_ref, o_ref, *, tk, lam_init):
    hi = pl.program_id(1)
    tq = q1_ref.shape[2]
    nk = v_ref.shape[0] // tk
    chunk = lambda j: pl.ds(pl.multiple_of(j * tk, tk), tk)
    mine = (lax.broadcasted_iota(I32, (tq, LANES), 1) // DIFF_QK) == (hi % 2)
    zero = jnp.zeros((), BF16)
    qs = [jnp.where(mine, q1_ref[0, 0], zero), jnp.where(mine, q2_ref[0, 0], zero)]
    k_ats = [lambda j: k1_ref[0, 0, chunk(j), :], lambda j: k2_ref[0, 0, chunk(j), :]]
    (l1, a1), (l2, a2) = _online_softmax_loop(qs, k_ats, lambda j: v_ref[chunk(j), :], nk)
    lp = lam_ref[...]
    lam = (jnp.exp(jnp.sum(lp[0:1] * lp[1:2], axis=1, keepdims=True))
           - jnp.exp(jnp.sum(lp[2:3] * lp[3:4], axis=1, keepdims=True)) + lam_init)
    o = a1 / l1 - lam * (a2 / l2)
    o_ref[0] = (_rms(o, g_ref[...]) * (1.0 - lam_init)).astype(o_ref.dtype)


def _diff_attn(dqk, proj, col_v, lam_p, sub_norm, lam_init, b, s):
    h = DIFF_HEADS
    tq = min(ATTN_TQ, s)
    qspec = lambda which: pl.BlockSpec((1, 1, tq, LANES), lambda bi, hi, i: (which, bi, i, hi // 2))
    kspec = lambda which: pl.BlockSpec((1, 1, s, LANES), lambda bi, hi, i: (which, bi, 0, hi // 2))
    const = lambda bi, hi, i: (0, 0)
    return pl.pallas_call(
        functools.partial(_diff_attn_kernel, tk=min(ATTN_TK, s), lam_init=lam_init),
        grid=(b, h, s // tq),
        in_specs=[
            qspec(0), qspec(1), kspec(2), kspec(3),
            pl.BlockSpec((s, DIFF_V), lambda bi, hi, i: (bi, col_v + hi)),
            pl.BlockSpec((4, DIFF_QK), const),
            pl.BlockSpec((1, DIFF_V), const),
        ],
        out_specs=pl.BlockSpec((1, tq, DIFF_V), lambda bi, hi, i: (bi, i, hi)),
        out_shape=jax.ShapeDtypeStruct((b, s, h * DIFF_V), BF16),
        compiler_params=_params(("parallel", "parallel", "arbitrary")),
        name="diff_attn",
    )(dqk, dqk, dqk, dqk, proj, lam_p, sub_norm)


def _swa_prep_kernel(x_ref, tab_ref, o_ref):
    c, sa, sb = tab_ref[0], tab_ref[1], tab_ref[2]
    nq = SWA_Q_HEADS * SWA_HD // LANES
    for j in range(x_ref.shape[1] // LANES):
        x = x_ref[:, LANES * j:LANES * (j + 1)].astype(F32)
        r = _rope_tile(x, c, sa, sb, DIFF_ROT // 2)
        if j < nq:
            r = r * (SWA_HD ** -0.5 * LOG2E)
        o_ref[:, LANES * j:LANES * (j + 1)] = r.astype(BF16)


def _swa_prep(proj, tab, n_cols):
    t = proj.shape[0]
    tm = min(256, t)
    return pl.pallas_call(
        _swa_prep_kernel,
        grid=(t // tm,),
        in_specs=[
            pl.BlockSpec((tm, n_cols), lambda i: (i, 0)),
            pl.BlockSpec((3, tm, LANES), lambda i: (1, i, 0)),
        ],
        out_specs=pl.BlockSpec((tm, n_cols), lambda i: (i, 0)),
        out_shape=jax.ShapeDtypeStruct((t, n_cols), BF16),
        compiler_params=_params(("parallel",)),
        name="swa_prep",
    )(proj, tab)


def _swa_attn_kernel(sink_ref, bias_ref, q_ref, kp_ref, km_ref, kn_ref, vp_ref, vm_ref, vn_ref, o_ref):
    g = pl.program_id(1)
    tq = q_ref.shape[0]
    k = jnp.concatenate([kp_ref[...], km_ref[...], kn_ref[...]], axis=0)
    v = jnp.concatenate([vp_ref[...], vm_ref[...], vn_ref[...]], axis=0)
    bias = bias_ref[0]
    lane = lax.broadcasted_iota(I32, (tq, LANES), 1)
    zero = jnp.zeros((), BF16)
    qs = []
    for h in range(SWA_GROUP):
        qp = q_ref[:, LANES * (h // 2):LANES * (h // 2 + 1)]
        qs.append(jnp.where((lane // SWA_HD) == h % 2, qp, zero))
    s_all = _dot_nt(jnp.concatenate(qs, axis=0), k)
    ps, denoms = [], []
    for h in range(SWA_GROUP):
        sink = sink_ref[g * SWA_GROUP + h] * LOG2E
        sc = s_all[h * tq:(h + 1) * tq] + bias
        m = jnp.maximum(jnp.max(sc, axis=1, keepdims=True), sink)
        p = jnp.exp2(sc - m)
        denoms.append(jnp.sum(p, axis=1, keepdims=True) + jnp.exp2(sink - m))
        ps.append(p.astype(BF16))
    o_all = _dot(jnp.concatenate(ps, axis=0), v)
    for pair in range(SWA_GROUP // 2):
        lo = o_all[2 * pair * tq:(2 * pair + 1) * tq] / denoms[2 * pair]
        hi = o_all[(2 * pair + 1) * tq:(2 * pair + 2) * tq] / denoms[2 * pair + 1]
        o_ref[:, LANES * pair:LANES * (pair + 1)] = jnp.where(lane < SWA_HD, lo, hi).astype(o_ref.dtype)


def _swa_bias(tq):
    qi = np.arange(tq)[:, None]
    kj = np.arange(tq + 2 * WINDOW)[None, :] - WINDOW
    band = np.abs(kj - qi) <= WINDOW
    masks = [band, band & (kj >= 0), band & (kj < tq), band & (kj >= 0) & (kj < tq)]
    return jnp.asarray(np.where(np.stack(masks), 0.0, -np.inf), F32)


def _swa_attn(qk, proj, sink, b, s, col_k, col_v, tq=256):
    tq = min(tq, s)
    nt = s // tq
    r = tq // WINDOW
    nwb = s // WINDOW
    gw = SWA_GROUP * SWA_HD
    row = lambda bi, i: bi * nt + i
    prev = lambda bi, i: bi * nwb + jnp.maximum(i * r - 1, 0)
    nxt = lambda bi, i: bi * nwb + jnp.minimum((i + 1) * r, nwb - 1)
    edge = lambda i: jnp.where(i == 0, 1, 0) + jnp.where(i == nt - 1, 2, 0)
    grid_spec = pltpu.PrefetchScalarGridSpec(
        num_scalar_prefetch=1,
        grid=(b, SWA_KV_HEADS, nt),
        in_specs=[
            pl.BlockSpec((1, tq, tq + 2 * WINDOW), lambda bi, g, i, sk: (edge(i), 0, 0)),
            pl.BlockSpec((tq, gw), lambda bi, g, i, sk: (row(bi, i), g)),
            pl.BlockSpec((WINDOW, LANES), lambda bi, g, i, sk: (prev(bi, i), col_k + g)),
            pl.BlockSpec((tq, LANES), lambda bi, g, i, sk: (row(bi, i), col_k + g)),
            pl.BlockSpec((WINDOW, LANES), lambda bi, g, i, sk: (nxt(bi, i), col_k + g)),
            pl.BlockSpec((WINDOW, LANES), lambda bi, g, i, sk: (prev(bi, i), col_v + g)),
            pl.BlockSpec((tq, LANES), lambda bi, g, i, sk: (row(bi, i), col_v + g)),
            pl.BlockSpec((WINDOW, LANES), lambda bi, g, i, sk: (nxt(bi, i), col_v + g)),
        ],
        out_specs=pl.BlockSpec((tq, gw), lambda bi, g, i, sk: (row(bi, i), g)),
    )
    return pl.pallas_call(
        _swa_attn_kernel,
        grid_spec=grid_spec,
        out_shape=jax.ShapeDtypeStruct((b * s, SWA_Q_HEADS * SWA_HD), BF16),
        compiler_params=_params(("parallel", "parallel", "parallel")),
        name="swa_attn",
    )(sink, _swa_bias(tq), qk, qk, qk, qk, proj, proj, proj)


def _layer_norm(y, g, b):
    mu = jnp.mean(y, -1, keepdims=True)
    d = y - mu
    var = jnp.mean(d * d, -1, keepdims=True)
    return d * lax.rsqrt(var + LN_EPS) * g + b


PACK_SUBLANES = D_MODEL // 2 // LANES
HI_MASK = -65536


def _pack_words(lo, hi):
    lb = lax.bitcast_convert_type(lo.astype(BF16).astype(F32), I32)
    hb = lax.bitcast_convert_type(hi.astype(BF16).astype(F32), I32)
    return (hb & HI_MASK) | lax.shift_right_logical(lb, 16)


def _unpack_words(w):
    return lax.bitcast_convert_type(w << 16, F32), lax.bitcast_convert_type(w & HI_MASK, F32)


def _word_rows(j, tm, lead=(), base=0):
    return lead + (pl.ds(base * PACK_SUBLANES + j, tm, stride=PACK_SUBLANES), slice(None))


def _store_packed(ref, y, keep=None, clear=None):
    tm, d = y.shape
    for j in range(PACK_SUBLANES):
        w = _pack_words(y[:, LANES * j:LANES * (j + 1)], y[:, d // 2 + LANES * j:d // 2 + LANES * (j + 1)])
        if keep is not None:
            w = jnp.where(keep, w, jnp.where(clear, 0, ref[_word_rows(j, tm)]))
        ref[_word_rows(j, tm)] = w


def _load_packed_bf16(ref, tm, base=0):
    los, his = [], []
    for j in range(PACK_SUBLANES):
        lo, hi = _unpack_words(ref[_word_rows(j, tm, base=base)])
        los.append(lo.astype(BF16))
        his.append(hi.astype(BF16))
    return jnp.concatenate(los + his, axis=1)


def _out_ln_kernel(*refs, n_in):
    o_refs = refs[:n_in]
    w_refs = refs[n_in:2 * n_in]
    h_ref, g_ref, b_ref, hf_ref, hp_ref = refs[2 * n_in:]
    mix = _dot(o_refs[0][...], w_refs[0][...])
    for o_ref, w_ref in zip(o_refs[1:], w_refs[1:]):
        mix = mix + _dot(o_ref[...], w_ref[...])
    out = _layer_norm(DEEPNORM_ALPHA * h_ref[...] + mix, g_ref[...], b_ref[...])
    hf_ref[...] = out
    _store_packed(hp_ref, out)


def _out_ln(os_, ws, h, g, b):
    t, d = h.shape
    tm = min(256, t)
    n_in = len(os_)
    const = lambda i: (0, 0)
    rowblk = lambda a: pl.BlockSpec((tm, a.shape[1]), lambda i: (i, 0))
    return pl.pallas_call(
        functools.partial(_out_ln_kernel, n_in=n_in),
        grid=(t // tm,),
        in_specs=[rowblk(o) for o in os_] + [pl.BlockSpec(w.shape, const) for w in ws]
        + [rowblk(h), pl.BlockSpec((1, d), const), pl.BlockSpec((1, d), const)],
        out_specs=[pl.BlockSpec((tm, d), lambda i: (i, 0)),
                   pl.BlockSpec((tm * PACK_SUBLANES, LANES), lambda i: (i, 0))],
        out_shape=[jax.ShapeDtypeStruct((t, d), F32), jax.ShapeDtypeStruct((t * PACK_SUBLANES, LANES), I32)],
        compiler_params=_params(("parallel",)),
        name="out_proj_ln",
    )(*os_, *ws, h, g, b)


def _router_kernel(h_ref, wh_ref, wl_ref, bias_ref, idx_ref, gate_ref, rank_ref, cnt_ref, cnt_scr):
    i = pl.program_id(0)
    tm = h_ref.shape[0]

    @pl.when(i == 0)
    def _():
        cnt_scr[...] = jnp.zeros(cnt_scr.shape, F32)

    h = h_ref[...]
    hh = h.astype(BF16)
    hl = (h - hh.astype(F32)).astype(BF16)
    logits = _dot(hh, wh_ref[...]) + (_dot(hh, wl_ref[...]) + _dot(hl, wh_ref[...]))
    scores = jax.nn.sigmoid(logits)
    sel = scores + bias_ref[...]
    e_iota = lax.broadcasted_iota(I32, (tm, N_EXPERTS), 1).astype(F32)
    out_iota = lax.broadcasted_iota(I32, (tm, LANES), 1)
    onehots, gates = [], []
    idx_out = jnp.zeros((tm, LANES), F32)
    for k in range(TOP_K):
        m = jnp.max(sel, axis=1, keepdims=True)
        pick = jnp.min(jnp.where(sel == m, e_iota, float(N_EXPERTS)), axis=1, keepdims=True)
        oh = e_iota == pick
        onehots.append(oh)
        gates.append(jnp.sum(jnp.where(oh, scores, 0.0), axis=1, keepdims=True))
        idx_out = jnp.where(out_iota == k, pick, idx_out)
        sel = jnp.where(oh, -jnp.inf, sel)
    maskf = onehots[0].astype(F32)
    for oh in onehots[1:]:
        maskf = maskf + oh.astype(F32)
    tri = (lax.broadcasted_iota(I32, (tm, tm), 1) < lax.broadcasted_iota(I32, (tm, tm), 0)).astype(BF16)
    ranks = cnt_scr[...] + _dot(tri, maskf.astype(BF16))
    gsum = gates[0]
    for gk in gates[1:]:
        gsum = gsum + gk
    gate_out = jnp.zeros((tm, LANES), F32)
    rank_out = jnp.zeros((tm, LANES), F32)
    for k in range(TOP_K):
        gate_out = jnp.where(out_iota == k, gates[k] / gsum * ROUTED_SCALE, gate_out)
        rk = jnp.sum(jnp.where(onehots[k], ranks, 0.0), axis=1, keepdims=True)
        rank_out = jnp.where(out_iota == k, rk, rank_out)
    idx_ref[...] = idx_out.astype(I32)
    gate_ref[...] = gate_out
    rank_ref[...] = rank_out.astype(I32)
    total = cnt_scr[...] + jnp.sum(maskf, axis=0, keepdims=True)
    cnt_scr[...] = total
    cnt_ref[...] = total


def _router(h, w_hi, w_lo, bias):
    t, d = h.shape
    tm = min(256, t)
    const = lambda i: (0, 0)
    out = lambda: pl.BlockSpec((tm, LANES), lambda i: (i, 0))
    return pl.pallas_call(
        _router_kernel,
        grid=(t // tm,),
        in_specs=[
            pl.BlockSpec((tm, d), lambda i: (i, 0)),
            pl.BlockSpec((d, N_EXPERTS), const),
            pl.BlockSpec((d, N_EXPERTS), const),
            pl.BlockSpec((1, N_EXPERTS), const),
        ],
        out_specs=[out(), out(), out(), pl.BlockSpec((1, N_EXPERTS), const)],
        out_shape=[
            jax.ShapeDtypeStruct((t, LANES), I32),
            jax.ShapeDtypeStruct((t, LANES), F32),
            jax.ShapeDtypeStruct((t, LANES), I32),
            jax.ShapeDtypeStruct((1, N_EXPERTS), F32),
        ],
        scratch_shapes=[pltpu.VMEM((1, N_EXPERTS), F32)],
        compiler_params=_params(("arbitrary",)),
        name="router",
    )(h, w_hi, w_lo, bias)


def _token_tile(ref, r, lead=()):
    return ref.at[lead + (pl.ds(pl.multiple_of(r * PACK_SUBLANES, PACK_SUBLANES), PACK_SUBLANES),)]


def _dispatch_kernel(dest_ref, x_ref, xs_ref, sem):
    tm = x_ref.shape[0] // PACK_SUBLANES

    def body(r, carry):
        src = _token_tile(x_ref, r)
        for k in range(TOP_K):
            pltpu.make_async_copy(src, xs_ref.at[dest_ref[r * TOP_K + k]], sem).start(priority=k % 2)
        return carry

    lax.fori_loop(0, tm, body, 0)
    for k in range(TOP_K):
        pltpu.make_async_copy(xs_ref.at[pl.ds(0, tm)], xs_ref.at[pl.ds(0, tm)], sem).wait()


def _dispatch(hp, dest_flat):
    t = hp.shape[0] // PACK_SUBLANES
    tm = min(256, t)
    return pl.pallas_call(
        _dispatch_kernel,
        grid=(t // tm,),
        in_specs=[
            pl.BlockSpec((tm * TOP_K,), lambda i: (i,), memory_space=pltpu.SMEM),
            pl.BlockSpec((tm * PACK_SUBLANES, LANES), lambda i: (i, 0)),
        ],
        out_specs=pl.BlockSpec(memory_space=pl.ANY),
        out_shape=jax.ShapeDtypeStruct((t * TOP_K, PACK_SUBLANES, LANES), I32),
        scratch_shapes=[pltpu.SemaphoreType.DMA(())],
        compiler_params=_params(("arbitrary",)),
        name="moe_dispatch",
    )(dest_flat, hp)


def _expert_kernel(tile_ref, exp_ref, lo_ref, hi_ref, first_ref, x_ref, wg_ref, wu_ref, wd_ref, y_ref,
                   wg_bf, wu_bf, wd_bf):
    w = pl.program_id(0)
    tm = x_ref.shape[0] // PACK_SUBLANES
    lo = lo_ref[w]
    hi = hi_ref[w]

    @pl.when((w == 0) | (exp_ref[w] != exp_ref[jnp.maximum(w - 1, 0)]))
    def _():
        wg_bf[...] = wg_ref[0, 0].astype(BF16)
        wu_bf[...] = wu_ref[0, 0].astype(BF16)
        wd_bf[...] = wd_ref[0, 0].astype(BF16)

    @pl.when(hi > lo)
    def _():
        x = _load_packed_bf16(x_ref, tm)
        gate = _dot(x, wg_bf[...])
        up = _dot(x, wu_bf[...])
        hmid = (gate * jax.nn.sigmoid(gate) * up).astype(BF16)
        y = _dot(hmid, wd_bf[...])
        rows = tile_ref[w] * tm + lax.broadcasted_iota(I32, (tm, 1), 0)
        mine = (rows >= lo) & (rows < hi)
        _store_packed(y_ref, y, keep=mine, clear=first_ref[w] == 1)


def _experts(xs, wg, wu, wd, layer, tile_w, exp_w, lo_w, hi_w, first_w, tm):
    d = D_MODEL
    rows = pl.BlockSpec((tm * PACK_SUBLANES, LANES), lambda w, tl, ex, lo, hi, fi: (tl[w], 0))
    nw = tile_w.shape[0]
    grid_spec = pltpu.PrefetchScalarGridSpec(
        num_scalar_prefetch=5,
        grid=(nw,),
        in_specs=[
            rows,
            pl.BlockSpec((1, 1, d, D_EXPERT), lambda w, tl, ex, lo, hi, fi: (layer, ex[w], 0, 0)),
            pl.BlockSpec((1, 1, d, D_EXPERT), lambda w, tl, ex, lo, hi, fi: (layer, ex[w], 0, 0)),
            pl.BlockSpec((1, 1, D_EXPERT, d), lambda w, tl, ex, lo, hi, fi: (layer, ex[w], 0, 0)),
        ],
        out_specs=rows,
        scratch_shapes=[pltpu.VMEM((d, D_EXPERT), BF16), pltpu.VMEM((d, D_EXPERT), BF16),
                        pltpu.VMEM((D_EXPERT, d), BF16)],
    )
    return pl.pallas_call(
        _expert_kernel,
        grid_spec=grid_spec,
        out_shape=jax.ShapeDtypeStruct(xs.shape, I32),
        compiler_params=_params(("arbitrary",)),
        name="moe_experts",
    )(tile_w, exp_w, lo_w, hi_w, first_w, xs, wg, wu, wd)


COMBINE_HALF = 128


def _combine_kernel(dcur_ref, dnext_ref, ys_ref, gate_ref, h_ref, hp_ref, sg_ref, su_ref, sd_ref, g_ref, b_ref,
                    hf_out, hb_out, buf0, buf1, sem0, sem1):
    i = pl.program_id(0)
    th = buf0.shape[1] // PACK_SUBLANES

    def gather(dref, base, buf, sem):
        for r in range(th):
            for k in range(TOP_K):
                src = ys_ref.at[dref[(base + r) * TOP_K + k]]
                pltpu.make_async_copy(src, _token_tile(buf, r, (k,)), sem).start(priority=k % 2)

    def wait(sem):
        for k in range(TOP_K):
            pltpu.make_async_copy(ys_ref.at[pl.ds(0, th)], ys_ref.at[pl.ds(0, th)], sem).wait()

    def reduce_half(half, buf):
        rows = pl.ds(half * th, th)
        x = _load_packed_bf16(hp_ref, th, base=half * th)
        gate = _dot(x, sg_ref[...])
        up = _dot(x, su_ref[...])
        acc = _dot((gate * jax.nn.sigmoid(gate) * up).astype(BF16), sd_ref[...])
        acc = acc + DEEPNORM_ALPHA * h_ref[rows, :]
        gates = gate_ref[rows, :]
        los = [None] * PACK_SUBLANES
        his = [None] * PACK_SUBLANES
        for k in range(TOP_K):
            gk = gates[:, k:k + 1]
            for j in range(PACK_SUBLANES):
                lo, hi = _unpack_words(buf[_word_rows(j, th, (k,))])
                los[j] = gk * lo if k == 0 else los[j] + gk * lo
                his[j] = gk * hi if k == 0 else his[j] + gk * hi
        out = _layer_norm(acc + jnp.concatenate(los + his, axis=1), g_ref[...], b_ref[...])
        hf_out[rows, :] = out
        hb_out[rows, :] = out.astype(BF16)

    @pl.when(i == 0)
    def _():
        def body(r, carry):
            for k in range(TOP_K):
                src = ys_ref.at[dcur_ref[r * TOP_K + k]]
                pltpu.make_async_copy(src, _token_tile(buf0, r, (k,)), sem0).start(priority=k % 2)
            return carry

        lax.fori_loop(0, th, body, 0)

    wait(sem0)
    gather(dcur_ref, th, buf1, sem1)
    reduce_half(0, buf0)
    wait(sem1)
    gather(dnext_ref, 0, buf0, sem0)
    reduce_half(1, buf1)

    @pl.when(i == pl.num_programs(0) - 1)
    def _():
        wait(sem0)


def _combine(dest_flat, ys, gate, h, hp, sg, su, sd, g, b):
    t, d = h.shape
    th = min(COMBINE_HALF, t // 2)
    tm = 2 * th
    n = t // tm
    const = lambda i: (0, 0)
    row = lambda width: pl.BlockSpec((tm, width), lambda i: (i, 0))
    buf = pltpu.VMEM((TOP_K, th * PACK_SUBLANES, LANES), I32)
    return pl.pallas_call(
        _combine_kernel,
        grid=(n,),
        in_specs=[
            pl.BlockSpec((tm * TOP_K,), lambda i: (i,), memory_space=pltpu.SMEM),
            pl.BlockSpec((th * TOP_K,), lambda i: (jnp.minimum(2 * i + 2, 2 * n - 1),), memory_space=pltpu.SMEM),
            pl.BlockSpec(memory_space=pl.ANY),
            row(LANES), row(d), pl.BlockSpec((tm * PACK_SUBLANES, LANES), lambda i: (i, 0)),
            pl.BlockSpec(sg.shape, const), pl.BlockSpec(su.shape, const), pl.BlockSpec(sd.shape, const),
            pl.BlockSpec((1, d), const), pl.BlockSpec((1, d), const),
        ],
        out_specs=[row(d), row(d)],
        out_shape=[jax.ShapeDtypeStruct((t, d), F32), jax.ShapeDtypeStruct((t, d), BF16)],
        scratch_shapes=[buf, buf, pltpu.SemaphoreType.DMA(()), pltpu.SemaphoreType.DMA(())],
        compiler_params=_params(("arbitrary",)),
        name="moe_combine",
    )(dest_flat, dest_flat, ys, gate, h, hp, sg, su, sd, g, b)


EXPERT_TILE = 512


def _group_metadata(counts, idx, rank, n_assign):
    tm = min(EXPERT_TILE, n_assign)
    nt = n_assign // tm
    nw = nt + N_EXPERTS - 1
    sizes = counts.reshape(N_EXPERTS).astype(I32)
    ends = jnp.cumsum(sizes)
    starts = ends - sizes
    my_start = jnp.sum(jnp.where(idx[..., None] == jnp.arange(N_EXPERTS, dtype=I32), starts, 0), axis=-1)
    dest = (my_start + rank).reshape(-1).astype(I32)
    first_tile = starts // tm
    n_e = jnp.where(sizes > 0, (ends - 1) // tm - first_tile + 1, 0)
    cum = jnp.cumsum(n_e)
    off = cum - n_e
    total = cum[-1]
    w = jnp.arange(nw, dtype=I32)
    valid = w < total
    experts = jnp.arange(N_EXPERTS, dtype=I32)
    last_used = jnp.max(jnp.where(n_e > 0, experts, 0))
    e_w = jnp.where(valid, jnp.sum(cum[None, :] <= w[:, None], axis=1), last_used).astype(I32)
    pick = lambda table: jnp.sum(jnp.where(e_w[:, None] == experts, table, 0), axis=1)
    tile_w = jnp.where(valid, pick(first_tile) + (w - pick(off)), nt - 1).astype(I32)
    lo_w = jnp.where(valid, jnp.maximum(pick(starts), tile_w * tm), 0).astype(I32)
    hi_w = jnp.where(valid, jnp.minimum(pick(ends), (tile_w + 1) * tm), 0).astype(I32)
    first_w = jnp.concatenate([jnp.ones((1,), I32), (tile_w[1:] != tile_w[:-1]).astype(I32)])
    return dest, tile_w, e_w, lo_w, hi_w, first_w, tm


def _moe(h, hp, r_hi, r_lo, r_bias, wg, wu, wd, layer, sg, su, sd, g, b):
    t = h.shape[0]
    a = t * TOP_K
    idx, gate, rank, counts = _router(h, r_hi, r_lo, r_bias)
    dest, tile_w, e_w, lo_w, hi_w, first_w, tm = _group_metadata(counts, idx[:, :TOP_K], rank[:, :TOP_K], a)
    xs = _dispatch(hp, dest)
    ys = _experts(xs.reshape(a * PACK_SUBLANES, LANES), wg, wu, wd, layer, tile_w, e_w, lo_w, hi_w, first_w, tm)
    return _combine(dest, ys.reshape(a, PACK_SUBLANES, LANES), gate, h, hp, sg, su, sd, g, b)


def _even_weights(w_in, w_uq, w_ukv, w_out):
    o1 = MLA_Q_RANK
    o2 = o1 + MLA_KV_RANK
    o3 = o2 + MLA_ROPE
    o4 = o3 + 4 * DIFF_HEADS * DIFF_QK
    c_q, c_kv, k_r, dq, dv = w_in[:, :o1], w_in[:, o1:o2], w_in[:, o2:o3], w_in[:, o3:o4], w_in[:, o4:]
    dq = dq.reshape(D_MODEL, DIFF_HEADS, 4, DIFF_QK).transpose(0, 2, 1, 3).reshape(D_MODEL, -1)
    pad = jnp.zeros((D_MODEL, 512 - MLA_KV_RANK - MLA_ROPE), w_in.dtype)
    w_proj = jnp.concatenate([dq, dv, c_q, c_kv, k_r, pad], axis=1).astype(BF16)
    uq = w_uq.reshape(MLA_Q_RANK, MLA_HEADS, MLA_NOPE + MLA_ROPE)
    wqn = uq[:, :, :MLA_NOPE].reshape(MLA_Q_RANK, -1).astype(BF16)
    wqr = uq[:, :, MLA_NOPE:].reshape(MLA_Q_RANK, -1).astype(BF16)
    ukv = w_ukv.reshape(MLA_KV_RANK, MLA_HEADS, MLA_NOPE + MLA_V)
    wkn = ukv[:, :, :MLA_NOPE].reshape(MLA_KV_RANK, -1).astype(BF16)
    wv = ukv[:, :, MLA_NOPE:].reshape(MLA_KV_RANK, -1).astype(BF16)
    n_mla = MLA_HEADS * MLA_V
    return w_proj, wqn, wqr, wkn, wv, w_out[:n_mla].astype(BF16), w_out[n_mla:].astype(BF16)


def _odd_weights(w_in, w_out):
    nq = SWA_Q_HEADS * SWA_HD
    nkv = SWA_KV_HEADS * SWA_HD
    dup = lambda w: jnp.concatenate([w.reshape(D_MODEL, SWA_KV_HEADS, 1, SWA_HD)] * 2, axis=2).reshape(D_MODEL, -1)
    w_proj = jnp.concatenate([w_in[:, :nq], dup(w_in[:, nq:nq + nkv]), dup(w_in[:, nq + nkv:])], axis=1)
    return w_proj.astype(BF16), w_out.astype(BF16)


def kernel(x, positions, ab_w_in, mla_q_norm, mla_w_uq, mla_kv_norm, mla_w_ukv, diff_lambda_q1, diff_lambda_k1, diff_lambda_q2, diff_lambda_k2, diff_sub_norm, ab_w_out, swa_w_in, swa_sink, swa_w_out, mix_ln_g, mix_ln_b, ffn_ln_g, ffn_ln_b, router_w, router_bias, exp_w_gate, exp_w_up, exp_w_down, shared_w_gate, shared_w_up, shared_w_down):
    b, s, d = x.shape
    t = b * s
    tab = _rope_tables(positions)
    h = x.reshape(t, d)
    hb = h.astype(BF16)
    row = lambda v: v.reshape(1, -1)
    for layer in range(DEPTH):
        i = layer // 2
        if layer % 2 == 0:
            w_proj, wqn, wqr, wkn, wv, wo_mla, wo_diff = _even_weights(
                ab_w_in[i], mla_w_uq[i], mla_w_ukv[i], ab_w_out[i])
            proj = _matmul(hb, w_proj, "even_in_proj")
            q, k, v = _mla_prep(proj, tab, row(mla_q_norm[i]), row(mla_kv_norm[i]), wqn, wqr, wkn, wv,
                                b, s, col_cq=6, col_ckv=7)
            o_mla = _mla_attn(q, k, v).reshape(t, -1)
            dqk = _diff_prep(proj, tab, b, s)
            lam_p = jnp.stack([diff_lambda_q1[i], diff_lambda_k1[i], diff_lambda_q2[i], diff_lambda_k2[i]])
            lam_init = 0.8 - 0.6 * math.exp(-0.3 * layer)
            o_diff = _diff_attn(dqk, proj, 2048 // DIFF_V, lam_p, row(diff_sub_norm[i]), lam_init,
                                b, s).reshape(t, -1)
            h, hp = _out_ln([o_mla, o_diff], [wo_mla, wo_diff], h, row(mix_ln_g[layer]), row(mix_ln_b[layer]))
        else:
            w_proj, wo = _odd_weights(swa_w_in[i], swa_w_out[i])
            proj = _matmul(hb, w_proj, "odd_in_proj")
            nq = SWA_Q_HEADS * SWA_HD
            n_rot = nq + SWA_KV_HEADS * LANES
            qk = _swa_prep(proj, tab, n_rot)
            o = _swa_attn(qk, proj, swa_sink[i], b, s, col_k=nq // LANES, col_v=n_rot // LANES)
            h, hp = _out_ln([o], [wo], h, row(mix_ln_g[layer]), row(mix_ln_b[layer]))
        rw = router_w[layer]
        r_hi = rw.astype(BF16)
        r_lo = (rw - r_hi.astype(F32)).astype(BF16)
        h, hb = _moe(h, hp, r_hi, r_lo, row(router_bias[layer]),
                     exp_w_gate, exp_w_up, exp_w_down, layer,
                     shared_w_gate[layer].astype(BF16), shared_w_up[layer].astype(BF16),
                     shared_w_down[layer].astype(BF16), row(ffn_ln_g[layer]), row(ffn_ln_b[layer]))
    return h.reshape(b, s, d)
```

```python
import functools
import math

import numpy as np
import jax
import jax.numpy as jnp
from jax import lax
from jax.experimental import pallas as pl
from jax.experimental.pallas import tpu as pltpu

F32 = jnp.float32
BF16 = jnp.bfloat16
I32 = jnp.int32

D_MODEL = 2048
DEPTH = 4
ROPE_THETA = 500000.0
LN_EPS = 1e-5
RMS_EPS = 1e-6
MLA_HEADS = 8
MLA_NOPE = 128
MLA_ROPE = 64
MLA_V = 128
MLA_Q_RANK = 512
MLA_KV_RANK = 256
DIFF_HEADS = 8
DIFF_QK = 64
DIFF_V = 128
DIFF_ROT = DIFF_QK // 4
SWA_Q_HEADS = 32
SWA_KV_HEADS = 4
SWA_GROUP = SWA_Q_HEADS // SWA_KV_HEADS
SWA_HD = 64
WINDOW = 128
N_EXPERTS = 64
TOP_K = 8
D_EXPERT = 256
ROUTED_SCALE = 2.5
DEEPNORM_ALPHA = (2 * DEPTH) ** 0.25
LOG2E = math.log2(math.e)

LANES = 128
MLA_QK_PAD = 256
VMEM_LIMIT = 48 << 20
ATTN_TQ = 512
ATTN_TK = 2048


def _params(sem):
    return pltpu.CompilerParams(dimension_semantics=sem, vmem_limit_bytes=VMEM_LIMIT)


def _dot(a, b):
    return jnp.dot(a, b, preferred_element_type=F32)


def _dot_nt(a, b):
    return lax.dot_general(a, b, (((1,), (1,)), ((), ())), preferred_element_type=F32)


def _tables_kernel(pos_ref, c_ref, out_ref):
    pos = pos_ref[...]
    a64 = pos * c_ref[0:1, :]
    a16 = pos * c_ref[3:4, :]
    s64 = jnp.sin(a64)
    s16 = jnp.sin(a16)
    out_ref[0] = jnp.cos(a64)
    out_ref[1] = s64 * c_ref[1:2, :]
    out_ref[2] = s64 * c_ref[2:3, :]
    out_ref[3] = jnp.cos(a16)
    out_ref[4] = s16 * c_ref[4:5, :]
    out_ref[5] = s16 * c_ref[5:6, :]


def _rope_consts():
    j = np.arange(LANES) % 64
    inv64 = ROPE_THETA ** (-jnp.arange(0, MLA_ROPE, 2, dtype=F32) / MLA_ROPE)
    inv16 = ROPE_THETA ** (-jnp.arange(0, DIFF_ROT, 2, dtype=F32) / DIFF_ROT)
    f64 = inv64[j % 32]
    f16 = jnp.where(j < 16, inv16[j % 8], 0.0)
    rows = [
        f64,
        jnp.asarray(np.where(j < 32, -1.0, 0.0), F32),
        jnp.asarray(np.where(j >= 32, 1.0, 0.0), F32),
        f16,
        jnp.asarray(np.where(j < 8, -1.0, 0.0), F32),
        jnp.asarray(np.where((j >= 8) & (j < 16), 1.0, 0.0), F32),
        jnp.zeros((LANES,), F32),
        jnp.zeros((LANES,), F32),
    ]
    return jnp.stack(rows).astype(F32)


def _rope_tables(positions):
    t = positions.size
    pos = jnp.broadcast_to(positions.reshape(t, 1).astype(F32), (t, LANES))
    tm = min(512, t)
    return pl.pallas_call(
        _tables_kernel,
        grid=(t // tm,),
        in_specs=[pl.BlockSpec((tm, LANES), lambda i: (i, 0)), pl.BlockSpec((8, LANES), lambda i: (0, 0))],
        out_specs=pl.BlockSpec((6, tm, LANES), lambda i: (0, i, 0)),
        out_shape=jax.ShapeDtypeStruct((6, t, LANES), F32),
        compiler_params=_params(("parallel",)),
        name="rope_tables",
    )(pos, _rope_consts())


def _rope_tile(x, c, sa, sb, half):
    return x * c + pltpu.roll(x, LANES - half, 1) * sa + pltpu.roll(x, half, 1) * sb


def _mm_kernel(x_ref, w_ref, o_ref):
    o_ref[...] = _dot(x_ref[...], w_ref[...]).astype(o_ref.dtype)


def _matmul(x, w, name, tm=512, tn=1024):
    m, k = x.shape
    n = w.shape[1]
    tm = min(tm, m)
    tn = min(tn, n)
    return pl.pallas_call(
        _mm_kernel,
        grid=(n // tn, m // tm),
        in_specs=[pl.BlockSpec((tm, k), lambda j, i: (i, 0)), pl.BlockSpec((k, tn), lambda j, i: (0, j))],
        out_specs=pl.BlockSpec((tm, tn), lambda j, i: (i, j)),
        out_shape=jax.ShapeDtypeStruct((m, n), BF16),
        compiler_params=_params(("parallel", "parallel")),
        name=name,
    )(x, w)


def _rms(x, g):
    return x * lax.rsqrt(jnp.mean(x * x, -1, keepdims=True) + RMS_EPS) * g


def _mla_prep_kernel(cq_ref, ckv_ref, tab_ref, qg_ref, kvg_ref, wqn_ref, wqr_ref, wkn_ref, wv_ref,
                     q_ref, k_ref, v_ref, *, scale):
    tm = cq_ref.shape[0]
    cq = cq_ref[...].astype(F32)
    ckv_all = ckv_ref[...].astype(F32)
    ckv = ckv_all[:, :MLA_KV_RANK]
    kr = ckv_all[:, MLA_KV_RANK:MLA_KV_RANK + LANES]
    cqn = _rms(cq, qg_ref[...]).astype(BF16)
    ckvn = _rms(ckv, kvg_ref[...]).astype(BF16)
    qn = _dot(cqn, wqn_ref[...]) * scale
    qr = _dot(cqn, wqr_ref[...]) * scale
    kn = _dot(ckvn, wkn_ref[...])
    vv = _dot(ckvn, wv_ref[...])
    c, sa, sb = tab_ref[0], tab_ref[1], tab_ref[2]
    lo = lax.broadcasted_iota(I32, (tm, LANES), 1) < MLA_ROPE
    half = MLA_ROPE // 2
    krr = jnp.where(lo, _rope_tile(kr, c, sa, sb, half), 0.0).astype(BF16)
    for j in range(MLA_HEADS // 2):
        r = _rope_tile(qr[:, LANES * j:LANES * (j + 1)], c, sa, sb, half)
        q_ref[0, 2 * j, :, LANES:] = jnp.where(lo, r, 0.0).astype(BF16)
        q_ref[0, 2 * j + 1, :, LANES:] = jnp.where(lo, pltpu.roll(r, MLA_ROPE, 1), 0.0).astype(BF16)
    for h in range(MLA_HEADS):
        sl = slice(LANES * h, LANES * (h + 1))
        q_ref[0, h, :, :LANES] = qn[:, sl].astype(BF16)
        k_ref[0, h, :, :LANES] = kn[:, sl].astype(BF16)
        k_ref[0, h, :, LANES:] = krr
        v_ref[0, h] = vv[:, sl].astype(BF16)


def _mla_prep(proj, tab, q_norm, kv_norm, wqn, wqr, wkn, wv, b, s, col_cq, col_ckv):
    tm = min(256, s)
    nt = s // tm
    h = MLA_HEADS
    row = lambda bi, i: bi * nt + i
    const = lambda bi, i: (0, 0)
    return pl.pallas_call(
        functools.partial(_mla_prep_kernel, scale=(MLA_NOPE + MLA_ROPE) ** -0.5 * LOG2E),
        grid=(b, nt),
        in_specs=[
            pl.BlockSpec((tm, MLA_Q_RANK), lambda bi, i: (row(bi, i), col_cq)),
            pl.BlockSpec((tm, 512), lambda bi, i: (row(bi, i), col_ckv)),
            pl.BlockSpec((3, tm, LANES), lambda bi, i: (0, row(bi, i), 0)),
            pl.BlockSpec((1, MLA_Q_RANK), const),
            pl.BlockSpec((1, MLA_KV_RANK), const),
            pl.BlockSpec(wqn.shape, const),
            pl.BlockSpec(wqr.shape, const),
            pl.BlockSpec(wkn.shape, const),
            pl.BlockSpec(wv.shape, const),
        ],
        out_specs=[
            pl.BlockSpec((1, h, tm, MLA_QK_PAD), lambda bi, i: (bi, 0, i, 0)),
            pl.BlockSpec((1, h, tm, MLA_QK_PAD), lambda bi, i: (bi, 0, i, 0)),
            pl.BlockSpec((1, h, tm, MLA_V), lambda bi, i: (bi, 0, i, 0)),
        ],
        out_shape=[
            jax.ShapeDtypeStruct((b, h, s, MLA_QK_PAD), BF16),
            jax.ShapeDtypeStruct((b, h, s, MLA_QK_PAD), BF16),
            jax.ShapeDtypeStruct((b, h, s, MLA_V), BF16),
        ],
        compiler_params=_params(("parallel", "parallel")),
        name="mla_prep",
    )(proj, proj, tab, q_norm, kv_norm, wqn, wqr, wkn, wv)


def _diff_prep_kernel(dq_ref, tab_ref, o_ref):
    c, sa, sb = tab_ref[0], tab_ref[1], tab_ref[2]
    n = DIFF_HEADS * DIFF_QK
    for i in range(4):
        for j in range(n // LANES):
            x = dq_ref[:, n * i + LANES * j:n * i + LANES * (j + 1)].astype(F32)
            r = _rope_tile(x, c, sa, sb, DIFF_ROT // 2)
            if i < 2:
                r = r * (DIFF_QK ** -0.5 * LOG2E)
            o_ref[i, 0, :, LANES * j:LANES * (j + 1)] = r.astype(BF16)


def _diff_prep(proj, tab, b, s):
    tm = min(256, s)
    nt = s // tm
    n = DIFF_HEADS * DIFF_QK
    return pl.pallas_call(
        _diff_prep_kernel,
        grid=(b, nt),
        in_specs=[
            pl.BlockSpec((tm, 4 * n), lambda bi, i: (bi * nt + i, 0)),
            pl.BlockSpec((3, tm, LANES), lambda bi, i: (1, bi * nt + i, 0)),
        ],
        out_specs=pl.BlockSpec((4, 1, tm, n), lambda bi, i: (0, bi, i, 0)),
        out_shape=jax.ShapeDtypeStruct((4, b, s, n), BF16),
        compiler_params=_params(("parallel", "parallel")),
        name="diff_prep",
    )(proj, tab)


def _online_softmax_loop(qs, k_ats, v_at, nk):
    tq = qs[0].shape[0]

    def body(j, carry):
        v = v_at(j)
        out = []
        for q, k_at, (m, l, acc) in zip(qs, k_ats, carry):
            s = _dot_nt(q, k_at(j))
            m_new = jnp.maximum(m, jnp.max(s, axis=1, keepdims=True))
            alpha = jnp.exp2(m - m_new)
            p = jnp.exp2(s - m_new)
            l = alpha * l + jnp.sum(p, axis=1, keepdims=True)
            acc = alpha * acc + _dot(p.astype(BF16), v)
            out.append((m_new, l, acc))
        return tuple(out)

    one = (jnp.full((tq, 1), -jnp.inf, F32), jnp.zeros((tq, 1), F32), jnp.zeros((tq, v_at(0).shape[1]), F32))
    final = lax.fori_loop(0, nk, body, tuple(one for _ in qs), unroll=True)
    return [(l, acc) for _, l, acc in final]


def _mla_attn_kernel(q_ref, k_ref, v_ref, o_ref, *, tk):
    nk = k_ref.shape[2] // tk
    chunk = lambda j: pl.ds(pl.multiple_of(j * tk, tk), tk)
    (l, acc), = _online_softmax_loop(
        [q_ref[0, 0]], [lambda j: k_ref[0, 0, chunk(j), :]], lambda j: v_ref[0, 0, chunk(j), :], nk)
    o_ref[0] = (acc / l).astype(o_ref.dtype)


def _mla_attn(q, k, v):
    b, h, s, _ = q.shape
    tq = min(ATTN_TQ, s)
    return pl.pallas_call(
        functools.partial(_mla_attn_kernel, tk=min(ATTN_TK, s)),
        grid=(b, h, s // tq),
        in_specs=[
            pl.BlockSpec((1, 1, tq, MLA_QK_PAD), lambda bi, hi, i: (bi, hi, i, 0)),
            pl.BlockSpec((1, 1, s, MLA_QK_PAD), lambda bi, hi, i: (bi, hi, 0, 0)),
            pl.BlockSpec((1, 1, s, MLA_V), lambda bi, hi, i: (bi, hi, 0, 0)),
        ],
        out_specs=pl.BlockSpec((1, tq, MLA_V), lambda bi, hi, i: (bi, i, hi)),
        out_shape=jax.ShapeDtypeStruct((b, s, h * MLA_V), BF16),
        compiler_params=_params(("parallel", "parallel", "arbitrary")),
        name="mla_attn",
    )(q, k, v)


def _diff_attn_kernel(q1_ref, q2_ref, k1_ref, k2_ref, v_ref, lam_ref, g_ref, o_ref, *, tk, lam_init):
    hi = pl.program_id(1)
    tq = q1_ref.shape[2]
    nk = v_ref.shape[0] // tk
    chunk = lambda j: pl.ds(pl.multiple_of(j * tk, tk), tk)
    mine = (lax.broadcasted_iota(I32, (tq, LANES), 1) // DIFF_QK) == (hi % 2)
    zero = jnp.zeros((), BF16)
    qs = [jnp.where(mine, q1_ref[0, 0], zero), jnp.where(mine, q2_ref[0, 0], zero)]
    k_ats = [lambda j: k1_ref[0, 0, chunk(j), :], lambda j: k2_ref[0, 0, chunk(j), :]]
    (l1, a1), (l2, a2) = _online_softmax_loop(qs, k_ats, lambda j: v_ref[chunk(j), :], nk)
    lp = lam_ref[...]
    lam = (jnp.exp(jnp.sum(lp[0:1] * lp[1:2], axis=1, keepdims=True))
           - jnp.exp(jnp.sum(lp[2:3] * lp[3:4], axis=1, keepdims=True)) + lam_init)
    o = a1 / l1 - lam * (a2 / l2)
    o_ref[0] = (_rms(o, g_ref[...]) * (1.0 - lam_init)).astype(o_ref.dtype)


def _diff_attn(dqk, proj, col_v, lam_p, sub_norm, lam_init, b, s):
    h = DIFF_HEADS
    tq = min(ATTN_TQ, s)
    qspec = lambda which: pl.BlockSpec((1, 1, tq, LANES), lambda bi, hi, i: (which, bi, i, hi // 2))
    kspec = lambda which: pl.BlockSpec((1, 1, s, LANES), lambda bi, hi, i: (which, bi, 0, hi // 2))
    const = lambda bi, hi, i: (0, 0)
    return pl.pallas_call(
        functools.partial(_diff_attn_kernel, tk=min(ATTN_TK, s), lam_init=lam_init),
        grid=(b, h, s // tq),
        in_specs=[
            qspec(0), qspec(1), kspec(2), kspec(3),
            pl.BlockSpec((s, DIFF_V), lambda bi, hi, i: (bi, col_v + hi)),
            pl.BlockSpec((4, DIFF_QK), const),
            pl.BlockSpec((1, DIFF_V), const),
        ],
        out_specs=pl.BlockSpec((1, tq, DIFF_V), lambda bi, hi, i: (bi, i, hi)),
        out_shape=jax.ShapeDtypeStruct((b, s, h * DIFF_V), BF16),
        compiler_params=_params(("parallel", "parallel", "arbitrary")),
        name="diff_attn",
    )(dqk, dqk, dqk, dqk, proj, lam_p, sub_norm)


def _swa_prep_kernel(x_ref, tab_ref, o_ref):
    c, sa, sb = tab_ref[0], tab_ref[1], tab_ref[2]
    nq = SWA_Q_HEADS * SWA_HD // LANES
    for j in range(x_ref.shape[1] // LANES):
        x = x_ref[:, LANES * j:LANES * (j + 1)].astype(F32)
        r = _rope_tile(x, c, sa, sb, DIFF_ROT // 2)
        if j < nq:
            r = r * (SWA_HD ** -0.5 * LOG2E)
        o_ref[:, LANES * j:LANES * (j + 1)] = r.astype(BF16)


def _swa_prep(proj, tab, n_cols):
    t = proj.shape[0]
    tm = min(256, t)
    return pl.pallas_call(
        _swa_prep_kernel,
        grid=(t // tm,),
        in_specs=[
            pl.BlockSpec((tm, n_cols), lambda i: (i, 0)),
            pl.BlockSpec((3, tm, LANES), lambda i: (1, i, 0)),
        ],
        out_specs=pl.BlockSpec((tm, n_cols), lambda i: (i, 0)),
        out_shape=jax.ShapeDtypeStruct((t, n_cols), BF16),
        compiler_params=_params(("parallel",)),
        name="swa_prep",
    )(proj, tab)


def _swa_attn_kernel(sink_ref, bias_ref, q_ref, kp_ref, km_ref, kn_ref, vp_ref, vm_ref, vn_ref, o_ref):
    g = pl.program_id(1)
    tq = q_ref.shape[0]
    k = jnp.concatenate([kp_ref[...], km_ref[...], kn_ref[...]], axis=0)
    v = jnp.concatenate([vp_ref[...], vm_ref[...], vn_ref[...]], axis=0)
    bias = bias_ref[0]
    lane = lax.broadcasted_iota(I32, (tq, LANES), 1)
    zero = jnp.zeros((), BF16)
    qs = []
    for h in range(SWA_GROUP):
        qp = q_ref[:, LANES * (h // 2):LANES * (h // 2 + 1)]
        qs.append(jnp.where((lane // SWA_HD) == h % 2, qp, zero))
    s_all = _dot_nt(jnp.concatenate(qs, axis=0), k)
    ps, denoms = [], []
    for h in range(SWA_GROUP):
        sink = sink_ref[g * SWA_GROUP + h] * LOG2E
        sc = s_all[h * tq:(h + 1) * tq] + bias
        m = jnp.maximum(jnp.max(sc, axis=1, keepdims=True), sink)
        p = jnp.exp2(sc - m)
        denoms.append(jnp.sum(p, axis=1, keepdims=True) + jnp.exp2(sink - m))
        ps.append(p.astype(BF16))
    o_all = _dot(jnp.concatenate(ps, axis=0), v)
    for pair in range(SWA_GROUP // 2):
        lo = o_all[2 * pair * tq:(2 * pair + 1) * tq] / denoms[2 * pair]
        hi = o_all[(2 * pair + 1) * tq:(2 * pair + 2) * tq] / denoms[2 * pair + 1]
        o_ref[:, LANES * pair:LANES * (pair + 1)] = jnp.where(lane < SWA_HD, lo, hi).astype(o_ref.dtype)


def _swa_bias(tq):
    qi = np.arange(tq)[:, None]
    kj = np.arange(tq + 2 * WINDOW)[None, :] - WINDOW
    band = np.abs(kj - qi) <= WINDOW
    masks = [band, band & (kj >= 0), band & (kj < tq), band & (kj >= 0) & (kj < tq)]
    return jnp.asarray(np.where(np.stack(masks), 0.0, -np.inf), F32)


def _swa_attn(qk, proj, sink, b, s, col_k, col_v, tq=256):
    tq = min(tq, s)
    nt = s // tq
    r = tq // WINDOW
    nwb = s // WINDOW
    gw = SWA_GROUP * SWA_HD
    row = lambda bi, i: bi * nt + i
    prev = lambda bi, i: bi * nwb + jnp.maximum(i * r - 1, 0)
    nxt = lambda bi, i: bi * nwb + jnp.minimum((i + 1) * r, nwb - 1)
    edge = lambda i: jnp.where(i == 0, 1, 0) + jnp.where(i == nt - 1, 2, 0)
    grid_spec = pltpu.PrefetchScalarGridSpec(
        num_scalar_prefetch=1,
        grid=(b, SWA_KV_HEADS, nt),
        in_specs=[
            pl.BlockSpec((1, tq, tq + 2 * WINDOW), lambda bi, g, i, sk: (edge(i), 0, 0)),
            pl.BlockSpec((tq, gw), lambda bi, g, i, sk: (row(bi, i), g)),
            pl.BlockSpec((WINDOW, LANES), lambda bi, g, i, sk: (prev(bi, i), col_k + g)),
            pl.BlockSpec((tq, LANES), lambda bi, g, i, sk: (row(bi, i), col_k + g)),
            pl.BlockSpec((WINDOW, LANES), lambda bi, g, i, sk: (nxt(bi, i), col_k + g)),
            pl.BlockSpec((WINDOW, LANES), lambda bi, g, i, sk: (prev(bi, i), col_v + g)),
            pl.BlockSpec((tq, LANES), lambda bi, g, i, sk: (row(bi, i), col_v + g)),
            pl.BlockSpec((WINDOW, LANES), lambda bi, g, i, sk: (nxt(bi, i), col_v + g)),
        ],
        out_specs=pl.BlockSpec((tq, gw), lambda bi, g, i, sk: (row(bi, i), g)),
    )
    return pl.pallas_call(
        _swa_attn_kernel,
        grid_spec=grid_spec,
        out_shape=jax.ShapeDtypeStruct((b * s, SWA_Q_HEADS * SWA_HD), BF16),
        compiler_params=_params(("parallel", "parallel", "parallel")),
        name="swa_attn",
    )(sink, _swa_bias(tq), qk, qk, qk, qk, proj, proj, proj)


def _layer_norm(y, g, b):
    mu = jnp.mean(y, -1, keepdims=True)
    d = y - mu
    var = jnp.mean(d * d, -1, keepdims=True)
    return d * lax.rsqrt(var + LN_EPS) * g + b


PACK_SUBLANES = D_MODEL // 2 // LANES
HI_MASK = -65536


def _pack_words(lo, hi):
    lb = lax.bitcast_convert_type(lo.astype(BF16).astype(F32), I32)
    hb = lax.bitcast_convert_type(hi.astype(BF16).astype(F32), I32)
    return (hb & HI_MASK) | lax.shift_right_logical(lb, 16)


def _unpack_words(w):
    return lax.bitcast_convert_type(w << 16, F32), lax.bitcast_convert_type(w & HI_MASK, F32)


def _word_rows(j, tm, lead=(), base=0):
    return lead + (pl.ds(base * PACK_SUBLANES + j, tm, stride=PACK_SUBLANES), slice(None))


def _store_packed(ref, y, keep=None, clear=None):
    tm, d = y.shape
    for j in range(PACK_SUBLANES):
        w = _pack_words(y[:, LANES * j:LANES * (j + 1)], y[:, d // 2 + LANES * j:d // 2 + LANES * (j + 1)])
        if keep is not None:
            w = jnp.where(keep, w, jnp.where(clear, 0, ref[_word_rows(j, tm)]))
        ref[_word_rows(j, tm)] = w


def _load_packed_bf16(ref, tm, base=0):
    los, his = [], []
    for j in range(PACK_SUBLANES):
        lo, hi = _unpack_words(ref[_word_rows(j, tm, base=base)])
        los.append(lo.astype(BF16))
        his.append(hi.astype(BF16))
    return jnp.concatenate(los + his, axis=1)


def _out_ln_kernel(*refs, n_in):
    o_refs = refs[:n_in]
    w_refs = refs[n_in:2 * n_in]
    h_ref, g_ref, b_ref, hf_ref, hp_ref = refs[2 * n_in:]
    mix = _dot(o_refs[0][...], w_refs[0][...])
    for o_ref, w_ref in zip(o_refs[1:], w_refs[1:]):
        mix = mix + _dot(o_ref[...], w_ref[...])
    out = _layer_norm(DEEPNORM_ALPHA * h_ref[...] + mix, g_ref[...], b_ref[...])
    hf_ref[...] = out
    _store_packed(hp_ref, out)


def _out_ln(os_, ws, h, g, b):
    t, d = h.shape
    tm = min(256, t)
    n_in = len(os_)
    const = lambda i: (0, 0)
    rowblk = lambda a: pl.BlockSpec((tm, a.shape[1]), lambda i: (i, 0))
    return pl.pallas_call(
        functools.partial(_out_ln_kernel, n_in=n_in),
        grid=(t // tm,),
        in_specs=[rowblk(o) for o in os_] + [pl.BlockSpec(w.shape, const) for w in ws]
        + [rowblk(h), pl.BlockSpec((1, d), const), pl.BlockSpec((1, d), const)],
        out_specs=[pl.BlockSpec((tm, d), lambda i: (i, 0)),
                   pl.BlockSpec((tm * PACK_SUBLANES, LANES), lambda i: (i, 0))],
        out_shape=[jax.ShapeDtypeStruct((t, d), F32), jax.ShapeDtypeStruct((t * PACK_SUBLANES, LANES), I32)],
        compiler_params=_params(("parallel",)),
        name="out_proj_ln",
    )(*os_, *ws, h, g, b)


def _router_kernel(h_ref, wh_ref, wl_ref, bias_ref, idx_ref, gate_ref, rank_ref, cnt_ref, cnt_scr):
    i = pl.program_id(0)
    tm = h_ref.shape[0]

    @pl.when(i == 0)
    def _():
        cnt_scr[...] = jnp.zeros(cnt_scr.shape, F32)

    h = h_ref[...]
    hh = h.astype(BF16)
    hl = (h - hh.astype(F32)).astype(BF16)
    logits = _dot(hh, wh_ref[...]) + (_dot(hh, wl_ref[...]) + _dot(hl, wh_ref[...]))
    scores = jax.nn.sigmoid(logits)
    sel = scores + bias_ref[...]
    e_iota = lax.broadcasted_iota(I32, (tm, N_EXPERTS), 1).astype(F32)
    out_iota = lax.broadcasted_iota(I32, (tm, LANES), 1)
    onehots, gates = [], []
    idx_out = jnp.zeros((tm, LANES), F32)
    for k in range(TOP_K):
        m = jnp.max(sel, axis=1, keepdims=True)
        pick = jnp.min(jnp.where(sel == m, e_iota, float(N_EXPERTS)), axis=1, keepdims=True)
        oh = e_iota == pick
        onehots.append(oh)
        gates.append(jnp.sum(jnp.where(oh, scores, 0.0), axis=1, keepdims=True))
        idx_out = jnp.where(out_iota == k, pick, idx_out)
        sel = jnp.where(oh, -jnp.inf, sel)
    maskf = onehots[0].astype(F32)
    for oh in onehots[1:]:
        maskf = maskf + oh.astype(F32)
    tri = (lax.broadcasted_iota(I32, (tm, tm), 1) < lax.broadcasted_iota(I32, (tm, tm), 0)).astype(BF16)
    ranks = cnt_scr[...] + _dot(tri, maskf.astype(BF16))
    gsum = gates[0]
    for gk in gates[1:]:
        gsum = gsum + gk
    gate_out = jnp.zeros((tm, LANES), F32)
    rank_out = jnp.zeros((tm, LANES), F32)
    for k in range(TOP_K):
        gate_out = jnp.where(out_iota == k, gates[k] / gsum * ROUTED_SCALE, gate_out)
        rk = jnp.sum(jnp.where(onehots[k], ranks, 0.0), axis=1, keepdims=True)
        rank_out = jnp.where(out_iota == k, rk, rank_out)
    idx_ref[...] = idx_out.astype(I32)
    gate_ref[...] = gate_out
    rank_ref[...] = rank_out.astype(I32)
    total = cnt_scr[...] + jnp.sum(maskf, axis=0, keepdims=True)
    cnt_scr[...] = total
    cnt_ref[...] = total


def _router(h, w_hi, w_lo, bias):
    t, d = h.shape
    tm = min(256, t)
    const = lambda i: (0, 0)
    out = lambda: pl.BlockSpec((tm, LANES), lambda i: (i, 0))
    return pl.pallas_call(
        _router_kernel,
        grid=(t // tm,),
        in_specs=[
            pl.BlockSpec((tm, d), lambda i: (i, 0)),
            pl.BlockSpec((d, N_EXPERTS), const),
            pl.BlockSpec((d, N_EXPERTS), const),
            pl.BlockSpec((1, N_EXPERTS), const),
        ],
        out_specs=[out(), out(), out(), pl.BlockSpec((1, N_EXPERTS), const)],
        out_shape=[
            jax.ShapeDtypeStruct((t, LANES), I32),
            jax.ShapeDtypeStruct((t, LANES), F32),
            jax.ShapeDtypeStruct((t, LANES), I32),
            jax.ShapeDtypeStruct((1, N_EXPERTS), F32),
        ],
        scratch_shapes=[pltpu.VMEM((1, N_EXPERTS), F32)],
        compiler_params=_params(("arbitrary",)),
        name="router",
    )(h, w_hi, w_lo, bias)


def _token_tile(ref, r, lead=()):
    return ref.at[lead + (pl.ds(pl.multiple_of(r * PACK_SUBLANES, PACK_SUBLANES), PACK_SUBLANES),)]


def _expert_kernel(tile_ref, exp_ref, lo_ref, hi_ref, first_ref, tok_cur_ref, tok_next_ref, hp_ref,
                   wg_ref, wu_ref, wd_ref, y_ref, wg_bf, wu_bf, wd_bf, xb0, xb1, sem0, sem1):
    w = pl.program_id(0)
    tm = xb0.shape[0] // PACK_SUBLANES

    @pl.when((w == 0) | (exp_ref[w] != exp_ref[jnp.maximum(w - 1, 0)]))
    def _():
        wg_bf[...] = wg_ref[0, 0].astype(BF16)
        wu_bf[...] = wu_ref[0, 0].astype(BF16)
        wd_bf[...] = wd_ref[0, 0].astype(BF16)

    def wait(sem):
        pltpu.make_async_copy(hp_ref.at[pl.ds(0, tm)], hp_ref.at[pl.ds(0, tm)], sem).wait()

    @pl.when(w == 0)
    def _():
        def body(r, carry):
            pltpu.make_async_copy(hp_ref.at[tok_cur_ref[r]], _token_tile(xb0, r), sem0).start()
            return carry

        lax.fori_loop(0, tm, body, 0)

    def step(cur, sem_cur, nxt, sem_nxt):
        wait(sem_cur)
        for r in range(tm):
            pltpu.make_async_copy(hp_ref.at[tok_next_ref[r]], _token_tile(nxt, r), sem_nxt).start(priority=r % 2)
        x = _load_packed_bf16(cur, tm)
        gate = _dot(x, wg_bf[...])
        up = _dot(x, wu_bf[...])
        hmid = (gate * jax.nn.sigmoid(gate) * up).astype(BF16)
        y = _dot(hmid, wd_bf[...])
        rows = tile_ref[w] * tm + lax.broadcasted_iota(I32, (tm, 1), 0)
        mine = (rows >= lo_ref[w]) & (rows < hi_ref[w])
        _store_packed(y_ref, y, keep=mine, clear=first_ref[w] == 1)

        @pl.when(w == pl.num_programs(0) - 1)
        def _():
            wait(sem_nxt)

    @pl.when(w % 2 == 0)
    def _():
        step(xb0, sem0, xb1, sem1)

    @pl.when(w % 2 == 1)
    def _():
        step(xb1, sem1, xb0, sem0)


def _experts(hp, tok_of_slot, wg, wu, wd, layer, tile_w, exp_w, lo_w, hi_w, first_w, tm):
    d = D_MODEL
    a = tok_of_slot.shape[0]
    nw = tile_w.shape[0]
    rows = pl.BlockSpec((tm * PACK_SUBLANES, LANES), lambda w, tl, ex, lo, hi, fi: (tl[w], 0))
    xbuf = pltpu.VMEM((tm * PACK_SUBLANES, LANES), I32)
    grid_spec = pltpu.PrefetchScalarGridSpec(
        num_scalar_prefetch=5,
        grid=(nw,),
        in_specs=[
            pl.BlockSpec((tm,), lambda w, tl, ex, lo, hi, fi: (tl[w],), memory_space=pltpu.SMEM),
            pl.BlockSpec((tm,), lambda w, tl, ex, lo, hi, fi: (tl[jnp.minimum(w + 1, nw - 1)],),
                         memory_space=pltpu.SMEM),
            pl.BlockSpec(memory_space=pl.ANY),
            pl.BlockSpec((1, 1, d, D_EXPERT), lambda w, tl, ex, lo, hi, fi: (layer, ex[w], 0, 0)),
            pl.BlockSpec((1, 1, d, D_EXPERT), lambda w, tl, ex, lo, hi, fi: (layer, ex[w], 0, 0)),
            pl.BlockSpec((1, 1, D_EXPERT, d), lambda w, tl, ex, lo, hi, fi: (layer, ex[w], 0, 0)),
        ],
        out_specs=rows,
        scratch_shapes=[pltpu.VMEM((d, D_EXPERT), BF16), pltpu.VMEM((d, D_EXPERT), BF16),
                        pltpu.VMEM((D_EXPERT, d), BF16), xbuf, xbuf,
                        pltpu.SemaphoreType.DMA(()), pltpu.SemaphoreType.DMA(())],
    )
    return pl.pallas_call(
        _expert_kernel,
        grid_spec=grid_spec,
        out_shape=jax.ShapeDtypeStruct((a * PACK_SUBLANES, LANES), I32),
        compiler_params=_params(("arbitrary",)),
        name="moe_experts",
    )(tile_w, exp_w, lo_w, hi_w, first_w, tok_of_slot, tok_of_slot, hp, wg, wu, wd)


COMBINE_HALF = 128


def _combine_kernel(dcur_ref, dnext_ref, ys_ref, gate_ref, h_ref, hp_ref, sg_ref, su_ref, sd_ref, g_ref, b_ref,
                    hf_out, hb_out, buf0, buf1, sem0, sem1):
    i = pl.program_id(0)
    th = buf0.shape[1] // PACK_SUBLANES

    def gather(dref, base, buf, sem):
        for r in range(th):
            for k in range(TOP_K):
                src = ys_ref.at[dref[(base + r) * TOP_K + k]]
                pltpu.make_async_copy(src, _token_tile(buf, r, (k,)), sem).start(priority=k % 2)

    def wait(sem):
        for k in range(TOP_K):
            pltpu.make_async_copy(ys_ref.at[pl.ds(0, th)], ys_ref.at[pl.ds(0, th)], sem).wait()

    def reduce_half(half, buf):
        rows = pl.ds(half * th, th)
        x = _load_packed_bf16(hp_ref, th, base=half * th)
        gate = _dot(x, sg_ref[...])
        up = _dot(x, su_ref[...])
        acc = _dot((gate * jax.nn.sigmoid(gate) * up).astype(BF16), sd_ref[...])
        acc = acc + DEEPNORM_ALPHA * h_ref[rows, :]
        gates = gate_ref[rows, :]
        los = [None] * PACK_SUBLANES
        his = [None] * PACK_SUBLANES
        for k in range(TOP_K):
            gk = gates[:, k:k + 1]
            for j in range(PACK_SUBLANES):
                lo, hi = _unpack_words(buf[_word_rows(j, th, (k,))])
                los[j] = gk * lo if k == 0 else los[j] + gk * lo
                his[j] = gk * hi if k == 0 else his[j] + gk * hi
        out = _layer_norm(acc + jnp.concatenate(los + his, axis=1), g_ref[...], b_ref[...])
        hf_out[rows, :] = out
        hb_out[rows, :] = out.astype(BF16)

    @pl.when(i == 0)
    def _():
        def body(r, carry):
            for k in range(TOP_K):
                src = ys_ref.at[dcur_ref[r * TOP_K + k]]
                pltpu.make_async_copy(src, _token_tile(buf0, r, (k,)), sem0).start(priority=k % 2)
            return carry

        lax.fori_loop(0, th, body, 0)

    wait(sem0)
    gather(dcur_ref, th, buf1, sem1)
    reduce_half(0, buf0)
    wait(sem1)
    gather(dnext_ref, 0, buf0, sem0)
    reduce_half(1, buf1)

    @pl.when(i == pl.num_programs(0) - 1)
    def _():
        wait(sem0)


def _combine(dest_flat, ys, gate, h, hp, sg, su, sd, g, b):
    t, d = h.shape
    th = min(COMBINE_HALF, t // 2)
    tm = 2 * th
    n = t // tm
    const = lambda i: (0, 0)
    row = lambda width: pl.BlockSpec((tm, width), lambda i: (i, 0))
    buf = pltpu.VMEM((TOP_K, th * PACK_SUBLANES, LANES), I32)
    return pl.pallas_call(
        _combine_kernel,
        grid=(n,),
        in_specs=[
            pl.BlockSpec((tm * TOP_K,), lambda i: (i,), memory_space=pltpu.SMEM),
            pl.BlockSpec((th * TOP_K,), lambda i: (jnp.minimum(2 * i + 2, 2 * n - 1),), memory_space=pltpu.SMEM),
            pl.BlockSpec(memory_space=pl.ANY),
            row(LANES), row(d), pl.BlockSpec((tm * PACK_SUBLANES, LANES), lambda i: (i, 0)),
            pl.BlockSpec(sg.shape, const), pl.BlockSpec(su.shape, const), pl.BlockSpec(sd.shape, const),
            pl.BlockSpec((1, d), const), pl.BlockSpec((1, d), const),
        ],
        out_specs=[row(d), row(d)],
        out_shape=[jax.ShapeDtypeStruct((t, d), F32), jax.ShapeDtypeStruct((t, d), BF16)],
        scratch_shapes=[buf, buf, pltpu.SemaphoreType.DMA(()), pltpu.SemaphoreType.DMA(())],
        compiler_params=_params(("arbitrary",)),
        name="moe_combine",
    )(dest_flat, dest_flat, ys, gate, h, hp, sg, su, sd, g, b)


EXPERT_TILE = 512


def _group_metadata(counts, idx, rank, n_assign):
    tm = min(EXPERT_TILE, n_assign)
    nt = n_assign // tm
    nw = nt + N_EXPERTS - 1
    sizes = counts.reshape(N_EXPERTS).astype(I32)
    ends = jnp.cumsum(sizes)
    starts = ends - sizes
    my_start = jnp.sum(jnp.where(idx[..., None] == jnp.arange(N_EXPERTS, dtype=I32), starts, 0), axis=-1)
    dest = (my_start + rank).reshape(-1).astype(I32)
    first_tile = starts // tm
    n_e = jnp.where(sizes > 0, (ends - 1) // tm - first_tile + 1, 0)
    cum = jnp.cumsum(n_e)
    off = cum - n_e
    total = cum[-1]
    w = jnp.arange(nw, dtype=I32)
    valid = w < total
    experts = jnp.arange(N_EXPERTS, dtype=I32)
    last_used = jnp.max(jnp.where(n_e > 0, experts, 0))
    e_w = jnp.where(valid, jnp.sum(cum[None, :] <= w[:, None], axis=1), last_used).astype(I32)
    pick = lambda table: jnp.sum(jnp.where(e_w[:, None] == experts, table, 0), axis=1)
    tile_w = jnp.where(valid, pick(first_tile) + (w - pick(off)), nt - 1).astype(I32)
    lo_w = jnp.where(valid, jnp.maximum(pick(starts), tile_w * tm), 0).astype(I32)
    hi_w = jnp.where(valid, jnp.minimum(pick(ends), (tile_w + 1) * tm), 0).astype(I32)
    first_w = jnp.concatenate([jnp.ones((1,), I32), (tile_w[1:] != tile_w[:-1]).astype(I32)])
    return dest, tile_w, e_w, lo_w, hi_w, first_w, tm


def _moe(h, hp, r_hi, r_lo, r_bias, wg, wu, wd, layer, sg, su, sd, g, b):
    t = h.shape[0]
    a = t * TOP_K
    idx, gate, rank, counts = _router(h, r_hi, r_lo, r_bias)
    dest, tile_w, e_w, lo_w, hi_w, first_w, tm = _group_metadata(counts, idx[:, :TOP_K], rank[:, :TOP_K], a)
    tok_of_slot = lax.shift_right_logical(jnp.argsort(dest).astype(I32), TOP_K.bit_length() - 1)
    ys = _experts(hp.reshape(t, PACK_SUBLANES, LANES), tok_of_slot, wg, wu, wd, layer,
                  tile_w, e_w, lo_w, hi_w, first_w, tm)
    return _combine(dest, ys.reshape(a, PACK_SUBLANES, LANES), gate, h, hp, sg, su, sd, g, b)


def _even_weights(w_in, w_uq, w_ukv, w_out):
    o1 = MLA_Q_RANK
    o2 = o1 + MLA_KV_RANK
    o3 = o2 + MLA_ROPE
    o4 = o3 + 4 * DIFF_HEADS * DIFF_QK
    c_q, c_kv, k_r, dq, dv = w_in[:, :o1], w_in[:, o1:o2], w_in[:, o2:o3], w_in[:, o3:o4], w_in[:, o4:]
    dq = dq.reshape(D_MODEL, DIFF_HEADS, 4, DIFF_QK).transpose(0, 2, 1, 3).reshape(D_MODEL, -1)
    pad = jnp.zeros((D_MODEL, 512 - MLA_KV_RANK - MLA_ROPE), w_in.dtype)
    w_proj = jnp.concatenate([dq, dv, c_q, c_kv, k_r, pad], axis=1).astype(BF16)
    uq = w_uq.reshape(MLA_Q_RANK, MLA_HEADS, MLA_NOPE + MLA_ROPE)
    wqn = uq[:, :, :MLA_NOPE].reshape(MLA_Q_RANK, -1).astype(BF16)
    wqr = uq[:, :, MLA_NOPE:].reshape(MLA_Q_RANK, -1).astype(BF16)
    ukv = w_ukv.reshape(MLA_KV_RANK, MLA_HEADS, MLA_NOPE + MLA_V)
    wkn = ukv[:, :, :MLA_NOPE].reshape(MLA_KV_RANK, -1).astype(BF16)
    wv = ukv[:, :, MLA_NOPE:].reshape(MLA_KV_RANK, -1).astype(BF16)
    n_mla = MLA_HEADS * MLA_V
    return w_proj, wqn, wqr, wkn, wv, w_out[:n_mla].astype(BF16), w_out[n_mla:].astype(BF16)


def _odd_weights(w_in, w_out):
    nq = SWA_Q_HEADS * SWA_HD
    nkv = SWA_KV_HEADS * SWA_HD
    dup = lambda w: jnp.concatenate([w.reshape(D_MODEL, SWA_KV_HEADS, 1, SWA_HD)] * 2, axis=2).reshape(D_MODEL, -1)
    w_proj = jnp.concatenate([w_in[:, :nq], dup(w_in[:, nq:nq + nkv]), dup(w_in[:, nq + nkv:])], axis=1)
    return w_proj.astype(BF16), w_out.astype(BF16)


def kernel(x, positions, ab_w_in, mla_q_norm, mla_w_uq, mla_kv_norm, mla_w_ukv, diff_lambda_q1, diff_lambda_k1, diff_lambda_q2, diff_lambda_k2, diff_sub_norm, ab_w_out, swa_w_in, swa_sink, swa_w_out, mix_ln_g, mix_ln_b, ffn_ln_g, ffn_ln_b, router_w, router_bias, exp_w_gate, exp_w_up, exp_w_down, shared_w_gate, shared_w_up, shared_w_down):
    b, s, d = x.shape
    t = b * s
    tab = _rope_tables(positions)
    h = x.reshape(t, d)
    hb = h.astype(BF16)
    row = lambda v: v.reshape(1, -1)
    for layer in range(DEPTH):
        i = layer // 2
        if layer % 2 == 0:
            w_proj, wqn, wqr, wkn, wv, wo_mla, wo_diff = _even_weights(
                ab_w_in[i], mla_w_uq[i], mla_w_ukv[i], ab_w_out[i])
            proj = _matmul(hb, w_proj, "even_in_proj")
            q, k, v = _mla_prep(proj, tab, row(mla_q_norm[i]), row(mla_kv_norm[i]), wqn, wqr, wkn, wv,
                                b, s, col_cq=6, col_ckv=7)
            o_mla = _mla_attn(q, k, v).reshape(t, -1)
            dqk = _diff_prep(proj, tab, b, s)
            lam_p = jnp.stack([diff_lambda_q1[i], diff_lambda_k1[i], diff_lambda_q2[i], diff_lambda_k2[i]])
            lam_init = 0.8 - 0.6 * math.exp(-0.3 * layer)
            o_diff = _diff_attn(dqk, proj, 2048 // DIFF_V, lam_p, row(diff_sub_norm[i]), lam_init,
                                b, s).reshape(t, -1)
            h, hp = _out_ln([o_mla, o_diff], [wo_mla, wo_diff], h, row(mix_ln_g[layer]), row(mix_ln_b[layer]))
        else:
            w_proj, wo = _odd_weights(swa_w_in[i], swa_w_out[i])
            proj = _matmul(hb, w_proj, "odd_in_proj")
            nq = SWA_Q_HEADS * SWA_HD
            n_rot = nq + SWA_KV_HEADS * LANES
            qk = _swa_prep(proj, tab, n_rot)
            o = _swa_attn(qk, proj, swa_sink[i], b, s, col_k=nq // LANES, col_v=n_rot // LANES)
            h, hp = _out_ln([o], [wo], h, row(mix_ln_g[layer]), row(mix_ln_b[layer]))
        rw = router_w[layer]
        r_hi = rw.astype(BF16)
        r_lo = (rw - r_hi.astype(F32)).astype(BF16)
        h, hb = _moe(h, hp, r_hi, r_lo, row(router_bias[layer]),
                     exp_w_gate, exp_w_up, exp_w_down, layer,
                     shared_w_gate[layer].astype(BF16), shared_w_up[layer].astype(BF16),
                     shared_w_down[layer].astype(BF16), row(ffn_ln_g[layer]), row(ffn_ln_b[layer]))
    return h.reshape(b, s, d)
```

```python
import functools
import math

import numpy as np
import jax
import jax.numpy as jnp
from jax import lax
from jax.experimental import pallas as pl
from jax.experimental.pallas import tpu as pltpu

F32 = jnp.float32
BF16 = jnp.bfloat16
I32 = jnp.int32

D_MODEL = 2048
DEPTH = 4
ROPE_THETA = 500000.0
LN_EPS = 1e-5
RMS_EPS = 1e-6
MLA_HEADS = 8
MLA_NOPE = 128
MLA_ROPE = 64
MLA_V = 128
MLA_Q_RANK = 512
MLA_KV_RANK = 256
DIFF_HEADS = 8
DIFF_QK = 64
DIFF_V = 128
DIFF_ROT = DIFF_QK // 4
SWA_Q_HEADS = 32
SWA_KV_HEADS = 4
SWA_GROUP = SWA_Q_HEADS // SWA_KV_HEADS
SWA_HD = 64
WINDOW = 128
N_EXPERTS = 64
TOP_K = 8
D_EXPERT = 256
ROUTED_SCALE = 2.5
DEEPNORM_ALPHA = (2 * DEPTH) ** 0.25
LOG2E = math.log2(math.e)

LANES = 128
MLA_QK_PAD = 256
VMEM_LIMIT = 48 << 20
ATTN_TQ = 512
MLA_TQ = 1024
MLA_TK = 2048
ATTN_TK = 2048


def _params(sem):
    return pltpu.CompilerParams(dimension_semantics=sem, vmem_limit_bytes=VMEM_LIMIT)


def _dot(a, b):
    return jnp.dot(a, b, preferred_element_type=F32)


def _dot_nt(a, b):
    return lax.dot_general(a, b, (((1,), (1,)), ((), ())), preferred_element_type=F32)


def _tables_kernel(pos_ref, c_ref, out_ref):
    pos = pos_ref[...]
    a64 = pos * c_ref[0:1, :]
    a16 = pos * c_ref[3:4, :]
    s64 = jnp.sin(a64)
    s16 = jnp.sin(a16)
    out_ref[0] = jnp.cos(a64)
    out_ref[1] = s64 * c_ref[1:2, :]
    out_ref[2] = s64 * c_ref[2:3, :]
    out_ref[3] = jnp.cos(a16)
    out_ref[4] = s16 * c_ref[4:5, :]
    out_ref[5] = s16 * c_ref[5:6, :]


def _rope_consts():
    j = np.arange(LANES) % 64
    inv64 = ROPE_THETA ** (-jnp.arange(0, MLA_ROPE, 2, dtype=F32) / MLA_ROPE)
    inv16 = ROPE_THETA ** (-jnp.arange(0, DIFF_ROT, 2, dtype=F32) / DIFF_ROT)
    f64 = inv64[j % 32]
    f16 = jnp.where(j < 16, inv16[j % 8], 0.0)
    rows = [
        f64,
        jnp.asarray(np.where(j < 32, -1.0, 0.0), F32),
        jnp.asarray(np.where(j >= 32, 1.0, 0.0), F32),
        f16,
        jnp.asarray(np.where(j < 8, -1.0, 0.0), F32),
        jnp.asarray(np.where((j >= 8) & (j < 16), 1.0, 0.0), F32),
        jnp.zeros((LANES,), F32),
        jnp.zeros((LANES,), F32),
    ]
    return jnp.stack(rows).astype(F32)


def _rope_tables(positions):
    t = positions.size
    pos = jnp.broadcast_to(positions.reshape(t, 1).astype(F32), (t, LANES))
    tm = min(512, t)
    return pl.pallas_call(
        _tables_kernel,
        grid=(t // tm,),
        in_specs=[pl.BlockSpec((tm, LANES), lambda i: (i, 0)), pl.BlockSpec((8, LANES), lambda i: (0, 0))],
        out_specs=pl.BlockSpec((6, tm, LANES), lambda i: (0, i, 0)),
        out_shape=jax.ShapeDtypeStruct((6, t, LANES), F32),
        compiler_params=_params(("parallel",)),
        name="rope_tables",
    )(pos, _rope_consts())


def _rope_tile(x, c, sa, sb, half):
    return x * c + pltpu.roll(x, LANES - half, 1) * sa + pltpu.roll(x, half, 1) * sb


def _mm_kernel(x_ref, w_ref, o_ref):
    o_ref[...] = _dot(x_ref[...], w_ref[...]).astype(o_ref.dtype)


def _matmul(x, w, name, tm=512, tn=1024):
    m, k = x.shape
    n = w.shape[1]
    tm = min(tm, m)
    tn = min(tn, n)
    return pl.pallas_call(
        _mm_kernel,
        grid=(n // tn, m // tm),
        in_specs=[pl.BlockSpec((tm, k), lambda j, i: (i, 0)), pl.BlockSpec((k, tn), lambda j, i: (0, j))],
        out_specs=pl.BlockSpec((tm, tn), lambda j, i: (i, j)),
        out_shape=jax.ShapeDtypeStruct((m, n), BF16),
        compiler_params=_params(("parallel", "parallel")),
        name=name,
    )(x, w)


def _rms(x, g):
    return x * lax.rsqrt(jnp.mean(x * x, -1, keepdims=True) + RMS_EPS) * g


def _mla_prep_kernel(cq_ref, ckv_ref, tab_ref, qg_ref, kvg_ref, wqn_ref, wqr_ref, wkn_ref, wv_ref,
                     q_ref, k_ref, v_ref, *, scale):
    tm = cq_ref.shape[0]
    cq = cq_ref[...].astype(F32)
    ckv_all = ckv_ref[...].astype(F32)
    ckv = ckv_all[:, :MLA_KV_RANK]
    kr = ckv_all[:, MLA_KV_RANK:MLA_KV_RANK + LANES]
    cqn = _rms(cq, qg_ref[...]).astype(BF16)
    ckvn = _rms(ckv, kvg_ref[...]).astype(BF16)
    qn = _dot(cqn, wqn_ref[...]) * scale
    qr = _dot(cqn, wqr_ref[...]) * scale
    kn = _dot(ckvn, wkn_ref[...])
    vv = _dot(ckvn, wv_ref[...])
    c, sa, sb = tab_ref[0], tab_ref[1], tab_ref[2]
    lo = lax.broadcasted_iota(I32, (tm, LANES), 1) < MLA_ROPE
    half = MLA_ROPE // 2
    krr = jnp.where(lo, _rope_tile(kr, c, sa, sb, half), 0.0).astype(BF16)
    for j in range(MLA_HEADS // 2):
        r = _rope_tile(qr[:, LANES * j:LANES * (j + 1)], c, sa, sb, half)
        q_ref[0, 2 * j, :, LANES:] = jnp.where(lo, r, 0.0).astype(BF16)
        q_ref[0, 2 * j + 1, :, LANES:] = jnp.where(lo, pltpu.roll(r, MLA_ROPE, 1), 0.0).astype(BF16)
    for h in range(MLA_HEADS):
        sl = slice(LANES * h, LANES * (h + 1))
        q_ref[0, h, :, :LANES] = qn[:, sl].astype(BF16)
        k_ref[0, h, :, :LANES] = kn[:, sl].astype(BF16)
        k_ref[0, h, :, LANES:] = krr
        v_ref[0, h] = vv[:, sl].astype(BF16)


def _mla_prep(proj, tab, q_norm, kv_norm, wqn, wqr, wkn, wv, b, s, col_cq, col_ckv):
    tm = min(256, s)
    nt = s // tm
    h = MLA_HEADS
    row = lambda bi, i: bi * nt + i
    const = lambda bi, i: (0, 0)
    return pl.pallas_call(
        functools.partial(_mla_prep_kernel, scale=(MLA_NOPE + MLA_ROPE) ** -0.5 * LOG2E),
        grid=(b, nt),
        in_specs=[
            pl.BlockSpec((tm, MLA_Q_RANK), lambda bi, i: (row(bi, i), col_cq)),
            pl.BlockSpec((tm, 512), lambda bi, i: (row(bi, i), col_ckv)),
            pl.BlockSpec((3, tm, LANES), lambda bi, i: (0, row(bi, i), 0)),
            pl.BlockSpec((1, MLA_Q_RANK), const),
            pl.BlockSpec((1, MLA_KV_RANK), const),
            pl.BlockSpec(wqn.shape, const),
            pl.BlockSpec(wqr.shape, const),
            pl.BlockSpec(wkn.shape, const),
            pl.BlockSpec(wv.shape, const),
        ],
        out_specs=[
            pl.BlockSpec((1, h, tm, MLA_QK_PAD), lambda bi, i: (bi, 0, i, 0)),
            pl.BlockSpec((1, h, tm, MLA_QK_PAD), lambda bi, i: (bi, 0, i, 0)),
            pl.BlockSpec((1, h, tm, MLA_V), lambda bi, i: (bi, 0, i, 0)),
        ],
        out_shape=[
            jax.ShapeDtypeStruct((b, h, s, MLA_QK_PAD), BF16),
            jax.ShapeDtypeStruct((b, h, s, MLA_QK_PAD), BF16),
            jax.ShapeDtypeStruct((b, h, s, MLA_V), BF16),
        ],
        compiler_params=_params(("parallel", "parallel")),
        name="mla_prep",
    )(proj, proj, tab, q_norm, kv_norm, wqn, wqr, wkn, wv)


def _diff_prep_kernel(dq_ref, tab_ref, o_ref):
    c, sa, sb = tab_ref[0], tab_ref[1], tab_ref[2]
    n = DIFF_HEADS * DIFF_QK
    for i in range(4):
        for j in range(n // LANES):
            x = dq_ref[:, n * i + LANES * j:n * i + LANES * (j + 1)].astype(F32)
            r = _rope_tile(x, c, sa, sb, DIFF_ROT // 2)
            if i < 2:
                r = r * (DIFF_QK ** -0.5 * LOG2E)
            o_ref[i, 0, :, LANES * j:LANES * (j + 1)] = r.astype(BF16)


def _diff_prep(proj, tab, b, s):
    tm = min(256, s)
    nt = s // tm
    n = DIFF_HEADS * DIFF_QK
    return pl.pallas_call(
        _diff_prep_kernel,
        grid=(b, nt),
        in_specs=[
            pl.BlockSpec((tm, 4 * n), lambda bi, i: (bi * nt + i, 0)),
            pl.BlockSpec((3, tm, LANES), lambda bi, i: (1, bi * nt + i, 0)),
        ],
        out_specs=pl.BlockSpec((4, 1, tm, n), lambda bi, i: (0, bi, i, 0)),
        out_shape=jax.ShapeDtypeStruct((4, b, s, n), BF16),
        compiler_params=_params(("parallel", "parallel")),
        name="diff_prep",
    )(proj, tab)


def _online_softmax_loop(qs, k_ats, v_at, nk):
    tq = qs[0].shape[0]

    def body(j, carry):
        v = v_at(j)
        out = []
        for q, k_at, (m, l, acc) in zip(qs, k_ats, carry):
            s = _dot_nt(q, k_at(j))
            m_new = jnp.maximum(m, jnp.max(s, axis=1, keepdims=True))
            alpha = jnp.exp2(m - m_new)
            p = jnp.exp2(s - m_new)
            l = alpha * l + jnp.sum(p, axis=1, keepdims=True)
            acc = alpha * acc + _dot(p.astype(BF16), v)
            out.append((m_new, l, acc))
        return tuple(out)

    one = (jnp.full((tq, 1), -jnp.inf, F32), jnp.zeros((tq, 1), F32), jnp.zeros((tq, v_at(0).shape[1]), F32))
    final = lax.fori_loop(0, nk, body, tuple(one for _ in qs), unroll=True)
    return [(l, acc) for _, l, acc in final]


def _mla_attn_kernel(q_ref, k_ref, v_ref, o_ref, *, tk):
    nk = k_ref.shape[2] // tk
    chunk = lambda j: pl.ds(pl.multiple_of(j * tk, tk), tk)
    (l, acc), = _online_softmax_loop(
        [q_ref[0, 0]], [lambda j: k_ref[0, 0, chunk(j), :]], lambda j: v_ref[0, 0, chunk(j), :], nk)
    o_ref[0] = (acc / l).astype(o_ref.dtype)


def _mla_attn(q, k, v):
    b, h, s, _ = q.shape
    tq = min(MLA_TQ, s)
    return pl.pallas_call(
        functools.partial(_mla_attn_kernel, tk=min(MLA_TK, s)),
        grid=(b, h, s // tq),
        in_specs=[
            pl.BlockSpec((1, 1, tq, MLA_QK_PAD), lambda bi, hi, i: (bi, hi, i, 0)),
            pl.BlockSpec((1, 1, s, MLA_QK_PAD), lambda bi, hi, i: (bi, hi, 0, 0)),
            pl.BlockSpec((1, 1, s, MLA_V), lambda bi, hi, i: (bi, hi, 0, 0)),
        ],
        out_specs=pl.BlockSpec((1, tq, MLA_V), lambda bi, hi, i: (bi, i, hi)),
        out_shape=jax.ShapeDtypeStruct((b, s, h * MLA_V), BF16),
        compiler_params=_params(("parallel", "parallel", "arbitrary")),
        name="mla_attn",
    )(q, k, v)


def _diff_attn_kernel(q1_ref, q2_ref, k1_ref, k2_ref, v_ref, lam_ref, g_ref, o_ref, *, tk, lam_init):
    hi = pl.program_id(1)
    tq = q1_ref.shape[2]
    nk = v_ref.shape[0] // tk
    chunk = lambda j: pl.ds(pl.multiple_of(j * tk, tk), tk)
    mine = (lax.broadcasted_iota(I32, (tq, LANES), 1) // DIFF_QK) == (hi % 2)
    zero = jnp.zeros((), BF16)
    qs = [jnp.where(mine, q1_ref[0, 0], zero), jnp.where(mine, q2_ref[0, 0], zero)]
    k_ats = [lambda j: k1_ref[0, 0, chunk(j), :], lambda j: k2_ref[0, 0, chunk(j), :]]
    (l1, a1), (l2, a2) = _online_softmax_loop(qs, k_ats, lambda j: v_ref[chunk(j), :], nk)
    lp = lam_ref[...]
    lam = (jnp.exp(jnp.sum(lp[0:1] * lp[1:2], axis=1, keepdims=True))
           - jnp.exp(jnp.sum(lp[2:3] * lp[3:4], axis=1, keepdims=True)) + lam_init)
    o = a1 / l1 - lam * (a2 / l2)
    o_ref[0] = (_rms(o, g_ref[...]) * (1.0 - lam_init)).astype(o_ref.dtype)


def _diff_attn(dqk, proj, col_v, lam_p, sub_norm, lam_init, b, s):
    h = DIFF_HEADS
    tq = min(ATTN_TQ, s)
    qspec = lambda which: pl.BlockSpec((1, 1, tq, LANES), lambda bi, hi, i: (which, bi, i, hi // 2))
    kspec = lambda which: pl.BlockSpec((1, 1, s, LANES), lambda bi, hi, i: (which, bi, 0, hi // 2))
    const = lambda bi, hi, i: (0, 0)
    return pl.pallas_call(
        functools.partial(_diff_attn_kernel, tk=min(ATTN_TK, s), lam_init=lam_init),
        grid=(b, h, s // tq),
        in_specs=[
            qspec(0), qspec(1), kspec(2), kspec(3),
            pl.BlockSpec((s, DIFF_V), lambda bi, hi, i: (bi, col_v + hi)),
            pl.BlockSpec((4, DIFF_QK), const),
            pl.BlockSpec((1, DIFF_V), const),
        ],
        out_specs=pl.BlockSpec((1, tq, DIFF_V), lambda bi, hi, i: (bi, i, hi)),
        out_shape=jax.ShapeDtypeStruct((b, s, h * DIFF_V), BF16),
        compiler_params=_params(("parallel", "parallel", "arbitrary")),
        name="diff_attn",
    )(dqk, dqk, dqk, dqk, proj, lam_p, sub_norm)


def _swa_prep_kernel(x_ref, tab_ref, o_ref):
    c, sa, sb = tab_ref[0], tab_ref[1], tab_ref[2]
    nq = SWA_Q_HEADS * SWA_HD // LANES
    for j in range(x_ref.shape[1] // LANES):
        x = x_ref[:, LANES * j:LANES * (j + 1)].astype(F32)
        r = _rope_tile(x, c, sa, sb, DIFF_ROT // 2)
        if j < nq:
            r = r * (SWA_HD ** -0.5 * LOG2E)
        o_ref[:, LANES * j:LANES * (j + 1)] = r.astype(BF16)


def _swa_prep(proj, tab, n_cols):
    t = proj.shape[0]
    tm = min(256, t)
    return pl.pallas_call(
        _swa_prep_kernel,
        grid=(t // tm,),
        in_specs=[
            pl.BlockSpec((tm, n_cols), lambda i: (i, 0)),
            pl.BlockSpec((3, tm, LANES), lambda i: (1, i, 0)),
        ],
        out_specs=pl.BlockSpec((tm, n_cols), lambda i: (i, 0)),
        out_shape=jax.ShapeDtypeStruct((t, n_cols), BF16),
        compiler_params=_params(("parallel",)),
        name="swa_prep",
    )(proj, tab)


def _swa_attn_kernel(sink_ref, bias_ref, q_ref, kp_ref, km_ref, kn_ref, vp_ref, vm_ref, vn_ref, o_ref):
    g = pl.program_id(1)
    tq = q_ref.shape[0]
    k = jnp.concatenate([kp_ref[...], km_ref[...], kn_ref[...]], axis=0)
    v = jnp.concatenate([vp_ref[...], vm_ref[...], vn_ref[...]], axis=0)
    bias = bias_ref[0]
    lane = lax.broadcasted_iota(I32, (tq, LANES), 1)
    zero = jnp.zeros((), BF16)
    qs = []
    for h in range(SWA_GROUP):
        qp = q_ref[:, LANES * (h // 2):LANES * (h // 2 + 1)]
        qs.append(jnp.where((lane // SWA_HD) == h % 2, qp, zero))
    s_all = _dot_nt(jnp.concatenate(qs, axis=0), k)
    ps, denoms = [], []
    for h in range(SWA_GROUP):
        sink = sink_ref[g * SWA_GROUP + h] * LOG2E
        sc = s_all[h * tq:(h + 1) * tq] + bias
        m = jnp.maximum(jnp.max(sc, axis=1, keepdims=True), sink)
        p = jnp.exp2(sc - m)
        denoms.append(jnp.sum(p, axis=1, keepdims=True) + jnp.exp2(sink - m))
        ps.append(p.astype(BF16))
    o_all = _dot(jnp.concatenate(ps, axis=0), v)
    for pair in range(SWA_GROUP // 2):
        lo = o_all[2 * pair * tq:(2 * pair + 1) * tq] / denoms[2 * pair]
        hi = o_all[(2 * pair + 1) * tq:(2 * pair + 2) * tq] / denoms[2 * pair + 1]
        o_ref[:, LANES * pair:LANES * (pair + 1)] = jnp.where(lane < SWA_HD, lo, hi).astype(o_ref.dtype)


def _swa_bias(tq):
    qi = np.arange(tq)[:, None]
    kj = np.arange(tq + 2 * WINDOW)[None, :] - WINDOW
    band = np.abs(kj - qi) <= WINDOW
    masks = [band, band & (kj >= 0), band & (kj < tq), band & (kj >= 0) & (kj < tq)]
    return jnp.asarray(np.where(np.stack(masks), 0.0, -np.inf), F32)


def _swa_attn(qk, proj, sink, b, s, col_k, col_v, tq=256):
    tq = min(tq, s)
    nt = s // tq
    r = tq // WINDOW
    nwb = s // WINDOW
    gw = SWA_GROUP * SWA_HD
    row = lambda bi, i: bi * nt + i
    prev = lambda bi, i: bi * nwb + jnp.maximum(i * r - 1, 0)
    nxt = lambda bi, i: bi * nwb + jnp.minimum((i + 1) * r, nwb - 1)
    edge = lambda i: jnp.where(i == 0, 1, 0) + jnp.where(i == nt - 1, 2, 0)
    grid_spec = pltpu.PrefetchScalarGridSpec(
        num_scalar_prefetch=1,
        grid=(b, SWA_KV_HEADS, nt),
        in_specs=[
            pl.BlockSpec((1, tq, tq + 2 * WINDOW), lambda bi, g, i, sk: (edge(i), 0, 0)),
            pl.BlockSpec((tq, gw), lambda bi, g, i, sk: (row(bi, i), g)),
            pl.BlockSpec((WINDOW, LANES), lambda bi, g, i, sk: (prev(bi, i), col_k + g)),
            pl.BlockSpec((tq, LANES), lambda bi, g, i, sk: (row(bi, i), col_k + g)),
            pl.BlockSpec((WINDOW, LANES), lambda bi, g, i, sk: (nxt(bi, i), col_k + g)),
            pl.BlockSpec((WINDOW, LANES), lambda bi, g, i, sk: (prev(bi, i), col_v + g)),
            pl.BlockSpec((tq, LANES), lambda bi, g, i, sk: (row(bi, i), col_v + g)),
            pl.BlockSpec((WINDOW, LANES), lambda bi, g, i, sk: (nxt(bi, i), col_v + g)),
        ],
        out_specs=pl.BlockSpec((tq, gw), lambda bi, g, i, sk: (row(bi, i), g)),
    )
    return pl.pallas_call(
        _swa_attn_kernel,
        grid_spec=grid_spec,
        out_shape=jax.ShapeDtypeStruct((b * s, SWA_Q_HEADS * SWA_HD), BF16),
        compiler_params=_params(("parallel", "parallel", "parallel")),
        name="swa_attn",
    )(sink, _swa_bias(tq), qk, qk, qk, qk, proj, proj, proj)


def _layer_norm(y, g, b):
    mu = jnp.mean(y, -1, keepdims=True)
    d = y - mu
    var = jnp.mean(d * d, -1, keepdims=True)
    return d * lax.rsqrt(var + LN_EPS) * g + b


PACK_SUBLANES = D_MODEL // 2 // LANES
HI_MASK = -65536


def _pack_words(lo, hi):
    lb = lax.bitcast_convert_type(lo.astype(BF16).astype(F32), I32)
    hb = lax.bitcast_convert_type(hi.astype(BF16).astype(F32), I32)
    return (hb & HI_MASK) | lax.shift_right_logical(lb, 16)


def _unpack_words(w):
    return lax.bitcast_convert_type(w << 16, F32), lax.bitcast_convert_type(w & HI_MASK, F32)


def _word_rows(j, tm, lead=(), base=0):
    return lead + (pl.ds(base * PACK_SUBLANES + j, tm, stride=PACK_SUBLANES), slice(None))


def _store_packed(ref, y, keep=None, clear=None):
    tm, d = y.shape
    for j in range(PACK_SUBLANES):
        w = _pack_words(y[:, LANES * j:LANES * (j + 1)], y[:, d // 2 + LANES * j:d // 2 + LANES * (j + 1)])
        if keep is not None:
            w = jnp.where(keep, w, jnp.where(clear, 0, ref[_word_rows(j, tm)]))
        ref[_word_rows(j, tm)] = w


def _load_packed_bf16(ref, tm, base=0):
    los, his = [], []
    for j in range(PACK_SUBLANES):
        lo, hi = _unpack_words(ref[_word_rows(j, tm, base=base)])
        los.append(lo.astype(BF16))
        his.append(hi.astype(BF16))
    return jnp.concatenate(los + his, axis=1)


def _out_ln_kernel(*refs, n_in):
    o_refs = refs[:n_in]
    w_refs = refs[n_in:2 * n_in]
    h_ref, g_ref, b_ref, hf_ref, hp_ref = refs[2 * n_in:]
    mix = _dot(o_refs[0][...], w_refs[0][...])
    for o_ref, w_ref in zip(o_refs[1:], w_refs[1:]):
        mix = mix + _dot(o_ref[...], w_ref[...])
    out = _layer_norm(DEEPNORM_ALPHA * h_ref[...] + mix, g_ref[...], b_ref[...])
    hf_ref[...] = out
    _store_packed(hp_ref, out)


def _out_ln(os_, ws, h, g, b):
    t, d = h.shape
    tm = min(256, t)
    n_in = len(os_)
    const = lambda i: (0, 0)
    rowblk = lambda a: pl.BlockSpec((tm, a.shape[1]), lambda i: (i, 0))
    return pl.pallas_call(
        functools.partial(_out_ln_kernel, n_in=n_in),
        grid=(t // tm,),
        in_specs=[rowblk(o) for o in os_] + [pl.BlockSpec(w.shape, const) for w in ws]
        + [rowblk(h), pl.BlockSpec((1, d), const), pl.BlockSpec((1, d), const)],
        out_specs=[pl.BlockSpec((tm, d), lambda i: (i, 0)),
                   pl.BlockSpec((tm * PACK_SUBLANES, LANES), lambda i: (i, 0))],
        out_shape=[jax.ShapeDtypeStruct((t, d), F32), jax.ShapeDtypeStruct((t * PACK_SUBLANES, LANES), I32)],
        compiler_params=_params(("parallel",)),
        name="out_proj_ln",
    )(*os_, *ws, h, g, b)


def _router_kernel(h_ref, wh_ref, wl_ref, bias_ref, idx_ref, gate_ref, rank_ref, cnt_ref, cnt_scr):
    i = pl.program_id(0)
    tm = h_ref.shape[0]

    @pl.when(i == 0)
    def _():
        cnt_scr[...] = jnp.zeros(cnt_scr.shape, F32)

    h = h_ref[...]
    hh = h.astype(BF16)
    hl = (h - hh.astype(F32)).astype(BF16)
    logits = _dot(hh, wh_ref[...]) + (_dot(hh, wl_ref[...]) + _dot(hl, wh_ref[...]))
    scores = jax.nn.sigmoid(logits)
    sel = scores + bias_ref[...]
    e_iota = lax.broadcasted_iota(I32, (tm, N_EXPERTS), 1).astype(F32)
    out_iota = lax.broadcasted_iota(I32, (tm, LANES), 1)
    onehots, gates = [], []
    idx_out = jnp.zeros((tm, LANES), F32)
    for k in range(TOP_K):
        m = jnp.max(sel, axis=1, keepdims=True)
        pick = jnp.min(jnp.where(sel == m, e_iota, float(N_EXPERTS)), axis=1, keepdims=True)
        oh = e_iota == pick
        onehots.append(oh)
        gates.append(jnp.sum(jnp.where(oh, scores, 0.0), axis=1, keepdims=True))
        idx_out = jnp.where(out_iota == k, pick, idx_out)
        sel = jnp.where(oh, -jnp.inf, sel)
    maskf = onehots[0].astype(F32)
    for oh in onehots[1:]:
        maskf = maskf + oh.astype(F32)
    tri = (lax.broadcasted_iota(I32, (tm, tm), 1) < lax.broadcasted_iota(I32, (tm, tm), 0)).astype(BF16)
    ranks = cnt_scr[...] + _dot(tri, maskf.astype(BF16))
    gsum = gates[0]
    for gk in gates[1:]:
        gsum = gsum + gk
    gate_out = jnp.zeros((tm, LANES), F32)
    rank_out = jnp.zeros((tm, LANES), F32)
    for k in range(TOP_K):
        gate_out = jnp.where(out_iota == k, gates[k] / gsum * ROUTED_SCALE, gate_out)
        rk = jnp.sum(jnp.where(onehots[k], ranks, 0.0), axis=1, keepdims=True)
        rank_out = jnp.where(out_iota == k, rk, rank_out)
    idx_ref[...] = idx_out.astype(I32)
    gate_ref[...] = gate_out
    rank_ref[...] = rank_out.astype(I32)
    total = cnt_scr[...] + jnp.sum(maskf, axis=0, keepdims=True)
    cnt_scr[...] = total
    cnt_ref[...] = total


def _router(h, w_hi, w_lo, bias):
    t, d = h.shape
    tm = min(256, t)
    const = lambda i: (0, 0)
    out = lambda: pl.BlockSpec((tm, LANES), lambda i: (i, 0))
    return pl.pallas_call(
        _router_kernel,
        grid=(t // tm,),
        in_specs=[
            pl.BlockSpec((tm, d), lambda i: (i, 0)),
            pl.BlockSpec((d, N_EXPERTS), const),
            pl.BlockSpec((d, N_EXPERTS), const),
            pl.BlockSpec((1, N_EXPERTS), const),
        ],
        out_specs=[out(), out(), out(), pl.BlockSpec((1, N_EXPERTS), const)],
        out_shape=[
            jax.ShapeDtypeStruct((t, LANES), I32),
            jax.ShapeDtypeStruct((t, LANES), F32),
            jax.ShapeDtypeStruct((t, LANES), I32),
            jax.ShapeDtypeStruct((1, N_EXPERTS), F32),
        ],
        scratch_shapes=[pltpu.VMEM((1, N_EXPERTS), F32)],
        compiler_params=_params(("arbitrary",)),
        name="router",
    )(h, w_hi, w_lo, bias)


def _token_tile(ref, r, lead=()):
    return ref.at[lead + (pl.ds(pl.multiple_of(r * PACK_SUBLANES, PACK_SUBLANES), PACK_SUBLANES),)]


def _dispatch_kernel(dest_ref, x_ref, xs_ref, sem):
    tm = x_ref.shape[0] // PACK_SUBLANES

    def body(r, carry):
        src = _token_tile(x_ref, r)
        for k in range(TOP_K):
            pltpu.make_async_copy(src, xs_ref.at[dest_ref[r * TOP_K + k]], sem).start(priority=k % 2)
        return carry

    lax.fori_loop(0, tm, body, 0)
    for k in range(TOP_K):
        pltpu.make_async_copy(xs_ref.at[pl.ds(0, tm)], xs_ref.at[pl.ds(0, tm)], sem).wait()


def _dispatch(hp, dest_flat):
    t = hp.shape[0] // PACK_SUBLANES
    tm = min(256, t)
    return pl.pallas_call(
        _dispatch_kernel,
        grid=(t // tm,),
        in_specs=[
            pl.BlockSpec((tm * TOP_K,), lambda i: (i,), memory_space=pltpu.SMEM),
            pl.BlockSpec((tm * PACK_SUBLANES, LANES), lambda i: (i, 0)),
        ],
        out_specs=pl.BlockSpec(memory_space=pl.ANY),
        out_shape=jax.ShapeDtypeStruct((t * TOP_K, PACK_SUBLANES, LANES), I32),
        scratch_shapes=[pltpu.SemaphoreType.DMA(())],
        compiler_params=_params(("arbitrary",)),
        name="moe_dispatch",
    )(dest_flat, hp)


def _expert_kernel(tile_ref, exp_ref, lo_ref, hi_ref, first_ref, x_ref, wg_ref, wu_ref, wd_ref, y_ref,
                   wg_bf, wu_bf, wd_bf):
    w = pl.program_id(0)
    tm = x_ref.shape[0] // PACK_SUBLANES
    lo = lo_ref[w]
    hi = hi_ref[w]

    @pl.when((w == 0) | (exp_ref[w] != exp_ref[jnp.maximum(w - 1, 0)]))
    def _():
        wg_bf[...] = wg_ref[0, 0].astype(BF16)
        wu_bf[...] = wu_ref[0, 0].astype(BF16)
        wd_bf[...] = wd_ref[0, 0].astype(BF16)

    @pl.when(hi > lo)
    def _():
        x = _load_packed_bf16(x_ref, tm)
        gate = _dot(x, wg_bf[...])
        up = _dot(x, wu_bf[...])
        hmid = (gate * jax.nn.sigmoid(gate) * up).astype(BF16)
        y = _dot(hmid, wd_bf[...])
        rows = tile_ref[w] * tm + lax.broadcasted_iota(I32, (tm, 1), 0)
        mine = (rows >= lo) & (rows < hi)
        _store_packed(y_ref, y, keep=mine, clear=first_ref[w] == 1)


def _experts(xs, wg, wu, wd, layer, tile_w, exp_w, lo_w, hi_w, first_w, tm):
    d = D_MODEL
    rows = pl.BlockSpec((tm * PACK_SUBLANES, LANES), lambda w, tl, ex, lo, hi, fi: (tl[w], 0))
    nw = tile_w.shape[0]
    grid_spec = pltpu.PrefetchScalarGridSpec(
        num_scalar_prefetch=5,
        grid=(nw,),
        in_specs=[
            rows,
            pl.BlockSpec((1, 1, d, D_EXPERT), lambda w, tl, ex, lo, hi, fi: (layer, ex[w], 0, 0)),
            pl.BlockSpec((1, 1, d, D_EXPERT), lambda w, tl, ex, lo, hi, fi: (layer, ex[w], 0, 0)),
            pl.BlockSpec((1, 1, D_EXPERT, d), lambda w, tl, ex, lo, hi, fi: (layer, ex[w], 0, 0)),
        ],
        out_specs=rows,
        scratch_shapes=[pltpu.VMEM((d, D_EXPERT), BF16), pltpu.VMEM((d, D_EXPERT), BF16),
                        pltpu.VMEM((D_EXPERT, d), BF16)],
    )
    return pl.pallas_call(
        _expert_kernel,
        grid_spec=grid_spec,
        out_shape=jax.ShapeDtypeStruct(xs.shape, I32),
        compiler_params=_params(("arbitrary",)),
        name="moe_experts",
    )(tile_w, exp_w, lo_w, hi_w, first_w, xs, wg, wu, wd)


COMBINE_HALF = 128


def _combine_kernel(dcur_ref, dnext_ref, ys_ref, gate_ref, h_ref, hp_ref, sg_ref, su_ref, sd_ref, g_ref, b_ref,
                    hf_out, hb_out, buf0, buf1, sem0, sem1):
    i = pl.program_id(0)
    th = buf0.shape[1] // PACK_SUBLANES

    def gather(dref, base, buf, sem):
        for r in range(th):
            for k in range(TOP_K):
                src = ys_ref.at[dref[(base + r) * TOP_K + k]]
                pltpu.make_async_copy(src, _token_tile(buf, r, (k,)), sem).start(priority=k % 2)

    def wait(sem):
        for k in range(TOP_K):
            pltpu.make_async_copy(ys_ref.at[pl.ds(0, th)], ys_ref.at[pl.ds(0, th)], sem).wait()

    def reduce_half(half, buf):
        rows = pl.ds(half * th, th)
        x = _load_packed_bf16(hp_ref, th, base=half * th)
        gate = _dot(x, sg_ref[...])
        up = _dot(x, su_ref[...])
        acc = _dot((gate * jax.nn.sigmoid(gate) * up).astype(BF16), sd_ref[...])
        acc = acc + DEEPNORM_ALPHA * h_ref[rows, :]
        gates = gate_ref[rows, :]
        los = [None] * PACK_SUBLANES
        his = [None] * PACK_SUBLANES
        for k in range(TOP_K):
            gk = gates[:, k:k + 1]
            for j in range(PACK_SUBLANES):
                lo, hi = _unpack_words(buf[_word_rows(j, th, (k,))])
                los[j] = gk * lo if k == 0 else los[j] + gk * lo
                his[j] = gk * hi if k == 0 else his[j] + gk * hi
        out = _layer_norm(acc + jnp.concatenate(los + his, axis=1), g_ref[...], b_ref[...])
        hf_out[rows, :] = out
        hb_out[rows, :] = out.astype(BF16)

    @pl.when(i == 0)
    def _():
        def body(r, carry):
            for k in range(TOP_K):
                src = ys_ref.at[dcur_ref[r * TOP_K + k]]
                pltpu.make_async_copy(src, _token_tile(buf0, r, (k,)), sem0).start(priority=k % 2)
            return carry

        lax.fori_loop(0, th, body, 0)

    wait(sem0)
    gather(dcur_ref, th, buf1, sem1)
    reduce_half(0, buf0)
    wait(sem1)
    gather(dnext_ref, 0, buf0, sem0)
    reduce_half(1, buf1)

    @pl.when(i == pl.num_programs(0) - 1)
    def _():
        wait(sem0)


def _combine(dest_flat, ys, gate, h, hp, sg, su, sd, g, b):
    t, d = h.shape
    th = min(COMBINE_HALF, t // 2)
    tm = 2 * th
    n = t // tm
    const = lambda i: (0, 0)
    row = lambda width: pl.BlockSpec((tm, width), lambda i: (i, 0))
    buf = pltpu.VMEM((TOP_K, th * PACK_SUBLANES, LANES), I32)
    return pl.pallas_call(
        _combine_kernel,
        grid=(n,),
        in_specs=[
            pl.BlockSpec((tm * TOP_K,), lambda i: (i,), memory_space=pltpu.SMEM),
            pl.BlockSpec((th * TOP_K,), lambda i: (jnp.minimum(2 * i + 2, 2 * n - 1),), memory_space=pltpu.SMEM),
            pl.BlockSpec(memory_space=pl.ANY),
            row(LANES), row(d), pl.BlockSpec((tm * PACK_SUBLANES, LANES), lambda i: (i, 0)),
            pl.BlockSpec(sg.shape, const), pl.BlockSpec(su.shape, const), pl.BlockSpec(sd.shape, const),
            pl.BlockSpec((1, d), const), pl.BlockSpec((1, d), const),
        ],
        out_specs=[row(d), row(d)],
        out_shape=[jax.ShapeDtypeStruct((t, d), F32), jax.ShapeDtypeStruct((t, d), BF16)],
        scratch_shapes=[buf, buf, pltpu.SemaphoreType.DMA(()), pltpu.SemaphoreType.DMA(())],
        compiler_params=_params(("arbitrary",)),
        name="moe_combine",
    )(dest_flat, dest_flat, ys, gate, h, hp, sg, su, sd, g, b)


EXPERT_TILE = 512


def _group_metadata(counts, idx, rank, n_assign):
    tm = min(EXPERT_TILE, n_assign)
    nt = n_assign // tm
    nw = nt + N_EXPERTS - 1
    sizes = counts.reshape(N_EXPERTS).astype(I32)
    ends = jnp.cumsum(sizes)
    starts = ends - sizes
    my_start = jnp.sum(jnp.where(idx[..., None] == jnp.arange(N_EXPERTS, dtype=I32), starts, 0), axis=-1)
    dest = (my_start + rank).reshape(-1).astype(I32)
    first_tile = starts // tm
    n_e = jnp.where(sizes > 0, (ends - 1) // tm - first_tile + 1, 0)
    cum = jnp.cumsum(n_e)
    off = cum - n_e
    total = cum[-1]
    w = jnp.arange(nw, dtype=I32)
    valid = w < total
    experts = jnp.arange(N_EXPERTS, dtype=I32)
    last_used = jnp.max(jnp.where(n_e > 0, experts, 0))
    e_w = jnp.where(valid, jnp.sum(cum[None, :] <= w[:, None], axis=1), last_used).astype(I32)
    pick = lambda table: jnp.sum(jnp.where(e_w[:, None] == experts, table, 0), axis=1)
    tile_w = jnp.where(valid, pick(first_tile) + (w - pick(off)), nt - 1).astype(I32)
    lo_w = jnp.where(valid, jnp.maximum(pick(starts), tile_w * tm), 0).astype(I32)
    hi_w = jnp.where(valid, jnp.minimum(pick(ends), (tile_w + 1) * tm), 0).astype(I32)
    first_w = jnp.concatenate([jnp.ones((1,), I32), (tile_w[1:] != tile_w[:-1]).astype(I32)])
    return dest, tile_w, e_w, lo_w, hi_w, first_w, tm


def _moe(h, hp, r_hi, r_lo, r_bias, wg, wu, wd, layer, sg, su, sd, g, b):
    t = h.shape[0]
    a = t * TOP_K
    idx, gate, rank, counts = _router(h, r_hi, r_lo, r_bias)
    dest, tile_w, e_w, lo_w, hi_w, first_w, tm = _group_metadata(counts, idx[:, :TOP_K], rank[:, :TOP_K], a)
    xs = _dispatch(hp, dest)
    ys = _experts(xs.reshape(a * PACK_SUBLANES, LANES), wg, wu, wd, layer, tile_w, e_w, lo_w, hi_w, first_w, tm)
    return _combine(dest, ys.reshape(a, PACK_SUBLANES, LANES), gate, h, hp, sg, su, sd, g, b)


def _even_weights(w_in, w_uq, w_ukv, w_out):
    o1 = MLA_Q_RANK
    o2 = o1 + MLA_KV_RANK
    o3 = o2 + MLA_ROPE
    o4 = o3 + 4 * DIFF_HEADS * DIFF_QK
    c_q, c_kv, k_r, dq, dv = w_in[:, :o1], w_in[:, o1:o2], w_in[:, o2:o3], w_in[:, o3:o4], w_in[:, o4:]
    dq = dq.reshape(D_MODEL, DIFF_HEADS, 4, DIFF_QK).transpose(0, 2, 1, 3).reshape(D_MODEL, -1)
    pad = jnp.zeros((D_MODEL, 512 - MLA_KV_RANK - MLA_ROPE), w_in.dtype)
    w_proj = jnp.concatenate([dq, dv, c_q, c_kv, k_r, pad], axis=1).astype(BF16)
    uq = w_uq.reshape(MLA_Q_RANK, MLA_HEADS, MLA_NOPE + MLA_ROPE)
    wqn = uq[:, :, :MLA_NOPE].reshape(MLA_Q_RANK, -1).astype(BF16)
    wqr = uq[:, :, MLA_NOPE:].reshape(MLA_Q_RANK, -1).astype(BF16)
    ukv = w_ukv.reshape(MLA_KV_RANK, MLA_HEADS, MLA_NOPE + MLA_V)
    wkn = ukv[:, :, :MLA_NOPE].reshape(MLA_KV_RANK, -1).astype(BF16)
    wv = ukv[:, :, MLA_NOPE:].reshape(MLA_KV_RANK, -1).astype(BF16)
    n_mla = MLA_HEADS * MLA_V
    return w_proj, wqn, wqr, wkn, wv, w_out[:n_mla].astype(BF16), w_out[n_mla:].astype(BF16)


def _odd_weights(w_in, w_out):
    nq = SWA_Q_HEADS * SWA_HD
    nkv = SWA_KV_HEADS * SWA_HD
    dup = lambda w: jnp.concatenate([w.reshape(D_MODEL, SWA_KV_HEADS, 1, SWA_HD)] * 2, axis=2).reshape(D_MODEL, -1)
    w_proj = jnp.concatenate([w_in[:, :nq], dup(w_in[:, nq:nq + nkv]), dup(w_in[:, nq + nkv:])], axis=1)
    return w_proj.astype(BF16), w_out.astype(BF16)


def kernel(x, positions, ab_w_in, mla_q_norm, mla_w_uq, mla_kv_norm, mla_w_ukv, diff_lambda_q1, diff_lambda_k1, diff_lambda_q2, diff_lambda_k2, diff_sub_norm, ab_w_out, swa_w_in, swa_sink, swa_w_out, mix_ln_g, mix_ln_b, ffn_ln_g, ffn_ln_b, router_w, router_bias, exp_w_gate, exp_w_up, exp_w_down, shared_w_gate, shared_w_up, shared_w_down):
    b, s, d = x.shape
    t = b * s
    tab = _rope_tables(positions)
    h = x.reshape(t, d)
    hb = h.astype(BF16)
    row = lambda v: v.reshape(1, -1)
    for layer in range(DEPTH):
        i = layer // 2
        if layer % 2 == 0:
            w_proj, wqn, wqr, wkn, wv, wo_mla, wo_diff = _even_weights(
                ab_w_in[i], mla_w_uq[i], mla_w_ukv[i], ab_w_out[i])
            proj = _matmul(hb, w_proj, "even_in_proj")
            q, k, v = _mla_prep(proj, tab, row(mla_q_norm[i]), row(mla_kv_norm[i]), wqn, wqr, wkn, wv,
                                b, s, col_cq=6, col_ckv=7)
            o_mla = _mla_attn(q, k, v).reshape(t, -1)
            dqk = _diff_prep(proj, tab, b, s)
            lam_p = jnp.stack([diff_lambda_q1[i], diff_lambda_k1[i], diff_lambda_q2[i], diff_lambda_k2[i]])
            lam_init = 0.8 - 0.6 * math.exp(-0.3 * layer)
            o_diff = _diff_attn(dqk, proj, 2048 // DIFF_V, lam_p, row(diff_sub_norm[i]), lam_init,
                                b, s).reshape(t, -1)
            h, hp = _out_ln([o_mla, o_diff], [wo_mla, wo_diff], h, row(mix_ln_g[layer]), row(mix_ln_b[layer]))
        else:
            w_proj, wo = _odd_weights(swa_w_in[i], swa_w_out[i])
            proj = _matmul(hb, w_proj, "odd_in_proj")
            nq = SWA_Q_HEADS * SWA_HD
            n_rot = nq + SWA_KV_HEADS * LANES
            qk = _swa_prep(proj, tab, n_rot)
            o = _swa_attn(qk, proj, swa_sink[i], b, s, col_k=nq // LANES, col_v=n_rot // LANES)
            h, hp = _out_ln([o], [wo], h, row(mix_ln_g[layer]), row(mix_ln_b[layer]))
        rw = router_w[layer]
        r_hi = rw.astype(BF16)
        r_lo = (rw - r_hi.astype(F32)).astype(BF16)
        h, hb = _moe(h, hp, r_hi, r_lo, row(router_bias[layer]),
                     exp_w_gate, exp_w_up, exp_w_down, layer,
                     shared_w_gate[layer].astype(BF16), shared_w_up[layer].astype(BF16),
                     shared_w_down[layer].astype(BF16), row(ffn_ln_g[layer]), row(ffn_ln_b[layer]))
    return h.reshape(b, s, d)
```

```python
import functools
import math

import numpy as np
import jax
import jax.numpy as jnp
from jax import lax
from jax.experimental import pallas as pl
from jax.experimental.pallas import tpu as pltpu

F32 = jnp.float32
BF16 = jnp.bfloat16
I32 = jnp.int32

D_MODEL = 2048
DEPTH = 4
ROPE_THETA = 500000.0
LN_EPS = 1e-5
RMS_EPS = 1e-6
MLA_HEADS = 8
MLA_NOPE = 128
MLA_ROPE = 64
MLA_V = 128
MLA_Q_RANK = 512
MLA_KV_RANK = 256
DIFF_HEADS = 8
DIFF_QK = 64
DIFF_V = 128
DIFF_ROT = DIFF_QK // 4
SWA_Q_HEADS = 32
SWA_KV_HEADS = 4
SWA_GROUP = SWA_Q_HEADS // SWA_KV_HEADS
SWA_HD = 64
WINDOW = 128
N_EXPERTS = 64
TOP_K = 8
D_EXPERT = 256
ROUTED_SCALE = 2.5
DEEPNORM_ALPHA = (2 * DEPTH) ** 0.25
LOG2E = math.log2(math.e)

LANES = 128
MLA_QK_PAD = 256
VMEM_LIMIT = 48 << 20
ATTN_TQ = 512
MLA_TQ = 1024
MLA_TK = 2048
ATTN_TK = 2048


def _params(sem):
    return pltpu.CompilerParams(dimension_semantics=sem, vmem_limit_bytes=VMEM_LIMIT)


def _dot(a, b):
    return jnp.dot(a, b, preferred_element_type=F32)


def _dot_nt(a, b):
    return lax.dot_general(a, b, (((1,), (1,)), ((), ())), preferred_element_type=F32)


def _tables_kernel(pos_ref, c_ref, out_ref):
    pos = pos_ref[...]
    a64 = pos * c_ref[0:1, :]
    a16 = pos * c_ref[3:4, :]
    s64 = jnp.sin(a64)
    s16 = jnp.sin(a16)
    out_ref[0] = jnp.cos(a64)
    out_ref[1] = s64 * c_ref[1:2, :]
    out_ref[2] = s64 * c_ref[2:3, :]
    out_ref[3] = jnp.cos(a16)
    out_ref[4] = s16 * c_ref[4:5, :]
    out_ref[5] = s16 * c_ref[5:6, :]


def _rope_consts():
    j = np.arange(LANES) % 64
    inv64 = ROPE_THETA ** (-jnp.arange(0, MLA_ROPE, 2, dtype=F32) / MLA_ROPE)
    inv16 = ROPE_THETA ** (-jnp.arange(0, DIFF_ROT, 2, dtype=F32) / DIFF_ROT)
    f64 = inv64[j % 32]
    f16 = jnp.where(j < 16, inv16[j % 8], 0.0)
    rows = [
        f64,
        jnp.asarray(np.where(j < 32, -1.0, 0.0), F32),
        jnp.asarray(np.where(j >= 32, 1.0, 0.0), F32),
        f16,
        jnp.asarray(np.where(j < 8, -1.0, 0.0), F32),
        jnp.asarray(np.where((j >= 8) & (j < 16), 1.0, 0.0), F32),
        jnp.zeros((LANES,), F32),
        jnp.zeros((LANES,), F32),
    ]
    return jnp.stack(rows).astype(F32)


def _rope_tables(positions):
    t = positions.size
    pos = jnp.broadcast_to(positions.reshape(t, 1).astype(F32), (t, LANES))
    tm = min(512, t)
    return pl.pallas_call(
        _tables_kernel,
        grid=(t // tm,),
        in_specs=[pl.BlockSpec((tm, LANES), lambda i: (i, 0)), pl.BlockSpec((8, LANES), lambda i: (0, 0))],
        out_specs=pl.BlockSpec((6, tm, LANES), lambda i: (0, i, 0)),
        out_shape=jax.ShapeDtypeStruct((6, t, LANES), F32),
        compiler_params=_params(("parallel",)),
        name="rope_tables",
    )(pos, _rope_consts())


def _rope_tile(x, c, sa, sb, half):
    return x * c + pltpu.roll(x, LANES - half, 1) * sa + pltpu.roll(x, half, 1) * sb


def _mm_kernel(x_ref, w_ref, o_ref):
    o_ref[...] = _dot(x_ref[...], w_ref[...]).astype(o_ref.dtype)


def _matmul(x, w, name, tm=512, tn=1024):
    m, k = x.shape
    n = w.shape[1]
    tm = min(tm, m)
    tn = min(tn, n)
    return pl.pallas_call(
        _mm_kernel,
        grid=(n // tn, m // tm),
        in_specs=[pl.BlockSpec((tm, k), lambda j, i: (i, 0)), pl.BlockSpec((k, tn), lambda j, i: (0, j))],
        out_specs=pl.BlockSpec((tm, tn), lambda j, i: (i, j)),
        out_shape=jax.ShapeDtypeStruct((m, n), BF16),
        compiler_params=_params(("parallel", "parallel")),
        name=name,
    )(x, w)


def _mm_rope_kernel(x_ref, w_ref, tab_ref, scale_ref, o_ref):
    acc = _dot(x_ref[...], w_ref[...])
    c, sa, sb = tab_ref[0], tab_ref[1], tab_ref[2]
    for j in range(o_ref.shape[1] // LANES):
        sl = slice(LANES * j, LANES * (j + 1))
        o_ref[:, sl] = (_rope_tile(acc[:, sl], c, sa, sb, DIFF_ROT // 2) * scale_ref[:, sl]).astype(o_ref.dtype)


def _matmul_rope(x, w, tab, scale, name, tm=512, tn=1024):
    m, k = x.shape
    n = w.shape[1]
    tm = min(tm, m)
    return pl.pallas_call(
        _mm_rope_kernel,
        grid=(n // tn, m // tm),
        in_specs=[
            pl.BlockSpec((tm, k), lambda j, i: (i, 0)),
            pl.BlockSpec((k, tn), lambda j, i: (0, j)),
            pl.BlockSpec((3, tm, LANES), lambda j, i: (1, i, 0)),
            pl.BlockSpec((1, tn), lambda j, i: (0, j)),
        ],
        out_specs=pl.BlockSpec((tm, tn), lambda j, i: (i, j)),
        out_shape=jax.ShapeDtypeStruct((m, n), BF16),
        compiler_params=_params(("parallel", "parallel")),
        name=name,
    )(x, w, tab, scale)


def _rms(x, g):
    return x * lax.rsqrt(jnp.mean(x * x, -1, keepdims=True) + RMS_EPS) * g


def _mla_prep_kernel(cq_ref, ckv_ref, tab_ref, qg_ref, kvg_ref, wqn_ref, wqr_ref, wkn_ref, wv_ref,
                     q_ref, k_ref, v_ref, *, scale):
    tm = cq_ref.shape[0]
    cq = cq_ref[...].astype(F32)
    ckv_all = ckv_ref[...].astype(F32)
    ckv = ckv_all[:, :MLA_KV_RANK]
    kr = ckv_all[:, MLA_KV_RANK:MLA_KV_RANK + LANES]
    cqn = _rms(cq, qg_ref[...]).astype(BF16)
    ckvn = _rms(ckv, kvg_ref[...]).astype(BF16)
    qn = _dot(cqn, wqn_ref[...]) * scale
    qr = _dot(cqn, wqr_ref[...]) * scale
    kn = _dot(ckvn, wkn_ref[...])
    vv = _dot(ckvn, wv_ref[...])
    c, sa, sb = tab_ref[0], tab_ref[1], tab_ref[2]
    lo = lax.broadcasted_iota(I32, (tm, LANES), 1) < MLA_ROPE
    half = MLA_ROPE // 2
    krr = jnp.where(lo, _rope_tile(kr, c, sa, sb, half), 0.0).astype(BF16)
    for j in range(MLA_HEADS // 2):
        r = _rope_tile(qr[:, LANES * j:LANES * (j + 1)], c, sa, sb, half)
        q_ref[0, 2 * j, :, LANES:] = jnp.where(lo, r, 0.0).astype(BF16)
        q_ref[0, 2 * j + 1, :, LANES:] = jnp.where(lo, pltpu.roll(r, MLA_ROPE, 1), 0.0).astype(BF16)
    for h in range(MLA_HEADS):
        sl = slice(LANES * h, LANES * (h + 1))
        q_ref[0, h, :, :LANES] = qn[:, sl].astype(BF16)
        k_ref[0, h, :, :LANES] = kn[:, sl].astype(BF16)
        k_ref[0, h, :, LANES:] = krr
        v_ref[0, h] = vv[:, sl].astype(BF16)


def _mla_prep(proj, tab, q_norm, kv_norm, wqn, wqr, wkn, wv, b, s, col_cq, col_ckv):
    tm = min(256, s)
    nt = s // tm
    h = MLA_HEADS
    row = lambda bi, i: bi * nt + i
    const = lambda bi, i: (0, 0)
    return pl.pallas_call(
        functools.partial(_mla_prep_kernel, scale=(MLA_NOPE + MLA_ROPE) ** -0.5 * LOG2E),
        grid=(b, nt),
        in_specs=[
            pl.BlockSpec((tm, MLA_Q_RANK), lambda bi, i: (row(bi, i), col_cq)),
            pl.BlockSpec((tm, 512), lambda bi, i: (row(bi, i), col_ckv)),
            pl.BlockSpec((3, tm, LANES), lambda bi, i: (0, row(bi, i), 0)),
            pl.BlockSpec((1, MLA_Q_RANK), const),
            pl.BlockSpec((1, MLA_KV_RANK), const),
            pl.BlockSpec(wqn.shape, const),
            pl.BlockSpec(wqr.shape, const),
            pl.BlockSpec(wkn.shape, const),
            pl.BlockSpec(wv.shape, const),
        ],
        out_specs=[
            pl.BlockSpec((1, h, tm, MLA_QK_PAD), lambda bi, i: (bi, 0, i, 0)),
            pl.BlockSpec((1, h, tm, MLA_QK_PAD), lambda bi, i: (bi, 0, i, 0)),
            pl.BlockSpec((1, h, tm, MLA_V), lambda bi, i: (bi, 0, i, 0)),
        ],
        out_shape=[
            jax.ShapeDtypeStruct((b, h, s, MLA_QK_PAD), BF16),
            jax.ShapeDtypeStruct((b, h, s, MLA_QK_PAD), BF16),
            jax.ShapeDtypeStruct((b, h, s, MLA_V), BF16),
        ],
        compiler_params=_params(("parallel", "parallel")),
        name="mla_prep",
    )(proj, proj, tab, q_norm, kv_norm, wqn, wqr, wkn, wv)


def _online_softmax_loop(qs, k_ats, v_at, nk):
    tq = qs[0].shape[0]

    def body(j, carry):
        v = v_at(j)
        out = []
        for q, k_at, (m, l, acc) in zip(qs, k_ats, carry):
            s = _dot_nt(q, k_at(j))
            m_new = jnp.maximum(m, jnp.max(s, axis=1, keepdims=True))
            alpha = jnp.exp2(m - m_new)
            p = jnp.exp2(s - m_new)
            l = alpha * l + jnp.sum(p, axis=1, keepdims=True)
            acc = alpha * acc + _dot(p.astype(BF16), v)
            out.append((m_new, l, acc))
        return tuple(out)

    one = (jnp.full((tq, 1), -jnp.inf, F32), jnp.zeros((tq, 1), F32), jnp.zeros((tq, v_at(0).shape[1]), F32))
    final = lax.fori_loop(0, nk, body, tuple(one for _ in qs), unroll=True)
    return [(l, acc) for _, l, acc in final]


def _mla_attn_kernel(q_ref, k_ref, v_ref, o_ref, *, tk):
    nk = k_ref.shape[2] // tk
    chunk = lambda j: pl.ds(pl.multiple_of(j * tk, tk), tk)
    (l, acc), = _online_softmax_loop(
        [q_ref[0, 0]], [lambda j: k_ref[0, 0, chunk(j), :]], lambda j: v_ref[0, 0, chunk(j), :], nk)
    o_ref[0] = (acc / l).astype(o_ref.dtype)


def _mla_attn(q, k, v):
    b, h, s, _ = q.shape
    tq = min(MLA_TQ, s)
    return pl.pallas_call(
        functools.partial(_mla_attn_kernel, tk=min(MLA_TK, s)),
        grid=(b, h, s // tq),
        in_specs=[
            pl.BlockSpec((1, 1, tq, MLA_QK_PAD), lambda bi, hi, i: (bi, hi, i, 0)),
            pl.BlockSpec((1, 1, s, MLA_QK_PAD), lambda bi, hi, i: (bi, hi, 0, 0)),
            pl.BlockSpec((1, 1, s, MLA_V), lambda bi, hi, i: (bi, hi, 0, 0)),
        ],
        out_specs=pl.BlockSpec((1, tq, MLA_V), lambda bi, hi, i: (bi, i, hi)),
        out_shape=jax.ShapeDtypeStruct((b, s, h * MLA_V), BF16),
        compiler_params=_params(("parallel", "parallel", "arbitrary")),
        name="mla_attn",
    )(q, k, v)


def _diff_attn_kernel(q1_ref, q2_ref, k1_ref, k2_ref, v_ref, lam_ref, g_ref, o_ref, *, tk, lam_init):
    hi = pl.program_id(1)
    tq = q1_ref.shape[0]
    nk = v_ref.shape[0] // tk
    chunk = lambda j: pl.ds(pl.multiple_of(j * tk, tk), tk)
    mine = (lax.broadcasted_iota(I32, (tq, LANES), 1) // DIFF_QK) == (hi % 2)
    zero = jnp.zeros((), BF16)
    qs = [jnp.where(mine, q1_ref[...], zero), jnp.where(mine, q2_ref[...], zero)]
    k_ats = [lambda j: k1_ref[chunk(j), :], lambda j: k2_ref[chunk(j), :]]
    (l1, a1), (l2, a2) = _online_softmax_loop(qs, k_ats, lambda j: v_ref[chunk(j), :], nk)
    lp = lam_ref[...]
    lam = (jnp.exp(jnp.sum(lp[0:1] * lp[1:2], axis=1, keepdims=True))
           - jnp.exp(jnp.sum(lp[2:3] * lp[3:4], axis=1, keepdims=True)) + lam_init)
    o = a1 / l1 - lam * (a2 / l2)
    o_ref[0] = (_rms(o, g_ref[...]) * (1.0 - lam_init)).astype(o_ref.dtype)


def _diff_attn(dqk, proj, col_v, lam_p, sub_norm, lam_init, b, s):
    h = DIFF_HEADS
    tq = min(ATTN_TQ, s)
    nq = s // tq
    pairs = h // 2
    qspec = lambda which: pl.BlockSpec((tq, LANES), lambda bi, hi, i: (bi * nq + i, which * pairs + hi // 2))
    kspec = lambda which: pl.BlockSpec((s, LANES), lambda bi, hi, i: (bi, which * pairs + hi // 2))
    const = lambda bi, hi, i: (0, 0)
    return pl.pallas_call(
        functools.partial(_diff_attn_kernel, tk=min(ATTN_TK, s), lam_init=lam_init),
        grid=(b, h, s // tq),
        in_specs=[
            qspec(0), qspec(1), kspec(2), kspec(3),
            pl.BlockSpec((s, DIFF_V), lambda bi, hi, i: (bi, col_v + hi)),
            pl.BlockSpec((4, DIFF_QK), const),
            pl.BlockSpec((1, DIFF_V), const),
        ],
        out_specs=pl.BlockSpec((1, tq, DIFF_V), lambda bi, hi, i: (bi, i, hi)),
        out_shape=jax.ShapeDtypeStruct((b, s, h * DIFF_V), BF16),
        compiler_params=_params(("parallel", "parallel", "arbitrary")),
        name="diff_attn",
    )(dqk, dqk, dqk, dqk, proj, lam_p, sub_norm)


def _swa_attn_kernel(sink_ref, bias_ref, q_ref, kp_ref, km_ref, kn_ref, vp_ref, vm_ref, vn_ref, o_ref):
    g = pl.program_id(1)
    tq = q_ref.shape[0]
    k = jnp.concatenate([kp_ref[...], km_ref[...], kn_ref[...]], axis=0)
    v = jnp.concatenate([vp_ref[...], vm_ref[...], vn_ref[...]], axis=0)
    bias = bias_ref[0]
    lane = lax.broadcasted_iota(I32, (tq, LANES), 1)
    zero = jnp.zeros((), BF16)
    qs = []
    for h in range(SWA_GROUP):
        qp = q_ref[:, LANES * (h // 2):LANES * (h // 2 + 1)]
        qs.append(jnp.where((lane // SWA_HD) == h % 2, qp, zero))
    s_all = _dot_nt(jnp.concatenate(qs, axis=0), k)
    ps, denoms = [], []
    for h in range(SWA_GROUP):
        sink = sink_ref[g * SWA_GROUP + h] * LOG2E
        sc = s_all[h * tq:(h + 1) * tq] + bias
        m = jnp.maximum(jnp.max(sc, axis=1, keepdims=True), sink)
        p = jnp.exp2(sc - m)
        denoms.append(jnp.sum(p, axis=1, keepdims=True) + jnp.exp2(sink - m))
        ps.append(p.astype(BF16))
    o_all = _dot(jnp.concatenate(ps, axis=0), v)
    for pair in range(SWA_GROUP // 2):
        lo = o_all[2 * pair * tq:(2 * pair + 1) * tq] / denoms[2 * pair]
        hi = o_all[(2 * pair + 1) * tq:(2 * pair + 2) * tq] / denoms[2 * pair + 1]
        o_ref[:, LANES * pair:LANES * (pair + 1)] = jnp.where(lane < SWA_HD, lo, hi).astype(o_ref.dtype)


def _swa_bias(tq):
    qi = np.arange(tq)[:, None]
    kj = np.arange(tq + 2 * WINDOW)[None, :] - WINDOW
    band = np.abs(kj - qi) <= WINDOW
    masks = [band, band & (kj >= 0), band & (kj < tq), band & (kj >= 0) & (kj < tq)]
    return jnp.asarray(np.where(np.stack(masks), 0.0, -np.inf), F32)


def _swa_attn(qk, proj, sink, b, s, col_k, col_v, tq=256):
    tq = min(tq, s)
    nt = s // tq
    r = tq // WINDOW
    nwb = s // WINDOW
    gw = SWA_GROUP * SWA_HD
    row = lambda bi, i: bi * nt + i
    prev = lambda bi, i: bi * nwb + jnp.maximum(i * r - 1, 0)
    nxt = lambda bi, i: bi * nwb + jnp.minimum((i + 1) * r, nwb - 1)
    edge = lambda i: jnp.where(i == 0, 1, 0) + jnp.where(i == nt - 1, 2, 0)
    grid_spec = pltpu.PrefetchScalarGridSpec(
        num_scalar_prefetch=1,
        grid=(b, SWA_KV_HEADS, nt),
        in_specs=[
            pl.BlockSpec((1, tq, tq + 2 * WINDOW), lambda bi, g, i, sk: (edge(i), 0, 0)),
            pl.BlockSpec((tq, gw), lambda bi, g, i, sk: (row(bi, i), g)),
            pl.BlockSpec((WINDOW, LANES), lambda bi, g, i, sk: (prev(bi, i), col_k + g)),
            pl.BlockSpec((tq, LANES), lambda bi, g, i, sk: (row(bi, i), col_k + g)),
            pl.BlockSpec((WINDOW, LANES), lambda bi, g, i, sk: (nxt(bi, i), col_k + g)),
            pl.BlockSpec((WINDOW, LANES), lambda bi, g, i, sk: (prev(bi, i), col_v + g)),
            pl.BlockSpec((tq, LANES), lambda bi, g, i, sk: (row(bi, i), col_v + g)),
            pl.BlockSpec((WINDOW, LANES), lambda bi, g, i, sk: (nxt(bi, i), col_v + g)),
        ],
        out_specs=pl.BlockSpec((tq, gw), lambda bi, g, i, sk: (row(bi, i), g)),
    )
    return pl.pallas_call(
        _swa_attn_kernel,
        grid_spec=grid_spec,
        out_shape=jax.ShapeDtypeStruct((b * s, SWA_Q_HEADS * SWA_HD), BF16),
        compiler_params=_params(("parallel", "parallel", "parallel")),
        name="swa_attn",
    )(sink, _swa_bias(tq), qk, qk, qk, qk, proj, proj, proj)


def _layer_norm(y, g, b):
    mu = jnp.mean(y, -1, keepdims=True)
    d = y - mu
    var = jnp.mean(d * d, -1, keepdims=True)
    return d * lax.rsqrt(var + LN_EPS) * g + b


PACK_SUBLANES = D_MODEL // 2 // LANES
HI_MASK = -65536


def _pack_words(lo, hi):
    lb = lax.bitcast_convert_type(lo.astype(BF16).astype(F32), I32)
    hb = lax.bitcast_convert_type(hi.astype(BF16).astype(F32), I32)
    return (hb & HI_MASK) | lax.shift_right_logical(lb, 16)


def _unpack_words(w):
    return lax.bitcast_convert_type(w << 16, F32), lax.bitcast_convert_type(w & HI_MASK, F32)


def _word_rows(j, tm, lead=(), base=0):
    return lead + (pl.ds(base * PACK_SUBLANES + j, tm, stride=PACK_SUBLANES), slice(None))


def _store_packed(ref, y, keep=None, clear=None):
    tm, d = y.shape
    for j in range(PACK_SUBLANES):
        w = _pack_words(y[:, LANES * j:LANES * (j + 1)], y[:, d // 2 + LANES * j:d // 2 + LANES * (j + 1)])
        if keep is not None:
            w = jnp.where(keep, w, jnp.where(clear, 0, ref[_word_rows(j, tm)]))
        ref[_word_rows(j, tm)] = w


def _load_packed_bf16(ref, tm, base=0):
    los, his = [], []
    for j in range(PACK_SUBLANES):
        lo, hi = _unpack_words(ref[_word_rows(j, tm, base=base)])
        los.append(lo.astype(BF16))
        his.append(hi.astype(BF16))
    return jnp.concatenate(los + his, axis=1)


def _out_ln_kernel(*refs, n_in):
    o_refs = refs[:n_in]
    w_refs = refs[n_in:2 * n_in]
    h_ref, g_ref, b_ref, hf_ref, hp_ref = refs[2 * n_in:]
    mix = _dot(o_refs[0][...], w_refs[0][...])
    for o_ref, w_ref in zip(o_refs[1:], w_refs[1:]):
        mix = mix + _dot(o_ref[...], w_ref[...])
    out = _layer_norm(DEEPNORM_ALPHA * h_ref[...] + mix, g_ref[...], b_ref[...])
    hf_ref[...] = out
    _store_packed(hp_ref, out)


def _out_ln(os_, ws, h, g, b):
    t, d = h.shape
    tm = min(256, t)
    n_in = len(os_)
    const = lambda i: (0, 0)
    rowblk = lambda a: pl.BlockSpec((tm, a.shape[1]), lambda i: (i, 0))
    return pl.pallas_call(
        functools.partial(_out_ln_kernel, n_in=n_in),
        grid=(t // tm,),
        in_specs=[rowblk(o) for o in os_] + [pl.BlockSpec(w.shape, const) for w in ws]
        + [rowblk(h), pl.BlockSpec((1, d), const), pl.BlockSpec((1, d), const)],
        out_specs=[pl.BlockSpec((tm, d), lambda i: (i, 0)),
                   pl.BlockSpec((tm * PACK_SUBLANES, LANES), lambda i: (i, 0))],
        out_shape=[jax.ShapeDtypeStruct((t, d), F32), jax.ShapeDtypeStruct((t * PACK_SUBLANES, LANES), I32)],
        compiler_params=_params(("parallel",)),
        name="out_proj_ln",
    )(*os_, *ws, h, g, b)


def _router_kernel(h_ref, wh_ref, wl_ref, bias_ref, idx_ref, gate_ref, rank_ref, cnt_ref, cnt_scr):
    i = pl.program_id(0)
    tm = h_ref.shape[0]

    @pl.when(i == 0)
    def _():
        cnt_scr[...] = jnp.zeros(cnt_scr.shape, F32)

    h = h_ref[...]
    hh = h.astype(BF16)
    hl = (h - hh.astype(F32)).astype(BF16)
    logits = _dot(hh, wh_ref[...]) + (_dot(hh, wl_ref[...]) + _dot(hl, wh_ref[...]))
    scores = jax.nn.sigmoid(logits)
    sel = scores + bias_ref[...]
    e_iota = lax.broadcasted_iota(I32, (tm, N_EXPERTS), 1).astype(F32)
    out_iota = lax.broadcasted_iota(I32, (tm, LANES), 1)
    onehots, gates = [], []
    idx_out = jnp.zeros((tm, LANES), F32)
    for k in range(TOP_K):
        m = jnp.max(sel, axis=1, keepdims=True)
        pick = jnp.min(jnp.where(sel == m, e_iota, float(N_EXPERTS)), axis=1, keepdims=True)
        oh = e_iota == pick
        onehots.append(oh)
        gates.append(jnp.sum(jnp.where(oh, scores, 0.0), axis=1, keepdims=True))
        idx_out = jnp.where(out_iota == k, pick, idx_out)
        sel = jnp.where(oh, -jnp.inf, sel)
    maskf = onehots[0].astype(F32)
    for oh in onehots[1:]:
        maskf = maskf + oh.astype(F32)
    tri = (lax.broadcasted_iota(I32, (tm, tm), 1) < lax.broadcasted_iota(I32, (tm, tm), 0)).astype(BF16)
    ranks = cnt_scr[...] + _dot(tri, maskf.astype(BF16))
    gsum = gates[0]
    for gk in gates[1:]:
        gsum = gsum + gk
    gate_out = jnp.zeros((tm, LANES), F32)
    rank_out = jnp.zeros((tm, LANES), F32)
    for k in range(TOP_K):
        gate_out = jnp.where(out_iota == k, gates[k] / gsum * ROUTED_SCALE, gate_out)
        rk = jnp.sum(jnp.where(onehots[k], ranks, 0.0), axis=1, keepdims=True)
        rank_out = jnp.where(out_iota == k, rk, rank_out)
    idx_ref[...] = idx_out.astype(I32)
    gate_ref[...] = gate_out
    rank_ref[...] = rank_out.astype(I32)
    total = cnt_scr[...] + jnp.sum(maskf, axis=0, keepdims=True)
    cnt_scr[...] = total
    cnt_ref[...] = total


def _router(h, w_hi, w_lo, bias):
    t, d = h.shape
    tm = min(256, t)
    const = lambda i: (0, 0)
    out = lambda: pl.BlockSpec((tm, LANES), lambda i: (i, 0))
    return pl.pallas_call(
        _router_kernel,
        grid=(t // tm,),
        in_specs=[
            pl.BlockSpec((tm, d), lambda i: (i, 0)),
            pl.BlockSpec((d, N_EXPERTS), const),
            pl.BlockSpec((d, N_EXPERTS), const),
            pl.BlockSpec((1, N_EXPERTS), const),
        ],
        out_specs=[out(), out(), out(), pl.BlockSpec((1, N_EXPERTS), const)],
        out_shape=[
            jax.ShapeDtypeStruct((t, LANES), I32),
            jax.ShapeDtypeStruct((t, LANES), F32),
            jax.ShapeDtypeStruct((t, LANES), I32),
            jax.ShapeDtypeStruct((1, N_EXPERTS), F32),
        ],
        scratch_shapes=[pltpu.VMEM((1, N_EXPERTS), F32)],
        compiler_params=_params(("arbitrary",)),
        name="router",
    )(h, w_hi, w_lo, bias)


def _token_tile(ref, r, lead=()):
    return ref.at[lead + (pl.ds(pl.multiple_of(r * PACK_SUBLANES, PACK_SUBLANES), PACK_SUBLANES),)]


def _dispatch_kernel(dest_ref, x_ref, xs_ref, sem):
    tm = x_ref.shape[0] // PACK_SUBLANES

    def body(r, carry):
        src = _token_tile(x_ref, r)
        for k in range(TOP_K):
            pltpu.make_async_copy(src, xs_ref.at[dest_ref[r * TOP_K + k]], sem).start(priority=k % 2)
        return carry

    lax.fori_loop(0, tm, body, 0)
    for k in range(TOP_K):
        pltpu.make_async_copy(xs_ref.at[pl.ds(0, tm)], xs_ref.at[pl.ds(0, tm)], sem).wait()


def _dispatch(hp, dest_flat):
    t = hp.shape[0] // PACK_SUBLANES
    tm = min(256, t)
    return pl.pallas_call(
        _dispatch_kernel,
        grid=(t // tm,),
        in_specs=[
            pl.BlockSpec((tm * TOP_K,), lambda i: (i,), memory_space=pltpu.SMEM),
            pl.BlockSpec((tm * PACK_SUBLANES, LANES), lambda i: (i, 0)),
        ],
        out_specs=pl.BlockSpec(memory_space=pl.ANY),
        out_shape=jax.ShapeDtypeStruct((t * TOP_K, PACK_SUBLANES, LANES), I32),
        scratch_shapes=[pltpu.SemaphoreType.DMA(())],
        compiler_params=_params(("arbitrary",)),
        name="moe_dispatch",
    )(dest_flat, hp)


def _expert_kernel(tile_ref, exp_ref, lo_ref, hi_ref, first_ref, x_ref, wg_ref, wu_ref, wd_ref, y_ref,
                   wg_bf, wu_bf, wd_bf):
    w = pl.program_id(0)
    tm = x_ref.shape[0] // PACK_SUBLANES
    lo = lo_ref[w]
    hi = hi_ref[w]

    @pl.when((w == 0) | (exp_ref[w] != exp_ref[jnp.maximum(w - 1, 0)]))
    def _():
        wg_bf[...] = wg_ref[0, 0].astype(BF16)
        wu_bf[...] = wu_ref[0, 0].astype(BF16)
        wd_bf[...] = wd_ref[0, 0].astype(BF16)

    @pl.when(hi > lo)
    def _():
        x = _load_packed_bf16(x_ref, tm)
        gate = _dot(x, wg_bf[...])
        up = _dot(x, wu_bf[...])
        hmid = (gate * jax.nn.sigmoid(gate) * up).astype(BF16)
        y = _dot(hmid, wd_bf[...])
        rows = tile_ref[w] * tm + lax.broadcasted_iota(I32, (tm, 1), 0)
        mine = (rows >= lo) & (rows < hi)
        _store_packed(y_ref, y, keep=mine, clear=first_ref[w] == 1)


def _experts(xs, wg, wu, wd, layer, tile_w, exp_w, lo_w, hi_w, first_w, tm):
    d = D_MODEL
    rows = pl.BlockSpec((tm * PACK_SUBLANES, LANES), lambda w, tl, ex, lo, hi, fi: (tl[w], 0))
    nw = tile_w.shape[0]
    grid_spec = pltpu.PrefetchScalarGridSpec(
        num_scalar_prefetch=5,
        grid=(nw,),
        in_specs=[
            rows,
            pl.BlockSpec((1, 1, d, D_EXPERT), lambda w, tl, ex, lo, hi, fi: (layer, ex[w], 0, 0)),
            pl.BlockSpec((1, 1, d, D_EXPERT), lambda w, tl, ex, lo, hi, fi: (layer, ex[w], 0, 0)),
            pl.BlockSpec((1, 1, D_EXPERT, d), lambda w, tl, ex, lo, hi, fi: (layer, ex[w], 0, 0)),
        ],
        out_specs=rows,
        scratch_shapes=[pltpu.VMEM((d, D_EXPERT), BF16), pltpu.VMEM((d, D_EXPERT), BF16),
                        pltpu.VMEM((D_EXPERT, d), BF16)],
    )
    return pl.pallas_call(
        _expert_kernel,
        grid_spec=grid_spec,
        out_shape=jax.ShapeDtypeStruct(xs.shape, I32),
        compiler_params=_params(("arbitrary",)),
        name="moe_experts",
    )(tile_w, exp_w, lo_w, hi_w, first_w, xs, wg, wu, wd)


COMBINE_HALF = 128


def _combine_kernel(dcur_ref, dnext_ref, ys_ref, gate_ref, h_ref, hp_ref, sg_ref, su_ref, sd_ref, g_ref, b_ref,
                    hf_out, hb_out, buf0, buf1, sem0, sem1):
    i = pl.program_id(0)
    th = buf0.shape[1] // PACK_SUBLANES

    def gather(dref, base, buf, sem):
        for r in range(th):
            for k in range(TOP_K):
                src = ys_ref.at[dref[(base + r) * TOP_K + k]]
                pltpu.make_async_copy(src, _token_tile(buf, r, (k,)), sem).start(priority=k % 2)

    def wait(sem):
        for k in range(TOP_K):
            pltpu.make_async_copy(ys_ref.at[pl.ds(0, th)], ys_ref.at[pl.ds(0, th)], sem).wait()

    def reduce_half(half, buf):
        rows = pl.ds(half * th, th)
        x = _load_packed_bf16(hp_ref, th, base=half * th)
        gate = _dot(x, sg_ref[...])
        up = _dot(x, su_ref[...])
        acc = _dot((gate * jax.nn.sigmoid(gate) * up).astype(BF16), sd_ref[...])
        acc = acc + DEEPNORM_ALPHA * h_ref[rows, :]
        gates = gate_ref[rows, :]
        los = [None] * PACK_SUBLANES
        his = [None] * PACK_SUBLANES
        for k in range(TOP_K):
            gk = gates[:, k:k + 1]
            for j in range(PACK_SUBLANES):
                lo, hi = _unpack_words(buf[_word_rows(j, th, (k,))])
                los[j] = gk * lo if k == 0 else los[j] + gk * lo
                his[j] = gk * hi if k == 0 else his[j] + gk * hi
        out = _layer_norm(acc + jnp.concatenate(los + his, axis=1), g_ref[...], b_ref[...])
        hf_out[rows, :] = out
        hb_out[rows, :] = out.astype(BF16)

    @pl.when(i == 0)
    def _():
        def body(r, carry):
            for k in range(TOP_K):
                src = ys_ref.at[dcur_ref[r * TOP_K + k]]
                pltpu.make_async_copy(src, _token_tile(buf0, r, (k,)), sem0).start(priority=k % 2)
            return carry

        lax.fori_loop(0, th, body, 0)

    wait(sem0)
    gather(dcur_ref, th, buf1, sem1)
    reduce_half(0, buf0)
    wait(sem1)
    gather(dnext_ref, 0, buf0, sem0)
    reduce_half(1, buf1)

    @pl.when(i == pl.num_programs(0) - 1)
    def _():
        wait(sem0)


def _combine(dest_flat, ys, gate, h, hp, sg, su, sd, g, b):
    t, d = h.shape
    th = min(COMBINE_HALF, t // 2)
    tm = 2 * th
    n = t // tm
    const = lambda i: (0, 0)
    row = lambda width: pl.BlockSpec((tm, width), lambda i: (i, 0))
    buf = pltpu.VMEM((TOP_K, th * PACK_SUBLANES, LANES), I32)
    return pl.pallas_call(
        _combine_kernel,
        grid=(n,),
        in_specs=[
            pl.BlockSpec((tm * TOP_K,), lambda i: (i,), memory_space=pltpu.SMEM),
            pl.BlockSpec((th * TOP_K,), lambda i: (jnp.minimum(2 * i + 2, 2 * n - 1),), memory_space=pltpu.SMEM),
            pl.BlockSpec(memory_space=pl.ANY),
            row(LANES), row(d), pl.BlockSpec((tm * PACK_SUBLANES, LANES), lambda i: (i, 0)),
            pl.BlockSpec(sg.shape, const), pl.BlockSpec(su.shape, const), pl.BlockSpec(sd.shape, const),
            pl.BlockSpec((1, d), const), pl.BlockSpec((1, d), const),
        ],
        out_specs=[row(d), row(d)],
        out_shape=[jax.ShapeDtypeStruct((t, d), F32), jax.ShapeDtypeStruct((t, d), BF16)],
        scratch_shapes=[buf, buf, pltpu.SemaphoreType.DMA(()), pltpu.SemaphoreType.DMA(())],
        compiler_params=_params(("arbitrary",)),
        name="moe_combine",
    )(dest_flat, dest_flat, ys, gate, h, hp, sg, su, sd, g, b)


EXPERT_TILE = 512


def _group_metadata(counts, idx, rank, n_assign):
    tm = min(EXPERT_TILE, n_assign)
    nt = n_assign // tm
    nw = nt + N_EXPERTS - 1
    sizes = counts.reshape(N_EXPERTS).astype(I32)
    ends = jnp.cumsum(sizes)
    starts = ends - sizes
    my_start = jnp.sum(jnp.where(idx[..., None] == jnp.arange(N_EXPERTS, dtype=I32), starts, 0), axis=-1)
    dest = (my_start + rank).reshape(-1).astype(I32)
    first_tile = starts // tm
    n_e = jnp.where(sizes > 0, (ends - 1) // tm - first_tile + 1, 0)
    cum = jnp.cumsum(n_e)
    off = cum - n_e
    total = cum[-1]
    w = jnp.arange(nw, dtype=I32)
    valid = w < total
    experts = jnp.arange(N_EXPERTS, dtype=I32)
    last_used = jnp.max(jnp.where(n_e > 0, experts, 0))
    e_w = jnp.where(valid, jnp.sum(cum[None, :] <= w[:, None], axis=1), last_used).astype(I32)
    pick = lambda table: jnp.sum(jnp.where(e_w[:, None] == experts, table, 0), axis=1)
    tile_w = jnp.where(valid, pick(first_tile) + (w - pick(off)), nt - 1).astype(I32)
    lo_w = jnp.where(valid, jnp.maximum(pick(starts), tile_w * tm), 0).astype(I32)
    hi_w = jnp.where(valid, jnp.minimum(pick(ends), (tile_w + 1) * tm), 0).astype(I32)
    first_w = jnp.concatenate([jnp.ones((1,), I32), (tile_w[1:] != tile_w[:-1]).astype(I32)])
    return dest, tile_w, e_w, lo_w, hi_w, first_w, tm


def _moe(h, hp, r_hi, r_lo, r_bias, wg, wu, wd, layer, sg, su, sd, g, b):
    t = h.shape[0]
    a = t * TOP_K
    idx, gate, rank, counts = _router(h, r_hi, r_lo, r_bias)
    dest, tile_w, e_w, lo_w, hi_w, first_w, tm = _group_metadata(counts, idx[:, :TOP_K], rank[:, :TOP_K], a)
    xs = _dispatch(hp, dest)
    ys = _experts(xs.reshape(a * PACK_SUBLANES, LANES), wg, wu, wd, layer, tile_w, e_w, lo_w, hi_w, first_w, tm)
    return _combine(dest, ys.reshape(a, PACK_SUBLANES, LANES), gate, h, hp, sg, su, sd, g, b)


def _even_weights(w_in, w_uq, w_ukv, w_out):
    o1 = MLA_Q_RANK
    o2 = o1 + MLA_KV_RANK
    o3 = o2 + MLA_ROPE
    o4 = o3 + 4 * DIFF_HEADS * DIFF_QK
    c_q, c_kv, k_r, dq, dv = w_in[:, :o1], w_in[:, o1:o2], w_in[:, o2:o3], w_in[:, o3:o4], w_in[:, o4:]
    dq = dq.reshape(D_MODEL, DIFF_HEADS, 4, DIFF_QK).transpose(0, 2, 1, 3).reshape(D_MODEL, -1)
    pad = jnp.zeros((D_MODEL, 512 - MLA_KV_RANK - MLA_ROPE), w_in.dtype)
    w_proj = jnp.concatenate([dq, dv, c_q, c_kv, k_r, pad], axis=1).astype(BF16)
    uq = w_uq.reshape(MLA_Q_RANK, MLA_HEADS, MLA_NOPE + MLA_ROPE)
    wqn = uq[:, :, :MLA_NOPE].reshape(MLA_Q_RANK, -1).astype(BF16)
    wqr = uq[:, :, MLA_NOPE:].reshape(MLA_Q_RANK, -1).astype(BF16)
    ukv = w_ukv.reshape(MLA_KV_RANK, MLA_HEADS, MLA_NOPE + MLA_V)
    wkn = ukv[:, :, :MLA_NOPE].reshape(MLA_KV_RANK, -1).astype(BF16)
    wv = ukv[:, :, MLA_NOPE:].reshape(MLA_KV_RANK, -1).astype(BF16)
    n_mla = MLA_HEADS * MLA_V
    return w_proj, wqn, wqr, wkn, wv, w_out[:n_mla].astype(BF16), w_out[n_mla:].astype(BF16)


def _odd_weights(w_in, w_out):
    nq = SWA_Q_HEADS * SWA_HD
    nkv = SWA_KV_HEADS * SWA_HD
    dup = lambda w: jnp.concatenate([w.reshape(D_MODEL, SWA_KV_HEADS, 1, SWA_HD)] * 2, axis=2).reshape(D_MODEL, -1)
    w_proj = jnp.concatenate([w_in[:, :nq], dup(w_in[:, nq:nq + nkv]), dup(w_in[:, nq + nkv:])], axis=1)
    return w_proj.astype(BF16), w_out.astype(BF16)


def kernel(x, positions, ab_w_in, mla_q_norm, mla_w_uq, mla_kv_norm, mla_w_ukv, diff_lambda_q1, diff_lambda_k1, diff_lambda_q2, diff_lambda_k2, diff_sub_norm, ab_w_out, swa_w_in, swa_sink, swa_w_out, mix_ln_g, mix_ln_b, ffn_ln_g, ffn_ln_b, router_w, router_bias, exp_w_gate, exp_w_up, exp_w_down, shared_w_gate, shared_w_up, shared_w_down):
    b, s, d = x.shape
    t = b * s
    tab = _rope_tables(positions)
    h = x.reshape(t, d)
    hb = h.astype(BF16)
    row = lambda v: v.reshape(1, -1)
    qk_scale = lambda n_q, n: jnp.where(jnp.arange(n) < n_q, DIFF_QK ** -0.5 * LOG2E, 1.0).astype(F32).reshape(1, n)
    for layer in range(DEPTH):
        i = layer // 2
        if layer % 2 == 0:
            w_proj, wqn, wqr, wkn, wv, wo_mla, wo_diff = _even_weights(
                ab_w_in[i], mla_w_uq[i], mla_w_ukv[i], ab_w_out[i])
            n_dq = 4 * DIFF_HEADS * DIFF_QK
            dqk = _matmul_rope(hb, w_proj[:, :n_dq], tab, qk_scale(n_dq // 2, n_dq), "even_in_proj_qk")
            proj = _matmul(hb, w_proj[:, n_dq:], "even_in_proj")
            q, k, v = _mla_prep(proj, tab, row(mla_q_norm[i]), row(mla_kv_norm[i]), wqn, wqr, wkn, wv,
                                b, s, col_cq=2, col_ckv=3)
            o_mla = _mla_attn(q, k, v).reshape(t, -1)
            lam_p = jnp.stack([diff_lambda_q1[i], diff_lambda_k1[i], diff_lambda_q2[i], diff_lambda_k2[i]])
            lam_init = 0.8 - 0.6 * math.exp(-0.3 * layer)
            o_diff = _diff_attn(dqk, proj, 0, lam_p, row(diff_sub_norm[i]), lam_init, b, s).reshape(t, -1)
            h, hp = _out_ln([o_mla, o_diff], [wo_mla, wo_diff], h, row(mix_ln_g[layer]), row(mix_ln_b[layer]))
        else:
            w_proj, wo = _odd_weights(swa_w_in[i], swa_w_out[i])
            nq = SWA_Q_HEADS * SWA_HD
            n_rot = nq + SWA_KV_HEADS * LANES
            qk = _matmul_rope(hb, w_proj[:, :n_rot], tab, qk_scale(nq, n_rot), "odd_in_proj_qk", tn=n_rot // 2)
            vdup = _matmul(hb, w_proj[:, n_rot:], "odd_in_proj")
            o = _swa_attn(qk, vdup, swa_sink[i], b, s, col_k=nq // LANES, col_v=0)
            h, hp = _out_ln([o], [wo], h, row(mix_ln_g[layer]), row(mix_ln_b[layer]))
        rw = router_w[layer]
        r_hi = rw.astype(BF16)
        r_lo = (rw - r_hi.astype(F32)).astype(BF16)
        h, hb = _moe(h, hp, r_hi, r_lo, row(router_bias[layer]),
                     exp_w_gate, exp_w_up, exp_w_down, layer,
                     shared_w_gate[layer].astype(BF16), shared_w_up[layer].astype(BF16),
                     shared_w_down[layer].astype(BF16), row(ffn_ln_g[layer]), row(ffn_ln_b[layer]))
    return h.reshape(b, s, d)
```

```python
import functools
import math

import numpy as np
import jax
import jax.numpy as jnp
from jax import lax
from jax.experimental import pallas as pl
from jax.experimental.pallas import tpu as pltpu

F32 = jnp.float32
BF16 = jnp.bfloat16
I32 = jnp.int32

D_MODEL = 2048
DEPTH = 4
ROPE_THETA = 500000.0
LN_EPS = 1e-5
RMS_EPS = 1e-6
MLA_HEADS = 8
MLA_NOPE = 128
MLA_ROPE = 64
MLA_V = 128
MLA_Q_RANK = 512
MLA_KV_RANK = 256
DIFF_HEADS = 8
DIFF_QK = 64
DIFF_V = 128
DIFF_ROT = DIFF_QK // 4
SWA_Q_HEADS = 32
SWA_KV_HEADS = 4
SWA_GROUP = SWA_Q_HEADS // SWA_KV_HEADS
SWA_HD = 64
WINDOW = 128
N_EXPERTS = 64
TOP_K = 8
D_EXPERT = 256
ROUTED_SCALE = 2.5
DEEPNORM_ALPHA = (2 * DEPTH) ** 0.25
LOG2E = math.log2(math.e)

LANES = 128
MLA_QK_PAD = 256
VMEM_LIMIT = 48 << 20
ATTN_TQ = 512
MLA_TQ = 1024
MLA_TK = 2048
ATTN_TK = 2048


def _params(sem):
    return pltpu.CompilerParams(dimension_semantics=sem, vmem_limit_bytes=VMEM_LIMIT)


def _dot(a, b):
    return jnp.dot(a, b, preferred_element_type=F32)


def _dot_nt(a, b):
    return lax.dot_general(a, b, (((1,), (1,)), ((), ())), preferred_element_type=F32)


def _tables_kernel(pos_ref, c_ref, out_ref):
    pos = pos_ref[...]
    a64 = pos * c_ref[0:1, :]
    a16 = pos * c_ref[3:4, :]
    s64 = jnp.sin(a64)
    s16 = jnp.sin(a16)
    out_ref[0] = jnp.cos(a64)
    out_ref[1] = s64 * c_ref[1:2, :]
    out_ref[2] = s64 * c_ref[2:3, :]
    out_ref[3] = jnp.cos(a16)
    out_ref[4] = s16 * c_ref[4:5, :]
    out_ref[5] = s16 * c_ref[5:6, :]


def _rope_consts():
    j = np.arange(LANES) % 64
    inv64 = ROPE_THETA ** (-jnp.arange(0, MLA_ROPE, 2, dtype=F32) / MLA_ROPE)
    inv16 = ROPE_THETA ** (-jnp.arange(0, DIFF_ROT, 2, dtype=F32) / DIFF_ROT)
    f64 = inv64[j % 32]
    f16 = jnp.where(j < 16, inv16[j % 8], 0.0)
    rows = [
        f64,
        jnp.asarray(np.where(j < 32, -1.0, 0.0), F32),
        jnp.asarray(np.where(j >= 32, 1.0, 0.0), F32),
        f16,
        jnp.asarray(np.where(j < 8, -1.0, 0.0), F32),
        jnp.asarray(np.where((j >= 8) & (j < 16), 1.0, 0.0), F32),
        jnp.zeros((LANES,), F32),
        jnp.zeros((LANES,), F32),
    ]
    return jnp.stack(rows).astype(F32)


def _rope_tables(positions):
    t = positions.size
    pos = jnp.broadcast_to(positions.reshape(t, 1).astype(F32), (t, LANES))
    tm = min(512, t)
    return pl.pallas_call(
        _tables_kernel,
        grid=(t // tm,),
        in_specs=[pl.BlockSpec((tm, LANES), lambda i: (i, 0)), pl.BlockSpec((8, LANES), lambda i: (0, 0))],
        out_specs=pl.BlockSpec((6, tm, LANES), lambda i: (0, i, 0)),
        out_shape=jax.ShapeDtypeStruct((6, t, LANES), F32),
        compiler_params=_params(("parallel",)),
        name="rope_tables",
    )(pos, _rope_consts())


def _rope_tile(x, c, sa, sb, half):
    return x * c + pltpu.roll(x, LANES - half, 1) * sa + pltpu.roll(x, half, 1) * sb


def _mm_kernel(x_ref, w_ref, o_ref):
    o_ref[...] = _dot(x_ref[...], w_ref[...]).astype(o_ref.dtype)


def _matmul(x, w, name, tm=512, tn=1024):
    m, k = x.shape
    n = w.shape[1]
    tm = min(tm, m)
    tn = min(tn, n)
    return pl.pallas_call(
        _mm_kernel,
        grid=(n // tn, m // tm),
        in_specs=[pl.BlockSpec((tm, k), lambda j, i: (i, 0)), pl.BlockSpec((k, tn), lambda j, i: (0, j))],
        out_specs=pl.BlockSpec((tm, tn), lambda j, i: (i, j)),
        out_shape=jax.ShapeDtypeStruct((m, n), BF16),
        compiler_params=_params(("parallel", "parallel")),
        name=name,
    )(x, w)


def _mm_rope_kernel(x_ref, w_ref, tab_ref, scale_ref, o_ref):
    acc = _dot(x_ref[...], w_ref[...])
    c, sa, sb = tab_ref[0], tab_ref[1], tab_ref[2]
    for j in range(o_ref.shape[1] // LANES):
        sl = slice(LANES * j, LANES * (j + 1))
        o_ref[:, sl] = (_rope_tile(acc[:, sl], c, sa, sb, DIFF_ROT // 2) * scale_ref[:, sl]).astype(o_ref.dtype)


def _matmul_rope(x, w, tab, scale, name, tm=512, tn=1024):
    m, k = x.shape
    n = w.shape[1]
    tm = min(tm, m)
    return pl.pallas_call(
        _mm_rope_kernel,
        grid=(n // tn, m // tm),
        in_specs=[
            pl.BlockSpec((tm, k), lambda j, i: (i, 0)),
            pl.BlockSpec((k, tn), lambda j, i: (0, j)),
            pl.BlockSpec((3, tm, LANES), lambda j, i: (1, i, 0)),
            pl.BlockSpec((1, tn), lambda j, i: (0, j)),
        ],
        out_specs=pl.BlockSpec((tm, tn), lambda j, i: (i, j)),
        out_shape=jax.ShapeDtypeStruct((m, n), BF16),
        compiler_params=_params(("parallel", "parallel")),
        name=name,
    )(x, w, tab, scale)


def _rms(x, g):
    return x * lax.rsqrt(jnp.mean(x * x, -1, keepdims=True) + RMS_EPS) * g


def _mla_prep_kernel(cq_ref, ckv_ref, tab_ref, qg_ref, kvg_ref, wqn_ref, wqr_ref, wkn_ref, wv_ref,
                     q_ref, k_ref, v_ref, *, scale):
    tm = cq_ref.shape[0]
    cq = cq_ref[...].astype(F32)
    ckv_all = ckv_ref[...].astype(F32)
    ckv = ckv_all[:, :MLA_KV_RANK]
    kr = ckv_all[:, MLA_KV_RANK:MLA_KV_RANK + LANES]
    cqn = _rms(cq, qg_ref[...]).astype(BF16)
    ckvn = _rms(ckv, kvg_ref[...]).astype(BF16)
    qn = _dot(cqn, wqn_ref[...]) * scale
    qr = _dot(cqn, wqr_ref[...]) * scale
    kn = _dot(ckvn, wkn_ref[...])
    vv = _dot(ckvn, wv_ref[...])
    c, sa, sb = tab_ref[0], tab_ref[1], tab_ref[2]
    lo = lax.broadcasted_iota(I32, (tm, LANES), 1) < MLA_ROPE
    half = MLA_ROPE // 2
    krr = jnp.where(lo, _rope_tile(kr, c, sa, sb, half), 0.0).astype(BF16)
    for j in range(MLA_HEADS // 2):
        r = _rope_tile(qr[:, LANES * j:LANES * (j + 1)], c, sa, sb, half)
        q_ref[0, 2 * j, :, LANES:] = jnp.where(lo, r, 0.0).astype(BF16)
        q_ref[0, 2 * j + 1, :, LANES:] = jnp.where(lo, pltpu.roll(r, MLA_ROPE, 1), 0.0).astype(BF16)
    for h in range(MLA_HEADS):
        sl = slice(LANES * h, LANES * (h + 1))
        q_ref[0, h, :, :LANES] = qn[:, sl].astype(BF16)
        k_ref[0, h, :, :LANES] = kn[:, sl].astype(BF16)
        k_ref[0, h, :, LANES:] = krr
        v_ref[0, h] = vv[:, sl].astype(BF16)


def _mla_prep(proj, tab, q_norm, kv_norm, wqn, wqr, wkn, wv, b, s, col_cq, col_ckv):
    tm = min(256, s)
    nt = s // tm
    h = MLA_HEADS
    row = lambda bi, i: bi * nt + i
    const = lambda bi, i: (0, 0)
    return pl.pallas_call(
        functools.partial(_mla_prep_kernel, scale=(MLA_NOPE + MLA_ROPE) ** -0.5 * LOG2E),
        grid=(b, nt),
        in_specs=[
            pl.BlockSpec((tm, MLA_Q_RANK), lambda bi, i: (row(bi, i), col_cq)),
            pl.BlockSpec((tm, 512), lambda bi, i: (row(bi, i), col_ckv)),
            pl.BlockSpec((3, tm, LANES), lambda bi, i: (0, row(bi, i), 0)),
            pl.BlockSpec((1, MLA_Q_RANK), const),
            pl.BlockSpec((1, MLA_KV_RANK), const),
            pl.BlockSpec(wqn.shape, const),
            pl.BlockSpec(wqr.shape, const),
            pl.BlockSpec(wkn.shape, const),
            pl.BlockSpec(wv.shape, const),
        ],
        out_specs=[
            pl.BlockSpec((1, h, tm, MLA_QK_PAD), lambda bi, i: (bi, 0, i, 0)),
            pl.BlockSpec((1, h, tm, MLA_QK_PAD), lambda bi, i: (bi, 0, i, 0)),
            pl.BlockSpec((1, h, tm, MLA_V), lambda bi, i: (bi, 0, i, 0)),
        ],
        out_shape=[
            jax.ShapeDtypeStruct((b, h, s, MLA_QK_PAD), BF16),
            jax.ShapeDtypeStruct((b, h, s, MLA_QK_PAD), BF16),
            jax.ShapeDtypeStruct((b, h, s, MLA_V), BF16),
        ],
        compiler_params=_params(("parallel", "parallel")),
        name="mla_prep",
    )(proj, proj, tab, q_norm, kv_norm, wqn, wqr, wkn, wv)


def _online_softmax_loop(qs, k_ats, v_at, nk):
    tq = qs[0].shape[0]

    def body(j, carry):
        v = v_at(j)
        out = []
        for q, k_at, (m, l, acc) in zip(qs, k_ats, carry):
            s = _dot_nt(q, k_at(j))
            m_new = jnp.maximum(m, jnp.max(s, axis=1, keepdims=True))
            alpha = jnp.exp2(m - m_new)
            p = jnp.exp2(s - m_new)
            l = alpha * l + jnp.sum(p, axis=1, keepdims=True)
            acc = alpha * acc + _dot(p.astype(BF16), v)
            out.append((m_new, l, acc))
        return tuple(out)

    one = (jnp.full((tq, 1), -jnp.inf, F32), jnp.zeros((tq, 1), F32), jnp.zeros((tq, v_at(0).shape[1]), F32))
    final = lax.fori_loop(0, nk, body, tuple(one for _ in qs), unroll=True)
    return [(l, acc) for _, l, acc in final]


def _mla_attn_kernel(q_ref, k_ref, v_ref, o_ref, *, tk):
    nk = k_ref.shape[2] // tk
    chunk = lambda j: pl.ds(pl.multiple_of(j * tk, tk), tk)
    (l, acc), = _online_softmax_loop(
        [q_ref[0, 0]], [lambda j: k_ref[0, 0, chunk(j), :]], lambda j: v_ref[0, 0, chunk(j), :], nk)
    o_ref[0] = (acc / l).astype(o_ref.dtype)


def _mla_attn(q, k, v):
    b, h, s, _ = q.shape
    tq = min(MLA_TQ, s)
    return pl.pallas_call(
        functools.partial(_mla_attn_kernel, tk=min(MLA_TK, s)),
        grid=(b, h, s // tq),
        in_specs=[
            pl.BlockSpec((1, 1, tq, MLA_QK_PAD), lambda bi, hi, i: (bi, hi, i, 0)),
            pl.BlockSpec((1, 1, s, MLA_QK_PAD), lambda bi, hi, i: (bi, hi, 0, 0)),
            pl.BlockSpec((1, 1, s, MLA_V), lambda bi, hi, i: (bi, hi, 0, 0)),
        ],
        out_specs=pl.BlockSpec((1, tq, MLA_V), lambda bi, hi, i: (bi, i, hi)),
        out_shape=jax.ShapeDtypeStruct((b, s, h * MLA_V), BF16),
        compiler_params=_params(("parallel", "parallel", "arbitrary")),
        name="mla_attn",
    )(q, k, v)


def _diff_attn_kernel(q1_ref, q2_ref, k1_ref, k2_ref, v_ref, lam_ref, g_ref, o_ref, *, tk, lam_init):
    hi = pl.program_id(1)
    tq = q1_ref.shape[0]
    nk = v_ref.shape[0] // tk
    chunk = lambda j: pl.ds(pl.multiple_of(j * tk, tk), tk)
    mine = (lax.broadcasted_iota(I32, (tq, LANES), 1) // DIFF_QK) == (hi % 2)
    zero = jnp.zeros((), BF16)
    qs = [jnp.where(mine, q1_ref[...], zero), jnp.where(mine, q2_ref[...], zero)]
    k_ats = [lambda j: k1_ref[chunk(j), :], lambda j: k2_ref[chunk(j), :]]
    (l1, a1), (l2, a2) = _online_softmax_loop(qs, k_ats, lambda j: v_ref[chunk(j), :], nk)
    lp = lam_ref[...]
    lam = (jnp.exp(jnp.sum(lp[0:1] * lp[1:2], axis=1, keepdims=True))
           - jnp.exp(jnp.sum(lp[2:3] * lp[3:4], axis=1, keepdims=True)) + lam_init)
    o = a1 / l1 - lam * (a2 / l2)
    o_ref[0] = (_rms(o, g_ref[...]) * (1.0 - lam_init)).astype(o_ref.dtype)


def _diff_attn(dqk, proj, col_v, lam_p, sub_norm, lam_init, b, s):
    h = DIFF_HEADS
    tq = min(ATTN_TQ, s)
    nq = s // tq
    pairs = h // 2
    qspec = lambda which: pl.BlockSpec((tq, LANES), lambda bi, hi, i: (bi * nq + i, which * pairs + hi // 2))
    kspec = lambda which: pl.BlockSpec((s, LANES), lambda bi, hi, i: (bi, which * pairs + hi // 2))
    const = lambda bi, hi, i: (0, 0)
    return pl.pallas_call(
        functools.partial(_diff_attn_kernel, tk=min(ATTN_TK, s), lam_init=lam_init),
        grid=(b, h, s // tq),
        in_specs=[
            qspec(0), qspec(1), kspec(2), kspec(3),
            pl.BlockSpec((s, DIFF_V), lambda bi, hi, i: (bi, col_v + hi)),
            pl.BlockSpec((4, DIFF_QK), const),
            pl.BlockSpec((1, DIFF_V), const),
        ],
        out_specs=pl.BlockSpec((1, tq, DIFF_V), lambda bi, hi, i: (bi, i, hi)),
        out_shape=jax.ShapeDtypeStruct((b, s, h * DIFF_V), BF16),
        compiler_params=_params(("parallel", "parallel", "arbitrary")),
        name="diff_attn",
    )(dqk, dqk, dqk, dqk, proj, lam_p, sub_norm)


def _swa_attn_kernel(sink_ref, bias_ref, q_ref, kp_ref, km_ref, kn_ref, vp_ref, vm_ref, vn_ref, o_ref):
    g = pl.program_id(1)
    tq = q_ref.shape[0]
    k = jnp.concatenate([kp_ref[...], km_ref[...], kn_ref[...]], axis=0)
    v = jnp.concatenate([vp_ref[...], vm_ref[...], vn_ref[...]], axis=0)
    bias = bias_ref[0]
    lane = lax.broadcasted_iota(I32, (tq, LANES), 1)
    zero = jnp.zeros((), BF16)
    qs = []
    for h in range(SWA_GROUP):
        qp = q_ref[:, LANES * (h // 2):LANES * (h // 2 + 1)]
        qs.append(jnp.where((lane // SWA_HD) == h % 2, qp, zero))
    s_all = _dot_nt(jnp.concatenate(qs, axis=0), k)
    ps, denoms = [], []
    for h in range(SWA_GROUP):
        sink = sink_ref[g * SWA_GROUP + h] * LOG2E
        sc = s_all[h * tq:(h + 1) * tq] + bias
        m = jnp.maximum(jnp.max(sc, axis=1, keepdims=True), sink)
        p = jnp.exp2(sc - m)
        denoms.append(jnp.sum(p, axis=1, keepdims=True) + jnp.exp2(sink - m))
        ps.append(p.astype(BF16))
    o_all = _dot(jnp.concatenate(ps, axis=0), v)
    for pair in range(SWA_GROUP // 2):
        lo = o_all[2 * pair * tq:(2 * pair + 1) * tq] / denoms[2 * pair]
        hi = o_all[(2 * pair + 1) * tq:(2 * pair + 2) * tq] / denoms[2 * pair + 1]
        o_ref[:, LANES * pair:LANES * (pair + 1)] = jnp.where(lane < SWA_HD, lo, hi).astype(o_ref.dtype)


def _swa_bias(tq):
    qi = np.arange(tq)[:, None]
    kj = np.arange(tq + 2 * WINDOW)[None, :] - WINDOW
    band = np.abs(kj - qi) <= WINDOW
    masks = [band, band & (kj >= 0), band & (kj < tq), band & (kj >= 0) & (kj < tq)]
    return jnp.asarray(np.where(np.stack(masks), 0.0, -np.inf), F32)


def _swa_attn(qk, proj, sink, b, s, col_k, col_v, tq=256):
    tq = min(tq, s)
    nt = s // tq
    r = tq // WINDOW
    nwb = s // WINDOW
    gw = SWA_GROUP * SWA_HD
    row = lambda bi, i: bi * nt + i
    prev = lambda bi, i: bi * nwb + jnp.maximum(i * r - 1, 0)
    nxt = lambda bi, i: bi * nwb + jnp.minimum((i + 1) * r, nwb - 1)
    edge = lambda i: jnp.where(i == 0, 1, 0) + jnp.where(i == nt - 1, 2, 0)
    grid_spec = pltpu.PrefetchScalarGridSpec(
        num_scalar_prefetch=1,
        grid=(b, SWA_KV_HEADS, nt),
        in_specs=[
            pl.BlockSpec((1, tq, tq + 2 * WINDOW), lambda bi, g, i, sk: (edge(i), 0, 0)),
            pl.BlockSpec((tq, gw), lambda bi, g, i, sk: (row(bi, i), g)),
            pl.BlockSpec((WINDOW, LANES), lambda bi, g, i, sk: (prev(bi, i), col_k + g)),
            pl.BlockSpec((tq, LANES), lambda bi, g, i, sk: (row(bi, i), col_k + g)),
            pl.BlockSpec((WINDOW, LANES), lambda bi, g, i, sk: (nxt(bi, i), col_k + g)),
            pl.BlockSpec((WINDOW, LANES), lambda bi, g, i, sk: (prev(bi, i), col_v + g)),
            pl.BlockSpec((tq, LANES), lambda bi, g, i, sk: (row(bi, i), col_v + g)),
            pl.BlockSpec((WINDOW, LANES), lambda bi, g, i, sk: (nxt(bi, i), col_v + g)),
        ],
        out_specs=pl.BlockSpec((tq, gw), lambda bi, g, i, sk: (row(bi, i), g)),
    )
    return pl.pallas_call(
        _swa_attn_kernel,
        grid_spec=grid_spec,
        out_shape=jax.ShapeDtypeStruct((b * s, SWA_Q_HEADS * SWA_HD), BF16),
        compiler_params=_params(("parallel", "parallel", "parallel")),
        name="swa_attn",
    )(sink, _swa_bias(tq), qk, qk, qk, qk, proj, proj, proj)


def _layer_norm(y, g, b):
    mu = jnp.mean(y, -1, keepdims=True)
    d = y - mu
    var = jnp.mean(d * d, -1, keepdims=True)
    return d * lax.rsqrt(var + LN_EPS) * g + b


PACK_SUBLANES = D_MODEL // 2 // LANES
HI_MASK = -65536


def _pack_words(lo, hi):
    lb = lax.bitcast_convert_type(lo.astype(BF16).astype(F32), I32)
    hb = lax.bitcast_convert_type(hi.astype(BF16).astype(F32), I32)
    return (hb & HI_MASK) | lax.shift_right_logical(lb, 16)


def _unpack_words(w):
    return lax.bitcast_convert_type(w << 16, F32), lax.bitcast_convert_type(w & HI_MASK, F32)


def _word_rows(j, tm, lead=(), base=0):
    return lead + (pl.ds(base * PACK_SUBLANES + j, tm, stride=PACK_SUBLANES), slice(None))


def _store_packed(ref, y, keep=None, clear=None):
    tm, d = y.shape
    for j in range(PACK_SUBLANES):
        w = _pack_words(y[:, LANES * j:LANES * (j + 1)], y[:, d // 2 + LANES * j:d // 2 + LANES * (j + 1)])
        if keep is not None:
            w = jnp.where(keep, w, jnp.where(clear, 0, ref[_word_rows(j, tm)]))
        ref[_word_rows(j, tm)] = w


def _load_packed_bf16(ref, tm, base=0):
    los, his = [], []
    for j in range(PACK_SUBLANES):
        lo, hi = _unpack_words(ref[_word_rows(j, tm, base=base)])
        los.append(lo.astype(BF16))
        his.append(hi.astype(BF16))
    return jnp.concatenate(los + his, axis=1)


def _out_ln_kernel(*refs, n_in):
    o_refs = refs[:n_in]
    w_refs = refs[n_in:2 * n_in]
    h_ref, g_ref, b_ref, hf_ref, hp_ref = refs[2 * n_in:]
    mix = _dot(o_refs[0][...], w_refs[0][...])
    for o_ref, w_ref in zip(o_refs[1:], w_refs[1:]):
        mix = mix + _dot(o_ref[...], w_ref[...])
    out = _layer_norm(DEEPNORM_ALPHA * h_ref[...] + mix, g_ref[...], b_ref[...])
    hf_ref[...] = out
    _store_packed(hp_ref, out)


def _out_ln(os_, ws, h, g, b):
    t, d = h.shape
    tm = min(256, t)
    n_in = len(os_)
    const = lambda i: (0, 0)
    rowblk = lambda a: pl.BlockSpec((tm, a.shape[1]), lambda i: (i, 0))
    return pl.pallas_call(
        functools.partial(_out_ln_kernel, n_in=n_in),
        grid=(t // tm,),
        in_specs=[rowblk(o) for o in os_] + [pl.BlockSpec(w.shape, const) for w in ws]
        + [rowblk(h), pl.BlockSpec((1, d), const), pl.BlockSpec((1, d), const)],
        out_specs=[pl.BlockSpec((tm, d), lambda i: (i, 0)),
                   pl.BlockSpec((tm * PACK_SUBLANES, LANES), lambda i: (i, 0))],
        out_shape=[jax.ShapeDtypeStruct((t, d), F32), jax.ShapeDtypeStruct((t * PACK_SUBLANES, LANES), I32)],
        compiler_params=_params(("parallel",)),
        name="out_proj_ln",
    )(*os_, *ws, h, g, b)


def _router_kernel(h_ref, wh_ref, wl_ref, bias_ref, idx_ref, gate_ref, rank_ref, cnt_ref, cnt_scr):
    i = pl.program_id(0)
    tm = h_ref.shape[0]

    @pl.when(i == 0)
    def _():
        cnt_scr[...] = jnp.zeros(cnt_scr.shape, F32)

    h = h_ref[...]
    hh = h.astype(BF16)
    hl = (h - hh.astype(F32)).astype(BF16)
    logits = _dot(hh, wh_ref[...]) + (_dot(hh, wl_ref[...]) + _dot(hl, wh_ref[...]))
    scores = jax.nn.sigmoid(logits.T[:N_EXPERTS])
    sel = scores + bias_ref[...]
    e_iota = lax.broadcasted_iota(I32, (N_EXPERTS, tm), 0).astype(F32)
    k_iota = lax.broadcasted_iota(I32, (TOP_K, tm), 0)
    onehots, gates = [], []
    idx_out = jnp.zeros((TOP_K, tm), F32)
    for k in range(TOP_K):
        m = jnp.max(sel, axis=0, keepdims=True)
        pick = jnp.min(jnp.where(sel == m, e_iota, float(N_EXPERTS)), axis=0, keepdims=True)
        oh = e_iota == pick
        onehots.append(oh)
        gates.append(jnp.sum(jnp.where(oh, scores, 0.0), axis=0, keepdims=True))
        idx_out = jnp.where(k_iota == k, pick, idx_out)
        sel = jnp.where(oh, -jnp.inf, sel)
    maskf = onehots[0].astype(F32)
    for oh in onehots[1:]:
        maskf = maskf + oh.astype(F32)
    tri = (lax.broadcasted_iota(I32, (tm, tm), 0) < lax.broadcasted_iota(I32, (tm, tm), 1)).astype(BF16)
    ranks = cnt_scr[...] + _dot(maskf.astype(BF16), tri)
    gsum = gates[0]
    for gk in gates[1:]:
        gsum = gsum + gk
    gate_out = jnp.zeros((TOP_K, tm), F32)
    rank_out = jnp.zeros((TOP_K, tm), F32)
    for k in range(TOP_K):
        gate_out = jnp.where(k_iota == k, gates[k] / gsum * ROUTED_SCALE, gate_out)
        rk = jnp.sum(jnp.where(onehots[k], ranks, 0.0), axis=0, keepdims=True)
        rank_out = jnp.where(k_iota == k, rk, rank_out)
    idx_ref[...] = idx_out.astype(I32)
    gate_ref[...] = gate_out
    rank_ref[...] = rank_out.astype(I32)
    total = cnt_scr[...] + jnp.sum(maskf, axis=1, keepdims=True)
    cnt_scr[...] = total
    cnt_ref[...] = total


def _router(h, w_hi, w_lo, bias):
    t, d = h.shape
    tm = min(256, t)
    const = lambda i: (0, 0)
    out = lambda: pl.BlockSpec((TOP_K, tm), lambda i: (0, i))
    return pl.pallas_call(
        _router_kernel,
        grid=(t // tm,),
        in_specs=[
            pl.BlockSpec((tm, d), lambda i: (i, 0)),
            pl.BlockSpec((d, LANES), const),
            pl.BlockSpec((d, LANES), const),
            pl.BlockSpec((N_EXPERTS, 1), const),
        ],
        out_specs=[out(), out(), out(), pl.BlockSpec((N_EXPERTS, 1), const)],
        out_shape=[
            jax.ShapeDtypeStruct((TOP_K, t), I32),
            jax.ShapeDtypeStruct((TOP_K, t), F32),
            jax.ShapeDtypeStruct((TOP_K, t), I32),
            jax.ShapeDtypeStruct((N_EXPERTS, 1), F32),
        ],
        scratch_shapes=[pltpu.VMEM((N_EXPERTS, 1), F32)],
        compiler_params=_params(("arbitrary",)),
        name="router",
    )(h, w_hi, w_lo, bias)


def _token_tile(ref, r, lead=()):
    return ref.at[lead + (pl.ds(pl.multiple_of(r * PACK_SUBLANES, PACK_SUBLANES), PACK_SUBLANES),)]


def _dispatch_kernel(dest_ref, x_ref, xs_ref, sem):
    tm = x_ref.shape[0] // PACK_SUBLANES

    def body(r, carry):
        src = _token_tile(x_ref, r)
        for k in range(TOP_K):
            pltpu.make_async_copy(src, xs_ref.at[dest_ref[r * TOP_K + k]], sem).start(priority=k % 2)
        return carry

    lax.fori_loop(0, tm, body, 0)
    for k in range(TOP_K):
        pltpu.make_async_copy(xs_ref.at[pl.ds(0, tm)], xs_ref.at[pl.ds(0, tm)], sem).wait()


def _dispatch(hp, dest_flat):
    t = hp.shape[0] // PACK_SUBLANES
    tm = min(256, t)
    return pl.pallas_call(
        _dispatch_kernel,
        grid=(t // tm,),
        in_specs=[
            pl.BlockSpec((tm * TOP_K,), lambda i: (i,), memory_space=pltpu.SMEM),
            pl.BlockSpec((tm * PACK_SUBLANES, LANES), lambda i: (i, 0)),
        ],
        out_specs=pl.BlockSpec(memory_space=pl.ANY),
        out_shape=jax.ShapeDtypeStruct((t * TOP_K, PACK_SUBLANES, LANES), I32),
        scratch_shapes=[pltpu.SemaphoreType.DMA(())],
        compiler_params=_params(("arbitrary",)),
        name="moe_dispatch",
    )(dest_flat, hp)


def _expert_kernel(tile_ref, exp_ref, lo_ref, hi_ref, first_ref, x_ref, wg_ref, wu_ref, wd_ref, y_ref,
                   wg_bf, wu_bf, wd_bf):
    w = pl.program_id(0)
    tm = x_ref.shape[0] // PACK_SUBLANES
    lo = lo_ref[w]
    hi = hi_ref[w]

    @pl.when((w == 0) | (exp_ref[w] != exp_ref[jnp.maximum(w - 1, 0)]))
    def _():
        wg_bf[...] = wg_ref[0, 0].astype(BF16)
        wu_bf[...] = wu_ref[0, 0].astype(BF16)
        wd_bf[...] = wd_ref[0, 0].astype(BF16)

    @pl.when(hi > lo)
    def _():
        x = _load_packed_bf16(x_ref, tm)
        gate = _dot(x, wg_bf[...])
        up = _dot(x, wu_bf[...])
        hmid = (gate * jax.nn.sigmoid(gate) * up).astype(BF16)
        y = _dot(hmid, wd_bf[...])
        rows = tile_ref[w] * tm + lax.broadcasted_iota(I32, (tm, 1), 0)
        mine = (rows >= lo) & (rows < hi)
        _store_packed(y_ref, y, keep=mine, clear=first_ref[w] == 1)


def _experts(xs, wg, wu, wd, layer, tile_w, exp_w, lo_w, hi_w, first_w, tm):
    d = D_MODEL
    rows = pl.BlockSpec((tm * PACK_SUBLANES, LANES), lambda w, tl, ex, lo, hi, fi: (tl[w], 0))
    nw = tile_w.shape[0]
    grid_spec = pltpu.PrefetchScalarGridSpec(
        num_scalar_prefetch=5,
        grid=(nw,),
        in_specs=[
            rows,
            pl.BlockSpec((1, 1, d, D_EXPERT), lambda w, tl, ex, lo, hi, fi: (layer, ex[w], 0, 0)),
            pl.BlockSpec((1, 1, d, D_EXPERT), lambda w, tl, ex, lo, hi, fi: (layer, ex[w], 0, 0)),
            pl.BlockSpec((1, 1, D_EXPERT, d), lambda w, tl, ex, lo, hi, fi: (layer, ex[w], 0, 0)),
        ],
        out_specs=rows,
        scratch_shapes=[pltpu.VMEM((d, D_EXPERT), BF16), pltpu.VMEM((d, D_EXPERT), BF16),
                        pltpu.VMEM((D_EXPERT, d), BF16)],
    )
    return pl.pallas_call(
        _expert_kernel,
        grid_spec=grid_spec,
        out_shape=jax.ShapeDtypeStruct(xs.shape, I32),
        compiler_params=_params(("arbitrary",)),
        name="moe_experts",
    )(tile_w, exp_w, lo_w, hi_w, first_w, xs, wg, wu, wd)


COMBINE_HALF = 128


def _combine_kernel(dcur_ref, dnext_ref, ys_ref, gate_ref, h_ref, hp_ref, sg_ref, su_ref, sd_ref, g_ref, b_ref,
                    hf_out, hb_out, buf0, buf1, sem0, sem1):
    i = pl.program_id(0)
    th = buf0.shape[1] // PACK_SUBLANES

    def gather(dref, base, buf, sem):
        for r in range(th):
            for k in range(TOP_K):
                src = ys_ref.at[dref[(base + r) * TOP_K + k]]
                pltpu.make_async_copy(src, _token_tile(buf, r, (k,)), sem).start(priority=k % 2)

    def wait(sem):
        for k in range(TOP_K):
            pltpu.make_async_copy(ys_ref.at[pl.ds(0, th)], ys_ref.at[pl.ds(0, th)], sem).wait()

    def reduce_half(half, buf):
        rows = pl.ds(half * th, th)
        x = _load_packed_bf16(hp_ref, th, base=half * th)
        gate = _dot(x, sg_ref[...])
        up = _dot(x, su_ref[...])
        acc = _dot((gate * jax.nn.sigmoid(gate) * up).astype(BF16), sd_ref[...])
        acc = acc + DEEPNORM_ALPHA * h_ref[rows, :]
        gates = gate_ref[rows, :]
        los = [None] * PACK_SUBLANES
        his = [None] * PACK_SUBLANES
        for k in range(TOP_K):
            gk = gates[:, k:k + 1]
            for j in range(PACK_SUBLANES):
                lo, hi = _unpack_words(buf[_word_rows(j, th, (k,))])
                los[j] = gk * lo if k == 0 else los[j] + gk * lo
                his[j] = gk * hi if k == 0 else his[j] + gk * hi
        out = _layer_norm(acc + jnp.concatenate(los + his, axis=1), g_ref[...], b_ref[...])
        hf_out[rows, :] = out
        hb_out[rows, :] = out.astype(BF16)

    @pl.when(i == 0)
    def _():
        def body(r, carry):
            for k in range(TOP_K):
                src = ys_ref.at[dcur_ref[r * TOP_K + k]]
                pltpu.make_async_copy(src, _token_tile(buf0, r, (k,)), sem0).start(priority=k % 2)
            return carry

        lax.fori_loop(0, th, body, 0)

    wait(sem0)
    gather(dcur_ref, th, buf1, sem1)
    reduce_half(0, buf0)
    wait(sem1)
    gather(dnext_ref, 0, buf0, sem0)
    reduce_half(1, buf1)

    @pl.when(i == pl.num_programs(0) - 1)
    def _():
        wait(sem0)


def _combine(dest_flat, ys, gate, h, hp, sg, su, sd, g, b):
    t, d = h.shape
    th = min(COMBINE_HALF, t // 2)
    tm = 2 * th
    n = t // tm
    const = lambda i: (0, 0)
    row = lambda width: pl.BlockSpec((tm, width), lambda i: (i, 0))
    buf = pltpu.VMEM((TOP_K, th * PACK_SUBLANES, LANES), I32)
    return pl.pallas_call(
        _combine_kernel,
        grid=(n,),
        in_specs=[
            pl.BlockSpec((tm * TOP_K,), lambda i: (i,), memory_space=pltpu.SMEM),
            pl.BlockSpec((th * TOP_K,), lambda i: (jnp.minimum(2 * i + 2, 2 * n - 1),), memory_space=pltpu.SMEM),
            pl.BlockSpec(memory_space=pl.ANY),
            row(TOP_K), row(d), pl.BlockSpec((tm * PACK_SUBLANES, LANES), lambda i: (i, 0)),
            pl.BlockSpec(sg.shape, const), pl.BlockSpec(su.shape, const), pl.BlockSpec(sd.shape, const),
            pl.BlockSpec((1, d), const), pl.BlockSpec((1, d), const),
        ],
        out_specs=[row(d), row(d)],
        out_shape=[jax.ShapeDtypeStruct((t, d), F32), jax.ShapeDtypeStruct((t, d), BF16)],
        scratch_shapes=[buf, buf, pltpu.SemaphoreType.DMA(()), pltpu.SemaphoreType.DMA(())],
        compiler_params=_params(("arbitrary",)),
        name="moe_combine",
    )(dest_flat, dest_flat, ys, gate, h, hp, sg, su, sd, g, b)


EXPERT_TILE = 512


def _group_metadata(counts, idx, rank, n_assign):
    tm = min(EXPERT_TILE, n_assign)
    nt = n_assign // tm
    nw = nt + N_EXPERTS - 1
    sizes = counts.reshape(N_EXPERTS).astype(I32)
    ends = jnp.cumsum(sizes)
    starts = ends - sizes
    my_start = jnp.sum(jnp.where(idx[..., None] == jnp.arange(N_EXPERTS, dtype=I32), starts, 0), axis=-1)
    dest = (my_start + rank).reshape(-1).astype(I32)
    first_tile = starts // tm
    n_e = jnp.where(sizes > 0, (ends - 1) // tm - first_tile + 1, 0)
    cum = jnp.cumsum(n_e)
    off = cum - n_e
    total = cum[-1]
    w = jnp.arange(nw, dtype=I32)
    valid = w < total
    experts = jnp.arange(N_EXPERTS, dtype=I32)
    last_used = jnp.max(jnp.where(n_e > 0, experts, 0))
    e_w = jnp.where(valid, jnp.sum(cum[None, :] <= w[:, None], axis=1), last_used).astype(I32)
    pick = lambda table: jnp.sum(jnp.where(e_w[:, None] == experts, table, 0), axis=1)
    tile_w = jnp.where(valid, pick(first_tile) + (w - pick(off)), nt - 1).astype(I32)
    lo_w = jnp.where(valid, jnp.maximum(pick(starts), tile_w * tm), 0).astype(I32)
    hi_w = jnp.where(valid, jnp.minimum(pick(ends), (tile_w + 1) * tm), 0).astype(I32)
    first_w = jnp.concatenate([jnp.ones((1,), I32), (tile_w[1:] != tile_w[:-1]).astype(I32)])
    return dest, tile_w, e_w, lo_w, hi_w, first_w, tm


def _moe(h, hp, r_hi, r_lo, r_bias, wg, wu, wd, layer, sg, su, sd, g, b):
    t = h.shape[0]
    a = t * TOP_K
    idx, gate, rank, counts = _router(h, r_hi, r_lo, r_bias)
    dest, tile_w, e_w, lo_w, hi_w, first_w, tm = _group_metadata(counts, idx.T, rank.T, a)
    gate = gate.T
    xs = _dispatch(hp, dest)
    ys = _experts(xs.reshape(a * PACK_SUBLANES, LANES), wg, wu, wd, layer, tile_w, e_w, lo_w, hi_w, first_w, tm)
    return _combine(dest, ys.reshape(a, PACK_SUBLANES, LANES), gate, h, hp, sg, su, sd, g, b)


def _even_weights(w_in, w_uq, w_ukv, w_out):
    o1 = MLA_Q_RANK
    o2 = o1 + MLA_KV_RANK
    o3 = o2 + MLA_ROPE
    o4 = o3 + 4 * DIFF_HEADS * DIFF_QK
    c_q, c_kv, k_r, dq, dv = w_in[:, :o1], w_in[:, o1:o2], w_in[:, o2:o3], w_in[:, o3:o4], w_in[:, o4:]
    dq = dq.reshape(D_MODEL, DIFF_HEADS, 4, DIFF_QK).transpose(0, 2, 1, 3).reshape(D_MODEL, -1)
    pad = jnp.zeros((D_MODEL, 512 - MLA_KV_RANK - MLA_ROPE), w_in.dtype)
    w_proj = jnp.concatenate([dq, dv, c_q, c_kv, k_r, pad], axis=1).astype(BF16)
    uq = w_uq.reshape(MLA_Q_RANK, MLA_HEADS, MLA_NOPE + MLA_ROPE)
    wqn = uq[:, :, :MLA_NOPE].reshape(MLA_Q_RANK, -1).astype(BF16)
    wqr = uq[:, :, MLA_NOPE:].reshape(MLA_Q_RANK, -1).astype(BF16)
    ukv = w_ukv.reshape(MLA_KV_RANK, MLA_HEADS, MLA_NOPE + MLA_V)
    wkn = ukv[:, :, :MLA_NOPE].reshape(MLA_KV_RANK, -1).astype(BF16)
    wv = ukv[:, :, MLA_NOPE:].reshape(MLA_KV_RANK, -1).astype(BF16)
    n_mla = MLA_HEADS * MLA_V
    return w_proj, wqn, wqr, wkn, wv, w_out[:n_mla].astype(BF16), w_out[n_mla:].astype(BF16)


def _odd_weights(w_in, w_out):
    nq = SWA_Q_HEADS * SWA_HD
    nkv = SWA_KV_HEADS * SWA_HD
    dup = lambda w: jnp.concatenate([w.reshape(D_MODEL, SWA_KV_HEADS, 1, SWA_HD)] * 2, axis=2).reshape(D_MODEL, -1)
    w_proj = jnp.concatenate([w_in[:, :nq], dup(w_in[:, nq:nq + nkv]), dup(w_in[:, nq + nkv:])], axis=1)
    return w_proj.astype(BF16), w_out.astype(BF16)


def kernel(x, positions, ab_w_in, mla_q_norm, mla_w_uq, mla_kv_norm, mla_w_ukv, diff_lambda_q1, diff_lambda_k1, diff_lambda_q2, diff_lambda_k2, diff_sub_norm, ab_w_out, swa_w_in, swa_sink, swa_w_out, mix_ln_g, mix_ln_b, ffn_ln_g, ffn_ln_b, router_w, router_bias, exp_w_gate, exp_w_up, exp_w_down, shared_w_gate, shared_w_up, shared_w_down):
    b, s, d = x.shape
    t = b * s
    tab = _rope_tables(positions)
    h = x.reshape(t, d)
    hb = h.astype(BF16)
    row = lambda v: v.reshape(1, -1)
    qk_scale = lambda n_q, n: jnp.where(jnp.arange(n) < n_q, DIFF_QK ** -0.5 * LOG2E, 1.0).astype(F32).reshape(1, n)
    for layer in range(DEPTH):
        i = layer // 2
        if layer % 2 == 0:
            w_proj, wqn, wqr, wkn, wv, wo_mla, wo_diff = _even_weights(
                ab_w_in[i], mla_w_uq[i], mla_w_ukv[i], ab_w_out[i])
            n_dq = 4 * DIFF_HEADS * DIFF_QK
            dqk = _matmul_rope(hb, w_proj[:, :n_dq], tab, qk_scale(n_dq // 2, n_dq), "even_in_proj_qk")
            proj = _matmul(hb, w_proj[:, n_dq:], "even_in_proj")
            q, k, v = _mla_prep(proj, tab, row(mla_q_norm[i]), row(mla_kv_norm[i]), wqn, wqr, wkn, wv,
                                b, s, col_cq=2, col_ckv=3)
            o_mla = _mla_attn(q, k, v).reshape(t, -1)
            lam_p = jnp.stack([diff_lambda_q1[i], diff_lambda_k1[i], diff_lambda_q2[i], diff_lambda_k2[i]])
            lam_init = 0.8 - 0.6 * math.exp(-0.3 * layer)
            o_diff = _diff_attn(dqk, proj, 0, lam_p, row(diff_sub_norm[i]), lam_init, b, s).reshape(t, -1)
            h, hp = _out_ln([o_mla, o_diff], [wo_mla, wo_diff], h, row(mix_ln_g[layer]), row(mix_ln_b[layer]))
        else:
            w_proj, wo = _odd_weights(swa_w_in[i], swa_w_out[i])
            nq = SWA_Q_HEADS * SWA_HD
            n_rot = nq + SWA_KV_HEADS * LANES
            qk = _matmul_rope(hb, w_proj[:, :n_rot], tab, qk_scale(nq, n_rot), "odd_in_proj_qk", tn=n_rot // 2)
            vdup = _matmul(hb, w_proj[:, n_rot:], "odd_in_proj")
            o = _swa_attn(qk, vdup, swa_sink[i], b, s, col_k=nq // LANES, col_v=0)
            h, hp = _out_ln([o], [wo], h, row(mix_ln_g[layer]), row(mix_ln_b[layer]))
        rw = jnp.pad(router_w[layer], ((0, 0), (0, LANES - N_EXPERTS)))
        r_hi = rw.astype(BF16)
        r_lo = (rw - r_hi.astype(F32)).astype(BF16)
        h, hb = _moe(h, hp, r_hi, r_lo, router_bias[layer].reshape(N_EXPERTS, 1),
                     exp_w_gate, exp_w_up, exp_w_down, layer,
                     shared_w_gate[layer].astype(BF16), shared_w_up[layer].astype(BF16),
                     shared_w_down[layer].astype(BF16), row(ffn_ln_g[layer]), row(ffn_ln_b[layer]))
    return h.reshape(b, s, d)
```

```python
import functools
import math

import numpy as np
import jax
import jax.numpy as jnp
from jax import lax
from jax.experimental import pallas as pl
from jax.experimental.pallas import tpu as pltpu

F32 = jnp.float32
BF16 = jnp.bfloat16
I32 = jnp.int32

D_MODEL = 2048
DEPTH = 4
ROPE_THETA = 500000.0
LN_EPS = 1e-5
RMS_EPS = 1e-6
MLA_HEADS = 8
MLA_NOPE = 128
MLA_ROPE = 64
MLA_V = 128
MLA_Q_RANK = 512
MLA_KV_RANK = 256
DIFF_HEADS = 8
DIFF_QK = 64
DIFF_V = 128
DIFF_ROT = DIFF_QK // 4
SWA_Q_HEADS = 32
SWA_KV_HEADS = 4
SWA_GROUP = SWA_Q_HEADS // SWA_KV_HEADS
SWA_HD = 64
WINDOW = 128
N_EXPERTS = 64
TOP_K = 8
D_EXPERT = 256
ROUTED_SCALE = 2.5
DEEPNORM_ALPHA = (2 * DEPTH) ** 0.25
LOG2E = math.log2(math.e)

LANES = 128
MLA_QK_PAD = 256
VMEM_LIMIT = 48 << 20
ATTN_TQ = 512
MLA_TQ = 1024
MLA_TK = 2048
ATTN_TK = 2048


def _params(sem):
    return pltpu.CompilerParams(dimension_semantics=sem, vmem_limit_bytes=VMEM_LIMIT)


def _dot(a, b):
    return jnp.dot(a, b, preferred_element_type=F32)


def _dot_nt(a, b):
    return lax.dot_general(a, b, (((1,), (1,)), ((), ())), preferred_element_type=F32)


def _tables_kernel(pos_ref, c_ref, out_ref):
    pos = pos_ref[...]
    a64 = pos * c_ref[0:1, :]
    a16 = pos * c_ref[3:4, :]
    s64 = jnp.sin(a64)
    s16 = jnp.sin(a16)
    out_ref[0] = jnp.cos(a64)
    out_ref[1] = s64 * c_ref[1:2, :]
    out_ref[2] = s64 * c_ref[2:3, :]
    out_ref[3] = jnp.cos(a16)
    out_ref[4] = s16 * c_ref[4:5, :]
    out_ref[5] = s16 * c_ref[5:6, :]


def _rope_consts():
    j = np.arange(LANES) % 64
    inv64 = ROPE_THETA ** (-jnp.arange(0, MLA_ROPE, 2, dtype=F32) / MLA_ROPE)
    inv16 = ROPE_THETA ** (-jnp.arange(0, DIFF_ROT, 2, dtype=F32) / DIFF_ROT)
    f64 = inv64[j % 32]
    f16 = jnp.where(j < 16, inv16[j % 8], 0.0)
    rows = [
        f64,
        jnp.asarray(np.where(j < 32, -1.0, 0.0), F32),
        jnp.asarray(np.where(j >= 32, 1.0, 0.0), F32),
        f16,
        jnp.asarray(np.where(j < 8, -1.0, 0.0), F32),
        jnp.asarray(np.where((j >= 8) & (j < 16), 1.0, 0.0), F32),
        jnp.zeros((LANES,), F32),
        jnp.zeros((LANES,), F32),
    ]
    return jnp.stack(rows).astype(F32)


def _rope_tables(positions):
    t = positions.size
    pos = jnp.broadcast_to(positions.reshape(t, 1).astype(F32), (t, LANES))
    tm = min(512, t)
    return pl.pallas_call(
        _tables_kernel,
        grid=(t // tm,),
        in_specs=[pl.BlockSpec((tm, LANES), lambda i: (i, 0)), pl.BlockSpec((8, LANES), lambda i: (0, 0))],
        out_specs=pl.BlockSpec((6, tm, LANES), lambda i: (0, i, 0)),
        out_shape=jax.ShapeDtypeStruct((6, t, LANES), F32),
        compiler_params=_params(("parallel",)),
        name="rope_tables",
    )(pos, _rope_consts())


def _rope_tile(x, c, sa, sb, half):
    return x * c + pltpu.roll(x, LANES - half, 1) * sa + pltpu.roll(x, half, 1) * sb


def _mm_kernel(x_ref, w_ref, o_ref):
    o_ref[...] = _dot(x_ref[...], w_ref[...]).astype(o_ref.dtype)


def _matmul(x, w, name, tm=512, tn=1024):
    m, k = x.shape
    n = w.shape[1]
    tm = min(tm, m)
    tn = min(tn, n)
    return pl.pallas_call(
        _mm_kernel,
        grid=(n // tn, m // tm),
        in_specs=[pl.BlockSpec((tm, k), lambda j, i: (i, 0)), pl.BlockSpec((k, tn), lambda j, i: (0, j))],
        out_specs=pl.BlockSpec((tm, tn), lambda j, i: (i, j)),
        out_shape=jax.ShapeDtypeStruct((m, n), BF16),
        compiler_params=_params(("parallel", "parallel")),
        name=name,
    )(x, w)


def _mm_rope_kernel(x_ref, w_ref, tab_ref, scale_ref, o_ref):
    acc = _dot(x_ref[...], w_ref[...])
    c, sa, sb = tab_ref[0], tab_ref[1], tab_ref[2]
    for j in range(o_ref.shape[1] // LANES):
        sl = slice(LANES * j, LANES * (j + 1))
        o_ref[:, sl] = (_rope_tile(acc[:, sl], c, sa, sb, DIFF_ROT // 2) * scale_ref[:, sl]).astype(o_ref.dtype)


def _matmul_rope(x, w, tab, scale, name, tm=512, tn=1024):
    m, k = x.shape
    n = w.shape[1]
    tm = min(tm, m)
    return pl.pallas_call(
        _mm_rope_kernel,
        grid=(n // tn, m // tm),
        in_specs=[
            pl.BlockSpec((tm, k), lambda j, i: (i, 0)),
            pl.BlockSpec((k, tn), lambda j, i: (0, j)),
            pl.BlockSpec((3, tm, LANES), lambda j, i: (1, i, 0)),
            pl.BlockSpec((1, tn), lambda j, i: (0, j)),
        ],
        out_specs=pl.BlockSpec((tm, tn), lambda j, i: (i, j)),
        out_shape=jax.ShapeDtypeStruct((m, n), BF16),
        compiler_params=_params(("parallel", "parallel")),
        name=name,
    )(x, w, tab, scale)


def _rms(x, g):
    return x * lax.rsqrt(jnp.mean(x * x, -1, keepdims=True) + RMS_EPS) * g


def _mla_prep_kernel(cq_ref, ckv_ref, tab_ref, qg_ref, kvg_ref, wqn_ref, wqr_ref, wkn_ref, wv_ref,
                     q_ref, k_ref, v_ref, *, scale):
    tm = cq_ref.shape[0]
    cq = cq_ref[...].astype(F32)
    ckv_all = ckv_ref[...].astype(F32)
    ckv = ckv_all[:, :MLA_KV_RANK]
    kr = ckv_all[:, MLA_KV_RANK:MLA_KV_RANK + LANES]
    cqn = _rms(cq, qg_ref[...]).astype(BF16)
    ckvn = _rms(ckv, kvg_ref[...]).astype(BF16)
    qn = _dot(cqn, wqn_ref[...]) * scale
    qr = _dot(cqn, wqr_ref[...]) * scale
    kn = _dot(ckvn, wkn_ref[...])
    vv = _dot(ckvn, wv_ref[...])
    c, sa, sb = tab_ref[0], tab_ref[1], tab_ref[2]
    lo = lax.broadcasted_iota(I32, (tm, LANES), 1) < MLA_ROPE
    half = MLA_ROPE // 2
    krr = jnp.where(lo, _rope_tile(kr, c, sa, sb, half), 0.0).astype(BF16)
    for j in range(MLA_HEADS // 2):
        r = _rope_tile(qr[:, LANES * j:LANES * (j + 1)], c, sa, sb, half)
        q_ref[0, 2 * j, :, LANES:] = jnp.where(lo, r, 0.0).astype(BF16)
        q_ref[0, 2 * j + 1, :, LANES:] = jnp.where(lo, pltpu.roll(r, MLA_ROPE, 1), 0.0).astype(BF16)
    for h in range(MLA_HEADS):
        sl = slice(LANES * h, LANES * (h + 1))
        q_ref[0, h, :, :LANES] = qn[:, sl].astype(BF16)
        k_ref[0, h, :, :LANES] = kn[:, sl].astype(BF16)
        k_ref[0, h, :, LANES:] = krr
        v_ref[0, h] = vv[:, sl].astype(BF16)


def _mla_prep(proj, tab, q_norm, kv_norm, wqn, wqr, wkn, wv, b, s, col_cq, col_ckv):
    tm = min(256, s)
    nt = s // tm
    h = MLA_HEADS
    row = lambda bi, i: bi * nt + i
    const = lambda bi, i: (0, 0)
    return pl.pallas_call(
        functools.partial(_mla_prep_kernel, scale=(MLA_NOPE + MLA_ROPE) ** -0.5 * LOG2E),
        grid=(b, nt),
        in_specs=[
            pl.BlockSpec((tm, MLA_Q_RANK), lambda bi, i: (row(bi, i), col_cq)),
            pl.BlockSpec((tm, 512), lambda bi, i: (row(bi, i), col_ckv)),
            pl.BlockSpec((3, tm, LANES), lambda bi, i: (0, row(bi, i), 0)),
            pl.BlockSpec((1, MLA_Q_RANK), const),
            pl.BlockSpec((1, MLA_KV_RANK), const),
            pl.BlockSpec(wqn.shape, const),
            pl.BlockSpec(wqr.shape, const),
            pl.BlockSpec(wkn.shape, const),
            pl.BlockSpec(wv.shape, const),
        ],
        out_specs=[
            pl.BlockSpec((1, h, tm, MLA_QK_PAD), lambda bi, i: (bi, 0, i, 0)),
            pl.BlockSpec((1, h, tm, MLA_QK_PAD), lambda bi, i: (bi, 0, i, 0)),
            pl.BlockSpec((1, h, tm, MLA_V), lambda bi, i: (bi, 0, i, 0)),
        ],
        out_shape=[
            jax.ShapeDtypeStruct((b, h, s, MLA_QK_PAD), BF16),
            jax.ShapeDtypeStruct((b, h, s, MLA_QK_PAD), BF16),
            jax.ShapeDtypeStruct((b, h, s, MLA_V), BF16),
        ],
        compiler_params=_params(("parallel", "parallel")),
        name="mla_prep",
    )(proj, proj, tab, q_norm, kv_norm, wqn, wqr, wkn, wv)


def _online_softmax_loop(qs, k_ats, v_at, nk):
    tq = qs[0].shape[0]

    def body(j, carry):
        v = v_at(j)
        out = []
        for q, k_at, (m, l, acc) in zip(qs, k_ats, carry):
            s = _dot_nt(q, k_at(j))
            m_new = jnp.maximum(m, jnp.max(s, axis=1, keepdims=True))
            alpha = jnp.exp2(m - m_new)
            p = jnp.exp2(s - m_new)
            l = alpha * l + jnp.sum(p, axis=1, keepdims=True)
            acc = alpha * acc + _dot(p.astype(BF16), v)
            out.append((m_new, l, acc))
        return tuple(out)

    one = (jnp.full((tq, 1), -jnp.inf, F32), jnp.zeros((tq, 1), F32), jnp.zeros((tq, v_at(0).shape[1]), F32))
    final = lax.fori_loop(0, nk, body, tuple(one for _ in qs), unroll=True)
    return [(l, acc) for _, l, acc in final]


def _mla_attn_kernel(q_ref, k_ref, v_ref, o_ref, *, tk):
    nk = k_ref.shape[2] // tk
    chunk = lambda j: pl.ds(pl.multiple_of(j * tk, tk), tk)
    (l, acc), = _online_softmax_loop(
        [q_ref[0, 0]], [lambda j: k_ref[0, 0, chunk(j), :]], lambda j: v_ref[0, 0, chunk(j), :], nk)
    o_ref[0] = (acc / l).astype(o_ref.dtype)


def _mla_attn(q, k, v):
    b, h, s, _ = q.shape
    tq = min(MLA_TQ, s)
    return pl.pallas_call(
        functools.partial(_mla_attn_kernel, tk=min(MLA_TK, s)),
        grid=(b, h, s // tq),
        in_specs=[
            pl.BlockSpec((1, 1, tq, MLA_QK_PAD), lambda bi, hi, i: (bi, hi, i, 0)),
            pl.BlockSpec((1, 1, s, MLA_QK_PAD), lambda bi, hi, i: (bi, hi, 0, 0)),
            pl.BlockSpec((1, 1, s, MLA_V), lambda bi, hi, i: (bi, hi, 0, 0)),
        ],
        out_specs=pl.BlockSpec((1, tq, MLA_V), lambda bi, hi, i: (bi, i, hi)),
        out_shape=jax.ShapeDtypeStruct((b, s, h * MLA_V), BF16),
        compiler_params=_params(("parallel", "parallel", "arbitrary")),
        name="mla_attn",
    )(q, k, v)


def _diff_attn_kernel(q1_ref, q2_ref, k1_ref, k2_ref, v_ref, lam_ref, g_ref, o_ref, *, tk, lam_init):
    hi = pl.program_id(1)
    tq = q1_ref.shape[0]
    nk = v_ref.shape[0] // tk
    chunk = lambda j: pl.ds(pl.multiple_of(j * tk, tk), tk)
    mine = (lax.broadcasted_iota(I32, (tq, LANES), 1) // DIFF_QK) == (hi % 2)
    zero = jnp.zeros((), BF16)
    qs = [jnp.where(mine, q1_ref[...], zero), jnp.where(mine, q2_ref[...], zero)]
    k_ats = [lambda j: k1_ref[chunk(j), :], lambda j: k2_ref[chunk(j), :]]
    (l1, a1), (l2, a2) = _online_softmax_loop(qs, k_ats, lambda j: v_ref[chunk(j), :], nk)
    lp = lam_ref[...]
    lam = (jnp.exp(jnp.sum(lp[0:1] * lp[1:2], axis=1, keepdims=True))
           - jnp.exp(jnp.sum(lp[2:3] * lp[3:4], axis=1, keepdims=True)) + lam_init)
    o = a1 / l1 - lam * (a2 / l2)
    o_ref[0] = (_rms(o, g_ref[...]) * (1.0 - lam_init)).astype(o_ref.dtype)


def _diff_attn(dqk, proj, col_v, lam_p, sub_norm, lam_init, b, s):
    h = DIFF_HEADS
    tq = min(ATTN_TQ, s)
    nq = s // tq
    pairs = h // 2
    qspec = lambda which: pl.BlockSpec((tq, LANES), lambda bi, hi, i: (bi * nq + i, which * pairs + hi // 2))
    kspec = lambda which: pl.BlockSpec((s, LANES), lambda bi, hi, i: (bi, which * pairs + hi // 2))
    const = lambda bi, hi, i: (0, 0)
    return pl.pallas_call(
        functools.partial(_diff_attn_kernel, tk=min(ATTN_TK, s), lam_init=lam_init),
        grid=(b, h, s // tq),
        in_specs=[
            qspec(0), qspec(1), kspec(2), kspec(3),
            pl.BlockSpec((s, DIFF_V), lambda bi, hi, i: (bi, col_v + hi)),
            pl.BlockSpec((4, DIFF_QK), const),
            pl.BlockSpec((1, DIFF_V), const),
        ],
        out_specs=pl.BlockSpec((1, tq, DIFF_V), lambda bi, hi, i: (bi, i, hi)),
        out_shape=jax.ShapeDtypeStruct((b, s, h * DIFF_V), BF16),
        compiler_params=_params(("parallel", "parallel", "arbitrary")),
        name="diff_attn",
    )(dqk, dqk, dqk, dqk, proj, lam_p, sub_norm)


def _swa_attn_kernel(sink_ref, bias_ref, q_ref, kp_ref, km_ref, kn_ref, vp_ref, vm_ref, vn_ref, o_ref):
    g = pl.program_id(1)
    tq = q_ref.shape[0]
    k = jnp.concatenate([kp_ref[...], km_ref[...], kn_ref[...]], axis=0)
    v = jnp.concatenate([vp_ref[...], vm_ref[...], vn_ref[...]], axis=0)
    bias = bias_ref[0]
    lane = lax.broadcasted_iota(I32, (tq, LANES), 1)
    zero = jnp.zeros((), BF16)
    qs = []
    for h in range(SWA_GROUP):
        qp = q_ref[:, LANES * (h // 2):LANES * (h // 2 + 1)]
        qs.append(jnp.where((lane // SWA_HD) == h % 2, qp, zero))
    s_all = _dot_nt(jnp.concatenate(qs, axis=0), k)
    ps, denoms = [], []
    for h in range(SWA_GROUP):
        sink = sink_ref[g * SWA_GROUP + h] * LOG2E
        sc = s_all[h * tq:(h + 1) * tq] + bias
        m = jnp.maximum(jnp.max(sc, axis=1, keepdims=True), sink)
        p = jnp.exp2(sc - m)
        denoms.append(jnp.sum(p, axis=1, keepdims=True) + jnp.exp2(sink - m))
        ps.append(p.astype(BF16))
    o_all = _dot(jnp.concatenate(ps, axis=0), v)
    for pair in range(SWA_GROUP // 2):
        lo = o_all[2 * pair * tq:(2 * pair + 1) * tq] / denoms[2 * pair]
        hi = o_all[(2 * pair + 1) * tq:(2 * pair + 2) * tq] / denoms[2 * pair + 1]
        o_ref[:, LANES * pair:LANES * (pair + 1)] = jnp.where(lane < SWA_HD, lo, hi).astype(o_ref.dtype)


def _swa_bias(tq):
    qi = np.arange(tq)[:, None]
    kj = np.arange(tq + 2 * WINDOW)[None, :] - WINDOW
    band = np.abs(kj - qi) <= WINDOW
    masks = [band, band & (kj >= 0), band & (kj < tq), band & (kj >= 0) & (kj < tq)]
    return jnp.asarray(np.where(np.stack(masks), 0.0, -np.inf), F32)


def _swa_attn(qk, proj, sink, b, s, col_k, col_v, tq=256):
    tq = min(tq, s)
    nt = s // tq
    r = tq // WINDOW
    nwb = s // WINDOW
    gw = SWA_GROUP * SWA_HD
    row = lambda bi, i: bi * nt + i
    prev = lambda bi, i: bi * nwb + jnp.maximum(i * r - 1, 0)
    nxt = lambda bi, i: bi * nwb + jnp.minimum((i + 1) * r, nwb - 1)
    edge = lambda i: jnp.where(i == 0, 1, 0) + jnp.where(i == nt - 1, 2, 0)
    grid_spec = pltpu.PrefetchScalarGridSpec(
        num_scalar_prefetch=1,
        grid=(b, SWA_KV_HEADS, nt),
        in_specs=[
            pl.BlockSpec((1, tq, tq + 2 * WINDOW), lambda bi, g, i, sk: (edge(i), 0, 0)),
            pl.BlockSpec((tq, gw), lambda bi, g, i, sk: (row(bi, i), g)),
            pl.BlockSpec((WINDOW, LANES), lambda bi, g, i, sk: (prev(bi, i), col_k + g)),
            pl.BlockSpec((tq, LANES), lambda bi, g, i, sk: (row(bi, i), col_k + g)),
            pl.BlockSpec((WINDOW, LANES), lambda bi, g, i, sk: (nxt(bi, i), col_k + g)),
            pl.BlockSpec((WINDOW, LANES), lambda bi, g, i, sk: (prev(bi, i), col_v + g)),
            pl.BlockSpec((tq, LANES), lambda bi, g, i, sk: (row(bi, i), col_v + g)),
            pl.BlockSpec((WINDOW, LANES), lambda bi, g, i, sk: (nxt(bi, i), col_v + g)),
        ],
        out_specs=pl.BlockSpec((tq, gw), lambda bi, g, i, sk: (row(bi, i), g)),
    )
    return pl.pallas_call(
        _swa_attn_kernel,
        grid_spec=grid_spec,
        out_shape=jax.ShapeDtypeStruct((b * s, SWA_Q_HEADS * SWA_HD), BF16),
        compiler_params=_params(("parallel", "parallel", "parallel")),
        name="swa_attn",
    )(sink, _swa_bias(tq), qk, qk, qk, qk, proj, proj, proj)


def _layer_norm(y, g, b):
    mu = jnp.mean(y, -1, keepdims=True)
    d = y - mu
    var = jnp.mean(d * d, -1, keepdims=True)
    return d * lax.rsqrt(var + LN_EPS) * g + b


PACK_SUBLANES = D_MODEL // 2 // LANES
HI_MASK = -65536


def _pack_words(lo, hi):
    lb = lax.bitcast_convert_type(lo.astype(BF16).astype(F32), I32)
    hb = lax.bitcast_convert_type(hi.astype(BF16).astype(F32), I32)
    return (hb & HI_MASK) | lax.shift_right_logical(lb, 16)


def _unpack_words(w):
    return lax.bitcast_convert_type(w << 16, F32), lax.bitcast_convert_type(w & HI_MASK, F32)


def _word_rows(j, tm, lead=(), base=0):
    return lead + (pl.ds(base * PACK_SUBLANES + j, tm, stride=PACK_SUBLANES), slice(None))


def _store_packed(ref, y, keep=None, clear=None):
    tm, d = y.shape
    for j in range(PACK_SUBLANES):
        w = _pack_words(y[:, LANES * j:LANES * (j + 1)], y[:, d // 2 + LANES * j:d // 2 + LANES * (j + 1)])
        if keep is not None:
            w = jnp.where(keep, w, jnp.where(clear, 0, ref[_word_rows(j, tm)]))
        ref[_word_rows(j, tm)] = w


def _load_packed_bf16(ref, tm, base=0):
    los, his = [], []
    for j in range(PACK_SUBLANES):
        lo, hi = _unpack_words(ref[_word_rows(j, tm, base=base)])
        los.append(lo.astype(BF16))
        his.append(hi.astype(BF16))
    return jnp.concatenate(los + his, axis=1)


def _out_ln_kernel(*refs, n_in):
    o_refs = refs[:n_in]
    w_refs = refs[n_in:2 * n_in]
    h_ref, g_ref, b_ref, hf_ref, hp_ref = refs[2 * n_in:]
    mix = _dot(o_refs[0][...], w_refs[0][...])
    for o_ref, w_ref in zip(o_refs[1:], w_refs[1:]):
        mix = mix + _dot(o_ref[...], w_ref[...])
    out = _layer_norm(DEEPNORM_ALPHA * h_ref[...] + mix, g_ref[...], b_ref[...])
    hf_ref[...] = out
    _store_packed(hp_ref, out)


def _out_ln(os_, ws, h, g, b):
    t, d = h.shape
    tm = min(256, t)
    n_in = len(os_)
    const = lambda i: (0, 0)
    rowblk = lambda a: pl.BlockSpec((tm, a.shape[1]), lambda i: (i, 0))
    return pl.pallas_call(
        functools.partial(_out_ln_kernel, n_in=n_in),
        grid=(t // tm,),
        in_specs=[rowblk(o) for o in os_] + [pl.BlockSpec(w.shape, const) for w in ws]
        + [rowblk(h), pl.BlockSpec((1, d), const), pl.BlockSpec((1, d), const)],
        out_specs=[pl.BlockSpec((tm, d), lambda i: (i, 0)),
                   pl.BlockSpec((tm * PACK_SUBLANES, LANES), lambda i: (i, 0))],
        out_shape=[jax.ShapeDtypeStruct((t, d), F32), jax.ShapeDtypeStruct((t * PACK_SUBLANES, LANES), I32)],
        compiler_params=_params(("parallel",)),
        name="out_proj_ln",
    )(*os_, *ws, h, g, b)


def _router_kernel(h_ref, wh_ref, wl_ref, bias_ref, idx_ref, gate_ref, rank_ref, cnt_ref, cnt_scr):
    i = pl.program_id(0)
    tm = h_ref.shape[0]

    @pl.when(i == 0)
    def _():
        cnt_scr[...] = jnp.zeros(cnt_scr.shape, F32)

    h = h_ref[...]
    hh = h.astype(BF16)
    hl = (h - hh.astype(F32)).astype(BF16)
    logits = _dot(hh, wh_ref[...]) + (_dot(hh, wl_ref[...]) + _dot(hl, wh_ref[...]))
    scores = jax.nn.sigmoid(logits.T[:N_EXPERTS])
    sel = scores + bias_ref[...]
    e_iota = lax.broadcasted_iota(I32, (N_EXPERTS, tm), 0).astype(F32)
    k_iota = lax.broadcasted_iota(I32, (TOP_K, tm), 0)
    onehots, gates = [], []
    idx_out = jnp.zeros((TOP_K, tm), F32)
    for k in range(TOP_K):
        m = jnp.max(sel, axis=0, keepdims=True)
        pick = jnp.min(jnp.where(sel == m, e_iota, float(N_EXPERTS)), axis=0, keepdims=True)
        oh = e_iota == pick
        onehots.append(oh)
        gates.append(jnp.sum(jnp.where(oh, scores, 0.0), axis=0, keepdims=True))
        idx_out = jnp.where(k_iota == k, pick, idx_out)
        sel = jnp.where(oh, -jnp.inf, sel)
    maskf = onehots[0].astype(F32)
    for oh in onehots[1:]:
        maskf = maskf + oh.astype(F32)
    tri = (lax.broadcasted_iota(I32, (tm, tm), 0) < lax.broadcasted_iota(I32, (tm, tm), 1)).astype(BF16)
    ranks = cnt_scr[...] + _dot(maskf.astype(BF16), tri)
    gsum = gates[0]
    for gk in gates[1:]:
        gsum = gsum + gk
    gate_out = jnp.zeros((TOP_K, tm), F32)
    rank_out = jnp.zeros((TOP_K, tm), F32)
    for k in range(TOP_K):
        gate_out = jnp.where(k_iota == k, gates[k] / gsum * ROUTED_SCALE, gate_out)
        rk = jnp.sum(jnp.where(onehots[k], ranks, 0.0), axis=0, keepdims=True)
        rank_out = jnp.where(k_iota == k, rk, rank_out)
    idx_ref[...] = idx_out.astype(I32)
    gate_ref[...] = gate_out
    rank_ref[...] = rank_out.astype(I32)
    total = cnt_scr[...] + jnp.sum(maskf, axis=1, keepdims=True)
    cnt_scr[...] = total
    cnt_ref[...] = total


def _router(h, w_hi, w_lo, bias):
    t, d = h.shape
    tm = min(256, t)
    const = lambda i: (0, 0)
    out = lambda: pl.BlockSpec((TOP_K, tm), lambda i: (0, i))
    return pl.pallas_call(
        _router_kernel,
        grid=(t // tm,),
        in_specs=[
            pl.BlockSpec((tm, d), lambda i: (i, 0)),
            pl.BlockSpec((d, LANES), const),
            pl.BlockSpec((d, LANES), const),
            pl.BlockSpec((N_EXPERTS, 1), const),
        ],
        out_specs=[out(), out(), out(), pl.BlockSpec((N_EXPERTS, 1), const)],
        out_shape=[
            jax.ShapeDtypeStruct((TOP_K, t), I32),
            jax.ShapeDtypeStruct((TOP_K, t), F32),
            jax.ShapeDtypeStruct((TOP_K, t), I32),
            jax.ShapeDtypeStruct((N_EXPERTS, 1), F32),
        ],
        scratch_shapes=[pltpu.VMEM((N_EXPERTS, 1), F32)],
        compiler_params=_params(("arbitrary",)),
        name="router",
    )(h, w_hi, w_lo, bias)


def _token_tile(ref, r, lead=()):
    return ref.at[lead + (pl.ds(pl.multiple_of(r * PACK_SUBLANES, PACK_SUBLANES), PACK_SUBLANES),)]


def _dispatch_kernel(dest_ref, x_ref, xs_ref, sem):
    tm = x_ref.shape[0] // PACK_SUBLANES

    def body(r, carry):
        src = _token_tile(x_ref, r)
        for k in range(TOP_K):
            pltpu.make_async_copy(src, xs_ref.at[dest_ref[r * TOP_K + k]], sem).start(priority=k % 2)
        return carry

    lax.fori_loop(0, tm, body, 0)
    for k in range(TOP_K):
        pltpu.make_async_copy(xs_ref.at[pl.ds(0, tm)], xs_ref.at[pl.ds(0, tm)], sem).wait()


def _dispatch(hp, dest_flat):
    t = hp.shape[0] // PACK_SUBLANES
    tm = min(256, t)
    return pl.pallas_call(
        _dispatch_kernel,
        grid=(t // tm,),
        in_specs=[
            pl.BlockSpec((tm * TOP_K,), lambda i: (i,), memory_space=pltpu.SMEM),
            pl.BlockSpec((tm * PACK_SUBLANES, LANES), lambda i: (i, 0)),
        ],
        out_specs=pl.BlockSpec(memory_space=pl.ANY),
        out_shape=jax.ShapeDtypeStruct((t * TOP_K, PACK_SUBLANES, LANES), I32),
        scratch_shapes=[pltpu.SemaphoreType.DMA(())],
        compiler_params=_params(("arbitrary",)),
        name="moe_dispatch",
    )(dest_flat, hp)


def _expert_kernel(tile_ref, exp_ref, lo_ref, hi_ref, first_ref, x_ref, wg_ref, wu_ref, wd_ref, y_ref,
                   wg_bf, wu_bf, wd_bf):
    w = pl.program_id(0)
    tm = x_ref.shape[0] // PACK_SUBLANES
    lo = lo_ref[w]
    hi = hi_ref[w]

    @pl.when((w == 0) | (exp_ref[w] != exp_ref[jnp.maximum(w - 1, 0)]))
    def _():
        wg_bf[...] = wg_ref[0, 0].astype(BF16)
        wu_bf[...] = wu_ref[0, 0].astype(BF16)
        wd_bf[...] = wd_ref[0, 0].astype(BF16)

    @pl.when(hi > lo)
    def _():
        x = _load_packed_bf16(x_ref, tm)
        gate = _dot(x, wg_bf[...])
        up = _dot(x, wu_bf[...])
        hmid = (gate * jax.nn.sigmoid(gate) * up).astype(BF16)
        y = _dot(hmid, wd_bf[...])
        rows = tile_ref[w] * tm + lax.broadcasted_iota(I32, (tm, 1), 0)
        mine = (rows >= lo) & (rows < hi)
        _store_packed(y_ref, y, keep=mine, clear=first_ref[w] == 1)


def _experts(xs, wg, wu, wd, layer, tile_w, exp_w, lo_w, hi_w, first_w, tm):
    d = D_MODEL
    rows = pl.BlockSpec((tm * PACK_SUBLANES, LANES), lambda w, tl, ex, lo, hi, fi: (tl[w], 0))
    nw = tile_w.shape[0]
    grid_spec = pltpu.PrefetchScalarGridSpec(
        num_scalar_prefetch=5,
        grid=(nw,),
        in_specs=[
            rows,
            pl.BlockSpec((1, 1, d, D_EXPERT), lambda w, tl, ex, lo, hi, fi: (layer, ex[w], 0, 0)),
            pl.BlockSpec((1, 1, d, D_EXPERT), lambda w, tl, ex, lo, hi, fi: (layer, ex[w], 0, 0)),
            pl.BlockSpec((1, 1, D_EXPERT, d), lambda w, tl, ex, lo, hi, fi: (layer, ex[w], 0, 0)),
        ],
        out_specs=rows,
        scratch_shapes=[pltpu.VMEM((d, D_EXPERT), BF16), pltpu.VMEM((d, D_EXPERT), BF16),
                        pltpu.VMEM((D_EXPERT, d), BF16)],
    )
    return pl.pallas_call(
        _expert_kernel,
        grid_spec=grid_spec,
        out_shape=jax.ShapeDtypeStruct(xs.shape, I32),
        compiler_params=_params(("arbitrary",)),
        name="moe_experts",
    )(tile_w, exp_w, lo_w, hi_w, first_w, xs, wg, wu, wd)


COMBINE_HALF = 128


def _combine_kernel(dcur_ref, dnext_ref, ys_ref, gate_ref, h_ref, hp_ref, sg_ref, su_ref, sd_ref, g_ref, b_ref,
                    hf_out, hb_out, buf0, buf1, sem0, sem1):
    i = pl.program_id(0)
    th = buf0.shape[1] // PACK_SUBLANES

    def gather(dref, base, buf, sem):
        for r in range(th):
            for k in range(TOP_K):
                src = ys_ref.at[dref[(base + r) * TOP_K + k]]
                pltpu.make_async_copy(src, _token_tile(buf, r, (k,)), sem).start(priority=1)

    def wait(sem):
        for k in range(TOP_K):
            pltpu.make_async_copy(ys_ref.at[pl.ds(0, th)], ys_ref.at[pl.ds(0, th)], sem).wait()

    def reduce_half(half, buf):
        rows = pl.ds(half * th, th)
        x = _load_packed_bf16(hp_ref, th, base=half * th)
        gate = _dot(x, sg_ref[...])
        up = _dot(x, su_ref[...])
        acc = _dot((gate * jax.nn.sigmoid(gate) * up).astype(BF16), sd_ref[...])
        acc = acc + DEEPNORM_ALPHA * h_ref[rows, :]
        gates = gate_ref[rows, :]
        los = [None] * PACK_SUBLANES
        his = [None] * PACK_SUBLANES
        for k in range(TOP_K):
            gk = gates[:, k:k + 1]
            for j in range(PACK_SUBLANES):
                lo, hi = _unpack_words(buf[_word_rows(j, th, (k,))])
                los[j] = gk * lo if k == 0 else los[j] + gk * lo
                his[j] = gk * hi if k == 0 else his[j] + gk * hi
        out = _layer_norm(acc + jnp.concatenate(los + his, axis=1), g_ref[...], b_ref[...])
        hf_out[rows, :] = out
        hb_out[rows, :] = out.astype(BF16)

    @pl.when(i == 0)
    def _():
        def body(r, carry):
            for k in range(TOP_K):
                src = ys_ref.at[dcur_ref[r * TOP_K + k]]
                pltpu.make_async_copy(src, _token_tile(buf0, r, (k,)), sem0).start(priority=k % 2)
            return carry

        lax.fori_loop(0, th, body, 0)

    wait(sem0)
    gather(dcur_ref, th, buf1, sem1)
    reduce_half(0, buf0)
    wait(sem1)
    gather(dnext_ref, 0, buf0, sem0)
    reduce_half(1, buf1)

    @pl.when(i == pl.num_programs(0) - 1)
    def _():
        wait(sem0)


def _combine(dest_flat, ys, gate, h, hp, sg, su, sd, g, b):
    t, d = h.shape
    th = min(COMBINE_HALF, t // 2)
    tm = 2 * th
    n = t // tm
    const = lambda i: (0, 0)
    row = lambda width: pl.BlockSpec((tm, width), lambda i: (i, 0))
    buf = pltpu.VMEM((TOP_K, th * PACK_SUBLANES, LANES), I32)
    return pl.pallas_call(
        _combine_kernel,
        grid=(n,),
        in_specs=[
            pl.BlockSpec((tm * TOP_K,), lambda i: (i,), memory_space=pltpu.SMEM),
            pl.BlockSpec((th * TOP_K,), lambda i: (jnp.minimum(2 * i + 2, 2 * n - 1),), memory_space=pltpu.SMEM),
            pl.BlockSpec(memory_space=pl.ANY),
            row(TOP_K), row(d), pl.BlockSpec((tm * PACK_SUBLANES, LANES), lambda i: (i, 0)),
            pl.BlockSpec(sg.shape, const), pl.BlockSpec(su.shape, const), pl.BlockSpec(sd.shape, const),
            pl.BlockSpec((1, d), const), pl.BlockSpec((1, d), const),
        ],
        out_specs=[row(d), row(d)],
        out_shape=[jax.ShapeDtypeStruct((t, d), F32), jax.ShapeDtypeStruct((t, d), BF16)],
        scratch_shapes=[buf, buf, pltpu.SemaphoreType.DMA(()), pltpu.SemaphoreType.DMA(())],
        compiler_params=_params(("arbitrary",)),
        name="moe_combine",
    )(dest_flat, dest_flat, ys, gate, h, hp, sg, su, sd, g, b)


EXPERT_TILE = 512


def _group_metadata(counts, idx, rank, n_assign):
    tm = min(EXPERT_TILE, n_assign)
    nt = n_assign // tm
    nw = nt + N_EXPERTS - 1
    sizes = counts.reshape(N_EXPERTS).astype(I32)
    ends = jnp.cumsum(sizes)
    starts = ends - sizes
    my_start = jnp.sum(jnp.where(idx[..., None] == jnp.arange(N_EXPERTS, dtype=I32), starts, 0), axis=-1)
    dest = (my_start + rank).reshape(-1).astype(I32)
    first_tile = starts // tm
    n_e = jnp.where(sizes > 0, (ends - 1) // tm - first_tile + 1, 0)
    cum = jnp.cumsum(n_e)
    off = cum - n_e
    total = cum[-1]
    w = jnp.arange(nw, dtype=I32)
    valid = w < total
    experts = jnp.arange(N_EXPERTS, dtype=I32)
    last_used = jnp.max(jnp.where(n_e > 0, experts, 0))
    e_w = jnp.where(valid, jnp.sum(cum[None, :] <= w[:, None], axis=1), last_used).astype(I32)
    pick = lambda table: jnp.sum(jnp.where(e_w[:, None] == experts, table, 0), axis=1)
    tile_w = jnp.where(valid, pick(first_tile) + (w - pick(off)), nt - 1).astype(I32)
    lo_w = jnp.where(valid, jnp.maximum(pick(starts), tile_w * tm), 0).astype(I32)
    hi_w = jnp.where(valid, jnp.minimum(pick(ends), (tile_w + 1) * tm), 0).astype(I32)
    first_w = jnp.concatenate([jnp.ones((1,), I32), (tile_w[1:] != tile_w[:-1]).astype(I32)])
    return dest, tile_w, e_w, lo_w, hi_w, first_w, tm


def _moe(h, hp, r_hi, r_lo, r_bias, wg, wu, wd, layer, sg, su, sd, g, b):
    t = h.shape[0]
    a = t * TOP_K
    idx, gate, rank, counts = _router(h, r_hi, r_lo, r_bias)
    dest, tile_w, e_w, lo_w, hi_w, first_w, tm = _group_metadata(counts, idx.T, rank.T, a)
    gate = gate.T
    xs = _dispatch(hp, dest)
    ys = _experts(xs.reshape(a * PACK_SUBLANES, LANES), wg, wu, wd, layer, tile_w, e_w, lo_w, hi_w, first_w, tm)
    return _combine(dest, ys.reshape(a, PACK_SUBLANES, LANES), gate, h, hp, sg, su, sd, g, b)


def _even_weights(w_in, w_uq, w_ukv, w_out):
    o1 = MLA_Q_RANK
    o2 = o1 + MLA_KV_RANK
    o3 = o2 + MLA_ROPE
    o4 = o3 + 4 * DIFF_HEADS * DIFF_QK
    c_q, c_kv, k_r, dq, dv = w_in[:, :o1], w_in[:, o1:o2], w_in[:, o2:o3], w_in[:, o3:o4], w_in[:, o4:]
    dq = dq.reshape(D_MODEL, DIFF_HEADS, 4, DIFF_QK).transpose(0, 2, 1, 3).reshape(D_MODEL, -1)
    pad = jnp.zeros((D_MODEL, 512 - MLA_KV_RANK - MLA_ROPE), w_in.dtype)
    w_proj = jnp.concatenate([dq, dv, c_q, c_kv, k_r, pad], axis=1).astype(BF16)
    uq = w_uq.reshape(MLA_Q_RANK, MLA_HEADS, MLA_NOPE + MLA_ROPE)
    wqn = uq[:, :, :MLA_NOPE].reshape(MLA_Q_RANK, -1).astype(BF16)
    wqr = uq[:, :, MLA_NOPE:].reshape(MLA_Q_RANK, -1).astype(BF16)
    ukv = w_ukv.reshape(MLA_KV_RANK, MLA_HEADS, MLA_NOPE + MLA_V)
    wkn = ukv[:, :, :MLA_NOPE].reshape(MLA_KV_RANK, -1).astype(BF16)
    wv = ukv[:, :, MLA_NOPE:].reshape(MLA_KV_RANK, -1).astype(BF16)
    n_mla = MLA_HEADS * MLA_V
    return w_proj, wqn, wqr, wkn, wv, w_out[:n_mla].astype(BF16), w_out[n_mla:].astype(BF16)


def _odd_weights(w_in, w_out):
    nq = SWA_Q_HEADS * SWA_HD
    nkv = SWA_KV_HEADS * SWA_HD
    dup = lambda w: jnp.concatenate([w.reshape(D_MODEL, SWA_KV_HEADS, 1, SWA_HD)] * 2, axis=2).reshape(D_MODEL, -1)
    w_proj = jnp.concatenate([w_in[:, :nq], dup(w_in[:, nq:nq + nkv]), dup(w_in[:, nq + nkv:])], axis=1)
    return w_proj.astype(BF16), w_out.astype(BF16)


def kernel(x, positions, ab_w_in, mla_q_norm, mla_w_uq, mla_kv_norm, mla_w_ukv, diff_lambda_q1, diff_lambda_k1, diff_lambda_q2, diff_lambda_k2, diff_sub_norm, ab_w_out, swa_w_in, swa_sink, swa_w_out, mix_ln_g, mix_ln_b, ffn_ln_g, ffn_ln_b, router_w, router_bias, exp_w_gate, exp_w_up, exp_w_down, shared_w_gate, shared_w_up, shared_w_down):
    b, s, d = x.shape
    t = b * s
    tab = _rope_tables(positions)
    h = x.reshape(t, d)
    hb = h.astype(BF16)
    row = lambda v: v.reshape(1, -1)
    qk_scale = lambda n_q, n: jnp.where(jnp.arange(n) < n_q, DIFF_QK ** -0.5 * LOG2E, 1.0).astype(F32).reshape(1, n)
    for layer in range(DEPTH):
        i = layer // 2
        if layer % 2 == 0:
            w_proj, wqn, wqr, wkn, wv, wo_mla, wo_diff = _even_weights(
                ab_w_in[i], mla_w_uq[i], mla_w_ukv[i], ab_w_out[i])
            n_dq = 4 * DIFF_HEADS * DIFF_QK
            dqk = _matmul_rope(hb, w_proj[:, :n_dq], tab, qk_scale(n_dq // 2, n_dq), "even_in_proj_qk")
            proj = _matmul(hb, w_proj[:, n_dq:], "even_in_proj")
            q, k, v = _mla_prep(proj, tab, row(mla_q_norm[i]), row(mla_kv_norm[i]), wqn, wqr, wkn, wv,
                                b, s, col_cq=2, col_ckv=3)
            o_mla = _mla_attn(q, k, v).reshape(t, -1)
            lam_p = jnp.stack([diff_lambda_q1[i], diff_lambda_k1[i], diff_lambda_q2[i], diff_lambda_k2[i]])
            lam_init = 0.8 - 0.6 * math.exp(-0.3 * layer)
            o_diff = _diff_attn(dqk, proj, 0, lam_p, row(diff_sub_norm[i]), lam_init, b, s).reshape(t, -1)
            h, hp = _out_ln([o_mla, o_diff], [wo_mla, wo_diff], h, row(mix_ln_g[layer]), row(mix_ln_b[layer]))
        else:
            w_proj, wo = _odd_weights(swa_w_in[i], swa_w_out[i])
            nq = SWA_Q_HEADS * SWA_HD
            n_rot = nq + SWA_KV_HEADS * LANES
            qk = _matmul_rope(hb, w_proj[:, :n_rot], tab, qk_scale(nq, n_rot), "odd_in_proj_qk", tn=n_rot // 2)
            vdup = _matmul(hb, w_proj[:, n_rot:], "odd_in_proj")
            o = _swa_attn(qk, vdup, swa_sink[i], b, s, col_k=nq // LANES, col_v=0)
            h, hp = _out_ln([o], [wo], h, row(mix_ln_g[layer]), row(mix_ln_b[layer]))
        rw = jnp.pad(router_w[layer], ((0, 0), (0, LANES - N_EXPERTS)))
        r_hi = rw.astype(BF16)
        r_lo = (rw - r_hi.astype(F32)).astype(BF16)
        h, hb = _moe(h, hp, r_hi, r_lo, router_bias[layer].reshape(N_EXPERTS, 1),
                     exp_w_gate, exp_w_up, exp_w_down, layer,
                     shared_w_gate[layer].astype(BF16), shared_w_up[layer].astype(BF16),
                     shared_w_down[layer].astype(BF16), row(ffn_ln_g[layer]), row(ffn_ln_b[layer]))
    return h.reshape(b, s, d)
```
